```python
import jax
import jax.numpy as jnp
from jax import lax
import numpy as np

D_MODEL = 1024
BATCH = 8
SEQ = 4096
DEPTH = 4

HEAD_DIM = 64
CHUNK = 64
MIX_HALF = D_MODEL // 2
N_EVEN = (DEPTH + 1) // 2
N_ODD = DEPTH // 2
NORM_EPS = 1e-6

RWKV_HEADS = MIX_HALF // HEAD_DIM
RWKV_DECAY_RANK = 64
RWKV_ICLR_RANK = 64
RWKV_GATE_RANK = 128
RWKV_IN = 3 * MIX_HALF + RWKV_DECAY_RANK + RWKV_ICLR_RANK + RWKV_GATE_RANK
RWKV_GN_EPS = 64e-5
RWKV_DECAY_SCALE = float(np.exp(-0.5))

GLA_HEADS = 4
GLA_DV = MIX_HALF // GLA_HEADS
GLA_DK = GLA_DV // 2
GLA_KEY = GLA_HEADS * GLA_DK
GLA_GATE_RANK = 16
GLA_GATE_NORM = 16.0
GLA_IN = 2 * GLA_KEY + MIX_HALF + GLA_GATE_RANK + MIX_HALF

HGRN_HEADS = 4
HGRN_DV = MIX_HALF // HGRN_HEADS
HGRN_DK = HGRN_DV // 2
HGRN_KEY = HGRN_HEADS * HGRN_DK
HGRN_IN = 2 * HGRN_KEY + 2 * MIX_HALF

SSD_HEADS = MIX_HALF // HEAD_DIM
SSD_GROUPS = 2
SSD_STATE = 64
SSD_CONV = 4
SSD_XBC = MIX_HALF + 2 * SSD_GROUPS * SSD_STATE
SSD_IN = MIX_HALF + SSD_XBC + SSD_HEADS

EVEN_IN = RWKV_IN + GLA_IN
ODD_IN = HGRN_IN + SSD_IN

N_EXPERTS = 16
N_EXPERT_GROUPS = 4
EXPERTS_PER_GROUP = N_EXPERTS // N_EXPERT_GROUPS
TOP_K = 2
D_EXPERT = D_MODEL // 2
MOE_BLOCK = 128

kernel_name = 'hybrid_rwkv7_gla_hgrn2_ssd_grouped_moe_adaln'


def split_cols(u, sizes):
    idx = [int(s) for s in np.cumsum(sizes)[:-1]]
    return jnp.split(u, idx, axis=-1)


def rms_norm(x, gain):
    xf = x.astype(jnp.float32)
    y = xf * lax.rsqrt(jnp.mean(xf * xf, axis=-1, keepdims=True) + NORM_EPS)
    return y.astype(x.dtype) * gain


def token_shift(u):
    return jnp.pad(u[:, :-1], ((0, 0), (1, 0), (0, 0)))


def causal_depthwise_conv(x, w, b):
    y = lax.conv_general_dilated(x, w[:, None, :], window_strides=(1,), padding=[(SSD_CONV - 1, 0)],
                                 dimension_numbers=('NWC', 'WIO', 'NWC'), feature_group_count=x.shape[-1])
    return y + b


def to_chunks(t):
    bsz, s, h, d = t.shape
    return t.reshape(bsz, s // CHUNK, CHUNK, h, d).transpose(1, 0, 3, 2, 4)


def chunked_gated_linear_attention(q, k, v, log_decay):
    bsz, s, h, dk = q.shape
    dv = v.shape[-1]
    causal = jnp.tril(jnp.ones((CHUNK, CHUNK), dtype=bool))

    def step(state, inp):
        qc, kc, vc, gc = inp
        b = jnp.cumsum(gc, axis=2)
        o_inter = jnp.einsum('bhid,bhdv->bhiv', qc * jnp.exp(b), state)
        diff = jnp.where(causal[:, :, None], b[:, :, :, None, :] - b[:, :, None, :, :], -jnp.inf)
        scores = jnp.einsum('bhid,bhjd,bhijd->bhij', qc, kc, jnp.exp(diff))
        o_intra = jnp.einsum('bhij,bhjv->bhiv', scores, vc)
        b_last = b[:, :, -1:, :]
        state = (jnp.exp(b_last[:, :, 0, :])[..., None] * state
                 + jnp.einsum('bhjd,bhjv->bhdv', kc * jnp.exp(b_last - b), vc))
        return state, o_inter + o_intra

    state0 = jnp.zeros((bsz, h, dk, dv), jnp.float32)
    _, o = lax.scan(step, state0, (to_chunks(q), to_chunks(k), to_chunks(v), to_chunks(log_decay)))
    return o.transpose(1, 0, 3, 2, 4).reshape(bsz, s, h, dv)


def rwkv7_recurrence(r, w, kk, a, kt, v):
    bsz, s, h, n = r.shape

    def step(state, inp):
        r_t, w_t, kk_t, a_t, kt_t, v_t = inp
        removed = jnp.einsum('bhvk,bhk->bhv', state, kk_t)
        state = (state * w_t[:, :, None, :]
                 - removed[..., None] * (kk_t * a_t)[:, :, None, :]
                 + v_t[..., None] * kt_t[:, :, None, :])
        return state, jnp.einsum('bhvk,bhk->bhv', state, r_t)

    xs = tuple(t.transpose(1, 0, 2, 3) for t in (r, w, kk, a, kt, v))
    _, y = lax.scan(step, jnp.zeros((bsz, h, n, n), jnp.float32), xs)
    return y.transpose(1, 0, 2, 3)


def ssd_chunked(xh, dt, a_neg, bm, cm):
    bsz, s, h, p = xh.shape
    g, n = bm.shape[2], bm.shape[3]
    rep = h // g
    nc = s // CHUNK
    causal = jnp.tril(jnp.ones((CHUNK, CHUNK), dtype=bool))
    x = (xh * dt[..., None]).reshape(bsz, nc, CHUNK, g, rep, p)
    bc = bm.reshape(bsz, nc, CHUNK, g, n)
    cc = cm.reshape(bsz, nc, CHUNK, g, n)
    da = (dt * a_neg).reshape(bsz, nc, CHUNK, g, rep).transpose(0, 1, 3, 4, 2)
    cum = jnp.cumsum(da, axis=-1)
    seg = jnp.exp(jnp.where(causal, cum[..., :, None] - cum[..., None, :], -jnp.inf))
    cb = jnp.einsum('bclgn,bcsgn->bcgls', cc, bc)
    y_diag = jnp.einsum('bcgls,bcgrls,bcsgrp->bclgrp', cb, seg, x)
    decay_to_end = jnp.exp(cum[..., -1:] - cum)
    chunk_states = jnp.einsum('bcsgn,bcgrs,bcsgrp->bcgrpn', bc, decay_to_end, x)
    chunk_decay = jnp.exp(cum[..., -1])

    def step(hs, inp):
        st, dec = inp
        return dec[..., None, None] * hs + st, hs

    h0 = jnp.zeros((bsz, g, rep, p, n), jnp.float32)
    _, prev = lax.scan(step, h0, (chunk_states.transpose(1, 0, 2, 3, 4, 5), chunk_decay.transpose(1, 0, 2, 3)))
    prev = prev.transpose(1, 0, 2, 3, 4, 5)
    y_off = jnp.einsum('bclgn,bcgrpn,bcgrl->bclgrp', cc, prev, jnp.exp(cum))
    return (y_diag + y_off).reshape(bsz, s, h, p)


def rwkv7_group(ua, mu, w0, w_up, a0, a_up, g_up, k_k, k_a, r_k, ln_gain, ln_bias):
    bsz, s, _ = ua.shape
    f32 = jnp.float32
    ua = ua + mu * (token_shift(ua) - ua)
    r, k, v, wd, ad, gd = split_cols(ua, [MIX_HALF, MIX_HALF, MIX_HALF, RWKV_DECAY_RANK, RWKV_ICLR_RANK, RWKV_GATE_RANK])
    d = (w0 + jnp.tanh(wd) @ w_up).astype(f32)
    w = jnp.exp(-RWKV_DECAY_SCALE * jax.nn.sigmoid(d))
    a = jax.nn.sigmoid((a0 + ad @ a_up).astype(f32))
    gate = jax.nn.sigmoid(gd) @ g_up
    hd = lambda t: t.astype(f32).reshape(bsz, s, RWKV_HEADS, HEAD_DIM)
    r, k, v, w, a = hd(r), hd(k), hd(v), hd(w), hd(a)
    kk = k * k_k.astype(f32).reshape(RWKV_HEADS, HEAD_DIM)
    kk = kk * lax.rsqrt(jnp.sum(kk * kk, axis=-1, keepdims=True) + 1e-12)
    kt = k * (1.0 + (a - 1.0) * k_a.astype(f32).reshape(RWKV_HEADS, HEAD_DIM))
    y = rwkv7_recurrence(r, w, kk, a, kt, v)
    mean = jnp.mean(y, axis=-1, keepdims=True)
    var = jnp.mean((y - mean) ** 2, axis=-1, keepdims=True)
    y = ((y - mean) * lax.rsqrt(var + RWKV_GN_EPS) * ln_gain.astype(f32).reshape(RWKV_HEADS, HEAD_DIM)
         + ln_bias.astype(f32).reshape(RWKV_HEADS, HEAD_DIM))
    y = y + jnp.sum(r * kt * r_k.astype(f32), axis=-1, keepdims=True) * v
    return (y.reshape(bsz, s, MIX_HALF) * gate).astype(ua.dtype)


def gla_group(ub, alpha_up, alpha_bias, norm_gain):
    bsz, s, _ = ub.shape
    f32 = jnp.float32
    q, k, v, ad, g = split_cols(ub, [GLA_KEY, GLA_KEY, MIX_HALF, GLA_GATE_RANK, MIX_HALF])
    log_a = jax.nn.log_sigmoid((ad @ alpha_up + alpha_bias).astype(f32)) / GLA_GATE_NORM
    hd = lambda t: t.astype(f32).reshape(bsz, s, GLA_HEADS, -1)
    o = chunked_gated_linear_attention(hd(q) * GLA_DK ** -0.5, hd(k), hd(v), hd(log_a))
    o = rms_norm(o, norm_gain)
    return (o.reshape(bsz, s, MIX_HALF) * jax.nn.silu(g)).astype(ub.dtype)


def hgrn_lower_bounds(c_lb):
    p = jax.nn.softmax(c_lb.astype(jnp.float32), axis=0)
    cum = jnp.cumsum(p, axis=0)
    return cum - cum[0:1]


def hgrn2_group(uc, lb, norm_gain):
    bsz, s, _ = uc.shape
    f32 = jnp.float32
    q, fr, i, g = split_cols(uc, [HGRN_KEY, HGRN_KEY, MIX_HALF, MIX_HALF])
    f = lb + (1.0 - lb) * jax.nn.sigmoid(fr.astype(f32))
    hd = lambda t: t.astype(f32).reshape(bsz, s, HGRN_HEADS, -1)
    o = chunked_gated_linear_attention(hd(q), hd(1.0 - f), hd(i), hd(jnp.log(f)))
    o = rms_norm(o, norm_gain)
    return (o.reshape(bsz, s, MIX_HALF) * jax.nn.silu(g)).astype(uc.dtype)


def ssd_group(ud, conv_w, conv_b, dt_bias, a_log, skip, norm_gain):
    bsz, s, _ = ud.shape
    f32 = jnp.float32
    z, xbc, dt_raw = split_cols(ud, [MIX_HALF, SSD_XBC, SSD_HEADS])
    xbc = jax.nn.silu(causal_depthwise_conv(xbc, conv_w, conv_b))
    xs, bm, cm = split_cols(xbc, [MIX_HALF, SSD_GROUPS * SSD_STATE, SSD_GROUPS * SSD_STATE])
    dt = jax.nn.softplus((dt_raw + dt_bias).astype(f32))
    a_neg = -jnp.exp(a_log.astype(f32))
    xh = xs.astype(f32).reshape(bsz, s, SSD_HEADS, HEAD_DIM)
    y = ssd_chunked(xh, dt, a_neg,
                    bm.astype(f32).reshape(bsz, s, SSD_GROUPS, SSD_STATE),
                    cm.astype(f32).reshape(bsz, s, SSD_GROUPS, SSD_STATE))
    y = y + skip.astype(f32)[:, None] * xh
    y = y.reshape(bsz, s, MIX_HALF) * jax.nn.silu(z.astype(f32))
    y = rms_norm(y.reshape(bsz, s, SSD_GROUPS, -1), norm_gain.astype(f32).reshape(SSD_GROUPS, -1))
    return y.reshape(bsz, s, MIX_HALF).astype(ud.dtype)


def grouped_moe(h, w_router, router_bias, w_gate, w_up, w_down):
    bsz, s, d = h.shape
    t = bsz * s
    hf = h.reshape(t, d)
    scores = jax.nn.sigmoid((hf @ w_router).astype(jnp.float32))
    sel = (scores + router_bias.astype(jnp.float32)).reshape(t, N_EXPERT_GROUPS, EXPERTS_PER_GROUP)
    group_score = jnp.sum(lax.top_k(sel, TOP_K)[0], axis=-1)
    group = jnp.argmax(group_score, axis=-1)
    in_group = jnp.take_along_axis(sel, group[:, None, None], axis=1)[:, 0]
    _, local = lax.top_k(in_group, TOP_K)
    expert = group[:, None] * EXPERTS_PER_GROUP + local
    gate = jnp.take_along_axis(scores, expert, axis=1)
    gate = gate / jnp.sum(gate, axis=-1, keepdims=True)
    n_assign = t * TOP_K
    flat_e = expert.reshape(n_assign)
    order = jnp.argsort(flat_e)
    sorted_e = flat_e[order]
    tok = (order // TOP_K).astype(jnp.int32)
    counts = jnp.bincount(flat_e, length=N_EXPERTS)
    padded = (counts + MOE_BLOCK - 1) // MOE_BLOCK * MOE_BLOCK
    pad_end = jnp.cumsum(padded)
    pad_start = pad_end - padded
    start = jnp.cumsum(counts) - counts
    dest = pad_start[sorted_e] + (jnp.arange(n_assign) - start[sorted_e])
    n_blocks = -(-n_assign // MOE_BLOCK) + N_EXPERTS
    n_pad = n_blocks * MOE_BLOCK
    buf_tok = jnp.zeros((n_pad,), jnp.int32).at[dest].set(tok)
    buf_w = jnp.zeros((n_pad,), h.dtype).at[dest].set(gate.reshape(n_assign)[order].astype(h.dtype))
    block_expert = jnp.minimum(jnp.searchsorted(pad_end, jnp.arange(n_blocks) * MOE_BLOCK, side='right'), N_EXPERTS - 1)
    xb = hf[buf_tok].reshape(n_blocks, MOE_BLOCK, d)

    def expert_block(args):
        xblk, e = args
        return (jax.nn.silu(xblk @ w_gate[e]) * (xblk @ w_up[e])) @ w_down[e]

    yb = lax.map(expert_block, (xb, block_expert)).reshape(n_pad, d)
    out = jnp.zeros((t, d), h.dtype).at[buf_tok].add(yb * buf_w[:, None])
    return out.reshape(bsz, s, d)


def setup_inputs(seed: int = 0) -> dict:
    key = jax.random.key(seed)
    keys = jax.random.split(key, 36)
    ki = iter(range(36))

    def nrm(shape, scale):
        return scale * jax.random.normal(keys[next(ki)], shape, jnp.float32)

    def unif(shape, lo, hi):
        return jax.random.uniform(keys[next(ki)], shape, jnp.float32, lo, hi)

    dt0 = jnp.exp(unif((N_ODD, SSD_HEADS), float(np.log(1e-3)), float(np.log(1e-1))))
    return {
        'x': nrm((BATCH, SEQ, D_MODEL), 1.0),
        'c': nrm((BATCH, D_MODEL), 1.0),
        'norm_gain': 1.0 + nrm((DEPTH, 2, D_MODEL), 0.05),
        'w_ada': nrm((DEPTH, D_MODEL, 6 * D_MODEL), 0.5 * D_MODEL ** -0.5),
        'b_ada': nrm((DEPTH, 6 * D_MODEL), 0.05),
        'w_in_even': nrm((N_EVEN, D_MODEL, EVEN_IN), D_MODEL ** -0.5),
        'w_in_odd': nrm((N_ODD, D_MODEL, ODD_IN), D_MODEL ** -0.5),
        'w_out': nrm((DEPTH, D_MODEL, D_MODEL), D_MODEL ** -0.5),
        'a_mu': unif((N_EVEN, RWKV_IN), 0.0, 1.0),
        'a_w0': unif((N_EVEN, MIX_HALF), -2.0, 2.0),
        'a_w_up': nrm((N_EVEN, RWKV_DECAY_RANK, MIX_HALF), RWKV_DECAY_RANK ** -0.5),
        'a_a0': nrm((N_EVEN, MIX_HALF), 0.5),
        'a_a_up': nrm((N_EVEN, RWKV_ICLR_RANK, MIX_HALF), RWKV_ICLR_RANK ** -0.5),
        'a_g_up': nrm((N_EVEN, RWKV_GATE_RANK, MIX_HALF), RWKV_GATE_RANK ** -0.5),
        'a_k_k': 0.85 + nrm((N_EVEN, MIX_HALF), 0.05),
        'a_k_a': 1.0 + nrm((N_EVEN, MIX_HALF), 0.05),
        'a_r_k': nrm((N_EVEN, RWKV_HEADS, HEAD_DIM), 0.1),
        'a_ln_gain': 1.0 + nrm((N_EVEN, MIX_HALF), 0.05),
        'a_ln_bias': nrm((N_EVEN, MIX_HALF), 0.02),
        'b_alpha_up': nrm((N_EVEN, GLA_GATE_RANK, GLA_KEY), GLA_GATE_RANK ** -0.5),
        'b_alpha_bias': nrm((N_EVEN, GLA_KEY), 0.5),
        'b_norm_gain': 1.0 + nrm((N_EVEN, GLA_DV), 0.05),
        'c_lb': nrm((DEPTH, HGRN_KEY), 0.5),
        'c_norm_gain': 1.0 + nrm((N_ODD, HGRN_DV), 0.05),
        'd_conv_w': nrm((N_ODD, SSD_CONV, SSD_XBC), SSD_CONV ** -0.5),
        'd_conv_b': nrm((N_ODD, SSD_XBC), 0.02),
        'd_dt_bias': dt0 + jnp.log(-jnp.expm1(-dt0)),
        'd_a_log': jnp.log(unif((N_ODD, SSD_HEADS), 1.0, 16.0)),
        'd_skip': 1.0 + nrm((N_ODD, SSD_HEADS), 0.05),
        'd_norm_gain': 1.0 + nrm((N_ODD, MIX_HALF), 0.05),
        'w_router': nrm((D_MODEL, N_EXPERTS), D_MODEL ** -0.5),
        'router_bias': nrm((N_EXPERTS,), 0.01),
        'w_gate': nrm((DEPTH, N_EXPERTS, D_MODEL, D_EXPERT), D_MODEL ** -0.5),
        'w_up': nrm((DEPTH, N_EXPERTS, D_MODEL, D_EXPERT), D_MODEL ** -0.5),
        'w_down': nrm((DEPTH, N_EXPERTS, D_EXPERT, D_MODEL), D_EXPERT ** -0.5),
        'final_gain': 1.0 + nrm((D_MODEL,), 0.05),
    }


def reference(x, c, norm_gain, w_ada, b_ada, w_in_even, w_in_odd, w_out,
              a_mu, a_w0, a_w_up, a_a0, a_a_up, a_g_up, a_k_k, a_k_a, a_r_k, a_ln_gain, a_ln_bias,
              b_alpha_up, b_alpha_bias, b_norm_gain, c_lb, c_norm_gain,
              d_conv_w, d_conv_b, d_dt_bias, d_a_log, d_skip, d_norm_gain,
              w_router, router_bias, w_gate, w_up, w_down, final_gain):
    lower_bounds = hgrn_lower_bounds(c_lb)
    cond = jax.nn.silu(c)
    for l in range(DEPTH):
        j = l // 2
        mod = cond @ w_ada[l] + b_ada[l]
        sh_m, sc_m, g_m, sh_f, sc_f, g_f = [m[:, None, :] for m in jnp.split(mod, 6, axis=-1)]
        h = rms_norm(x, norm_gain[l, 0]) * (1.0 + sc_m) + sh_m
        if l % 2 == 0:
            ua, ub = split_cols(h @ w_in_even[j], [RWKV_IN, GLA_IN])
            y_first = rwkv7_group(ua, a_mu[j], a_w0[j], a_w_up[j], a_a0[j], a_a_up[j], a_g_up[j],
                                  a_k_k[j], a_k_a[j], a_r_k[j], a_ln_gain[j], a_ln_bias[j])
            y_second = gla_group(ub, b_alpha_up[j], b_alpha_bias[j], b_norm_gain[j])
        else:
            uc, ud = split_cols(h @ w_in_odd[j], [HGRN_IN, SSD_IN])
            y_first = hgrn2_group(uc, lower_bounds[l], c_norm_gain[j])
            y_second = ssd_group(ud, d_conv_w[j], d_conv_b[j], d_dt_bias[j], d_a_log[j], d_skip[j], d_norm_gain[j])
        x = x + g_m * (jnp.concatenate([y_first, y_second], axis=-1) @ w_out[l])
        h = rms_norm(x, norm_gain[l, 1]) * (1.0 + sc_f) + sh_f
        x = x + g_f * grouped_moe(h, w_router, router_bias, w_gate[l], w_up[l], w_down[l])
    return rms_norm(x, final_gain)
```

```python
import functools

import numpy as np
import jax
import jax.numpy as jnp
from jax import lax
from jax.experimental import pallas as pl
from jax.experimental.pallas import tpu as pltpu

F32 = jnp.float32
BF16 = jnp.bfloat16
HIGHEST = lax.Precision.HIGHEST

D_MODEL = 1024
MIX_HALF = 512
HEAD_DIM = 64
CHUNK = 64
NORM_EPS = 1e-6
RWKV_HEADS = 8
RWKV_IN = 1792
RWKV_GN_EPS = 64e-5
RWKV_DECAY_SCALE = float(np.exp(-0.5))
GLA_HEADS = 4
GLA_GATE_RANK = 16
GLA_GATE_NORM = 16.0
SSD_HEADS = 8
SSD_GROUPS = 2
SSD_STATE = 64
SSD_CONV = 4
SSD_XBC = 768
N_EXPERTS = 16
N_EXPERT_GROUPS = 4
EXPERTS_PER_GROUP = 4
D_EXPERT = 512
LANE = 128
EVEN_COLS = 3456
ODD_COLS = 2944
MOE_ROWS = 256
VMEM_LIMIT = 48 * 1024 * 1024


def _dot(a, b):
    return jnp.dot(a.astype(BF16), b.astype(BF16), preferred_element_type=F32)


def _dot_nt(a, b):
    return lax.dot_general(a.astype(BF16), b.astype(BF16), (((1,), (1,)), ((), ())),
                           preferred_element_type=F32)


def _dot_tn(a, b):
    return lax.dot_general(a.astype(BF16), b.astype(BF16), (((0,), (0,)), ((), ())),
                           preferred_element_type=F32)


def _dot_exact(a, b):
    return jnp.dot(a, b, precision=HIGHEST, preferred_element_type=F32)


def _sigmoid(x):
    return 1.0 / (1.0 + jnp.exp(-x))


def _silu(x):
    return x * _sigmoid(x)


def _softplus(x):
    return jnp.maximum(x, 0.0) + jnp.log(1.0 + jnp.exp(-jnp.abs(x)))


def _log_sigmoid(x):
    return jnp.minimum(x, 0.0) - jnp.log(1.0 + jnp.exp(-jnp.abs(x)))


def _chunk_constants():
    t = np.arange(CHUNK)
    tril = (t[None, :] <= t[:, None]).astype(np.float32)
    rows = [tril]
    masks = []
    for shift in range(5, -1, -1):
        n = 2 << shift
        mid = (t & ~(n - 1)) + (n >> 1) - 1
        rows.append(tril[mid])
        masks.append(((t[:, None] > t[None, :]) & (((t[:, None] ^ t[None, :]) >> shift) == 1)))
    masks.append(t[:, None] == t[None, :])
    cum_sel = np.concatenate(rows, axis=0)
    pair_masks = np.stack(masks).astype(np.float32)
    return jnp.asarray(cum_sel), jnp.asarray(pair_masks)


def _head_indicator(width, seg):
    i = np.arange(width)
    return jnp.asarray((i[:, None] // seg == i[None, :] // seg).astype(np.float32), dtype=BF16)


def _ada_kernel(c_ref, w_ref, b_ref, o_ref):
    cond = _silu(c_ref[...])
    o_ref[0] = _dot_exact(cond, w_ref[0]) + b_ref[0]


def _ada(c, w_ada, b_ada):
    depth, d, n = w_ada.shape
    bsz = c.shape[0]
    tn = 1536
    return pl.pallas_call(
        _ada_kernel,
        grid=(depth, n // tn),
        in_specs=[pl.BlockSpec((bsz, d), lambda l, j: (0, 0)),
                  pl.BlockSpec((1, d, tn), lambda l, j: (l, 0, j)),
                  pl.BlockSpec((1, 1, tn), lambda l, j: (l, 0, j))],
        out_specs=pl.BlockSpec((1, bsz, tn), lambda l, j: (l, 0, j)),
        out_shape=jax.ShapeDtypeStruct((depth, bsz, n), F32),
        compiler_params=pltpu.CompilerParams(dimension_semantics=("parallel", "parallel"),
                                             vmem_limit_bytes=VMEM_LIMIT),
        name="ada",
    )(c, w_ada, b_ada.reshape(depth, 1, n))


def _modulated_norm(x, gain, scale, shift):
    ms = jnp.mean(x * x, axis=-1, keepdims=True)
    return (x * lax.rsqrt(ms + NORM_EPS)) * gain * (1.0 + scale) + shift


def _in_proj_kernel(x_ref, gain_ref, sc_ref, sh_ref, w_ref, o_ref, *, col_chunk):
    h = _modulated_norm(x_ref[...], gain_ref[...], sc_ref[0], sh_ref[0]).astype(BF16)
    n = o_ref.shape[1]
    for j in range(0, n, col_chunk):
        o_ref[:, j:j + col_chunk] = jnp.dot(h, w_ref[:, j:j + col_chunk],
                                            preferred_element_type=F32).astype(o_ref.dtype)


def _in_proj(x2, gain, scale, shift, w, seq, tm):
    t, d = x2.shape
    n = w.shape[1]
    col_chunk = n // 3 if (n // 3) % LANE == 0 and n % 3 == 0 else n
    per_b = seq // tm
    return pl.pallas_call(
        functools.partial(_in_proj_kernel, col_chunk=col_chunk),
        grid=(t // tm,),
        in_specs=[pl.BlockSpec((tm, d), lambda i: (i, 0)),
                  pl.BlockSpec((1, d), lambda i: (0, 0)),
                  pl.BlockSpec((1, 1, d), lambda i: (i // per_b, 0, 0)),
                  pl.BlockSpec((1, 1, d), lambda i: (i // per_b, 0, 0)),
                  pl.BlockSpec((d, n), lambda i: (0, 0))],
        out_specs=pl.BlockSpec((tm, n), lambda i: (i, 0)),
        out_shape=jax.ShapeDtypeStruct((t, n), BF16),
        compiler_params=pltpu.CompilerParams(dimension_semantics=("parallel",),
                                             vmem_limit_bytes=VMEM_LIMIT),
        name="in_proj",
    )(x2, gain, scale, shift, w)


def _gla_chunk(q, k, v, g, st_ref, cum_sel, pair_masks, heads, dk, dv):
    ball = _dot_exact(cum_sel, g)
    b = ball[0:CHUNK]
    b_last = b[CHUNK - 1:CHUNK]
    q_in = q * jnp.exp(b)
    k_st = k * jnp.exp(b_last - b)
    decay = jnp.exp(b_last)
    scores = [None] * heads
    for lvl in range(7):
        if lvl < 6:
            e = jnp.exp(-jnp.abs(b - ball[CHUNK * (lvl + 1):CHUNK * (lvl + 2)]))
            qe, ke = q * e, k * e
        else:
            qe, ke = q, k
        keep = pair_masks[lvl] > 0.5
        for h in range(heads):
            p = _dot_nt(qe[:, h * dk:(h + 1) * dk], ke[:, h * dk:(h + 1) * dk])
            p = jnp.where(keep, p, 0.0)
            scores[h] = p if scores[h] is None else scores[h] + p
    outs = []
    for h in range(heads):
        ks = slice(h * dk, (h + 1) * dk)
        vs = slice(h * dv, (h + 1) * dv)
        st = st_ref[h]
        outs.append(_dot_nt(q_in[:, ks], st) + _dot(scores[h], v[:, vs]))
        st_ref[h] = st * decay[:, ks] + _dot_tn(v[:, vs], k_st[:, ks])
    return jnp.concatenate(outs, axis=-1)


def _head_rms(o, gain, heads, dv):
    outs = []
    for h in range(heads):
        oh = o[:, h * dv:(h + 1) * dv]
        ms = jnp.mean(oh * oh, axis=-1, keepdims=True)
        outs.append(oh * lax.rsqrt(ms + NORM_EPS) * gain)
    return jnp.concatenate(outs, axis=-1)


def _rwkv_chunk(r, kk, a, kt, v, logw, s_ref, cum_sel, pair_masks):
    tril = cum_sel[0:CHUNK]
    b = _dot_exact(tril, logw)
    b_last = b[CHUNK - 1:CHUNK]
    e_neg = jnp.exp(-b)
    e_end = jnp.exp(b_last - b)
    decay = jnp.exp(b_last)
    beta = a * kk
    k_bar = kk * jnp.exp(b - logw)
    r_bar = r * jnp.exp(b)
    beta_t, k_t = beta * e_neg, kt * e_neg
    beta_hat, k_hat = beta * e_end, kt * e_end
    row = lax.broadcasted_iota(jnp.int32, (CHUNK, CHUNK), 0)
    col = lax.broadcasted_iota(jnp.int32, (CHUNK, CHUNK), 1)
    strict = col < row
    incl = col <= row
    same_blk = (row >> 4) == (col >> 4)
    eye = (row == col).astype(F32)
    outs = []
    for h in range(RWKV_HEADS):
        sl = slice(h * HEAD_DIM, (h + 1) * HEAD_DIM)
        kr = jnp.concatenate([k_bar[:, sl], r_bar[:, sl]], axis=0)
        bk = jnp.concatenate([beta_t[:, sl], k_t[:, sl]], axis=0)
        m1 = _dot_nt(kr, bk)
        a_m = jnp.where(strict, m1[0:CHUNK, 0:CHUNK], 0.0)
        b_m = jnp.where(strict, m1[0:CHUNK, CHUNK:], 0.0)
        cb_m = jnp.where(incl, m1[CHUNK:, 0:CHUNK], 0.0)
        ck_m = jnp.where(incl, m1[CHUNK:, CHUNK:], 0.0)
        s0 = s_ref[h]
        m2 = _dot_nt(kr, s0)
        vh = v[:, sl]
        rhs = m2[0:CHUNK] + _dot(b_m, vh)
        x1 = jnp.where(same_blk, -a_m, 0.0)
        a_off = jnp.where(same_blk, 0.0, a_m)
        x2 = _dot(x1, x1)
        x4 = _dot(x2, x2)
        x8 = _dot(x4, x4)
        p = eye + x1
        p = p + _dot(p, x2)
        p = p + _dot(p, x4)
        t_d = p + _dot(p, x8)
        n_m = _dot(t_d, a_off)
        z = _dot(t_d, rhs)
        z = z + _dot(_dot(n_m, n_m), z)
        u = z - _dot(n_m, z)
        outs.append(m2[CHUNK:] + _dot(ck_m, vh) - _dot(cb_m, u))
        s_ref[h] = (s0 * decay[:, sl] + _dot_tn(vh, k_hat[:, sl]) - _dot_tn(u, beta_hat[:, sl]))
    return jnp.concatenate(outs, axis=-1)


def _mix_even_kernel(u_ref, mu_ref, w0_ref, wup_ref, a0_ref, aup_ref, gup_ref, kk_ref, ka_ref,
                     rk_ref, lng_ref, lnb_ref, alup_ref, albias_ref, bng_ref, ind_ref, csel_ref,
                     pmask_ref, y_ref, s_ref, g_ref, prev_ref):
    @pl.when(pl.program_id(1) == 0)
    def _():
        s_ref[...] = jnp.zeros_like(s_ref)
        g_ref[...] = jnp.zeros_like(g_ref)
        prev_ref[...] = jnp.zeros_like(prev_ref)

    u = u_ref[0].astype(F32)
    cum_sel = csel_ref[...]
    pair_masks = pmask_ref[...]
    ind = ind_ref[...]
    ua = u[:, 0:RWKV_IN]
    rows = lax.broadcasted_iota(jnp.int32, ua.shape, 0)
    shifted = jnp.where(rows == 0, prev_ref[...], pltpu.roll(ua, 1, axis=0))
    prev_ref[...] = ua[CHUNK - 1:CHUNK]
    xa = ua + mu_ref[...] * (shifted - ua)
    r, k, v = xa[:, 0:512], xa[:, 512:1024], xa[:, 1024:1536]
    wd, ad, gd = xa[:, 1536:1600], xa[:, 1600:1664], xa[:, 1664:1792]
    logw = -RWKV_DECAY_SCALE * _sigmoid(w0_ref[...] + _dot(jnp.tanh(wd), wup_ref[...]))
    a = _sigmoid(a0_ref[...] + _dot(ad, aup_ref[...]))
    gate = _dot(_sigmoid(gd), gup_ref[...])
    kk = k * kk_ref[...]
    kk = kk * lax.rsqrt(_dot(kk * kk, ind) + 1e-12)
    kt = k * (1.0 + (a - 1.0) * ka_ref[...])
    y = _rwkv_chunk(r, kk, a, kt, v, logw, s_ref, cum_sel, pair_masks)
    mean = _dot(y, ind) * (1.0 / HEAD_DIM)
    yc = y - mean
    var = _dot(yc * yc, ind) * (1.0 / HEAD_DIM)
    y = yc * lax.rsqrt(var + RWKV_GN_EPS) * lng_ref[...] + lnb_ref[...]
    y = y + _dot(r * kt * rk_ref[...], ind) * v
    y_ref[0, :, 0:MIX_HALF] = (y * gate).astype(y_ref.dtype)
    ub = u[:, RWKV_IN:]
    q, kg, vg = ub[:, 0:256] * (HEAD_DIM ** -0.5), ub[:, 256:512], ub[:, 512:1024]
    alpha, gg = ub[:, 1024:1152], ub[:, 1152:1664]
    log_a = _log_sigmoid(_dot(alpha, alup_ref[...]) + albias_ref[...]) * (1.0 / GLA_GATE_NORM)
    o = _gla_chunk(q, kg, vg, log_a, g_ref, cum_sel, pair_masks, GLA_HEADS, 64, 128)
    o = _head_rms(o, bng_ref[...], GLA_HEADS, 128)
    y_ref[0, :, MIX_HALF:] = (o * _silu(gg)).astype(y_ref.dtype)


def _row(p):
    return p.reshape(1, -1).astype(F32)


def _mix_even(u3, p, consts):
    bsz, seq, n = u3.shape
    cum_sel, pair_masks, ind512 = consts
    small = [p["mu"], p["w0"], p["w_up"], p["a0"], p["a_up"], p["g_up"], p["k_k"], p["k_a"], p["r_k"],
             p["ln_gain"], p["ln_bias"], p["alpha_up"], p["alpha_bias"], p["b_norm_gain"],
             ind512, cum_sel, pair_masks]

    def full(arr):
        nd = arr.ndim
        return pl.BlockSpec(arr.shape, lambda b, s, _nd=nd: (0,) * _nd)

    return pl.pallas_call(
        _mix_even_kernel,
        grid=(bsz, seq // CHUNK),
        in_specs=[pl.BlockSpec((1, CHUNK, n), lambda b, s: (b, s, 0))] + [full(a) for a in small],
        out_specs=pl.BlockSpec((1, CHUNK, D_MODEL), lambda b, s: (b, s, 0)),
        out_shape=jax.ShapeDtypeStruct((bsz, seq, D_MODEL), BF16),
        scratch_shapes=[pltpu.VMEM((RWKV_HEADS, HEAD_DIM, HEAD_DIM), F32),
                        pltpu.VMEM((GLA_HEADS, 128, 64), F32),
                        pltpu.VMEM((1, RWKV_IN), F32)],
        compiler_params=pltpu.CompilerParams(dimension_semantics=("parallel", "arbitrary"),
                                             vmem_limit_bytes=VMEM_LIMIT),
        name="mix_even",
    )(u3, *small)


def _mix_odd_kernel(u_ref, lb_ref, cng_ref, convw_ref, convb_ref, dtb_ref, alog_ref, skip_ref,
                    dng_ref, expand_ref, csel_ref, pmask_ref, y_ref, h_ref, d_ref, tail_ref):
    @pl.when(pl.program_id(1) == 0)
    def _():
        h_ref[...] = jnp.zeros_like(h_ref)
        d_ref[...] = jnp.zeros_like(d_ref)
        tail_ref[...] = jnp.zeros_like(tail_ref)

    u = u_ref[0].astype(F32)
    cum_sel = csel_ref[...]
    pair_masks = pmask_ref[...]
    q, fr, iv, g = u[:, 0:256], u[:, 256:512], u[:, 512:1024], u[:, 1024:1536]
    lb = lb_ref[...]
    f = lb + (1.0 - lb) * _sigmoid(fr)
    o = _gla_chunk(q, 1.0 - f, iv, jnp.log(f), h_ref, cum_sel, pair_masks, 4, 64, 128)
    o = _head_rms(o, cng_ref[...], 4, 128)
    y_ref[0, :, 0:MIX_HALF] = (o * _silu(g)).astype(y_ref.dtype)
    z, xbc, dt_raw = u[:, 1536:2048], u[:, 2048:2816], u[:, 2816:2944]
    tail = tail_ref[...]
    tail_ref[...] = xbc[CHUNK - 8:CHUNK]
    rows8 = lax.broadcasted_iota(jnp.int32, (8, SSD_XBC), 0)
    conv = xbc * convw_ref[SSD_CONV - 1:SSD_CONV] + convb_ref[...]
    for back in range(1, SSD_CONV):
        rolled = pltpu.roll(xbc, back, axis=0)
        head8 = jnp.where(rows8 < back, pltpu.roll(tail, back, axis=0), rolled[0:8])
        shifted = jnp.concatenate([head8, rolled[8:]], axis=0)
        conv = conv + shifted * convw_ref[SSD_CONV - 1 - back:SSD_CONV - back]
    xbc = _silu(conv)
    xs, bmat, cmat = xbc[:, 0:512], xbc[:, 512:640], xbc[:, 640:768]
    tril = cum_sel[0:CHUNK]
    dt = _dot_exact(_softplus(dt_raw + dtb_ref[...]), expand_ref[...])
    da = dt * (-jnp.exp(alog_ref[...]))
    cum = _dot_exact(tril, da)
    cum_last = cum[CHUNK - 1:CHUNK]
    e_cum = jnp.exp(cum)
    xdt = xs * dt
    x_end = xdt * jnp.exp(cum_last - cum)
    decay = jnp.exp(cum_last)
    row = lax.broadcasted_iota(jnp.int32, (CHUNK, CHUNK), 0)
    col = lax.broadcasted_iota(jnp.int32, (CHUNK, CHUNK), 1)
    causal = col <= row
    outs = []
    for grp in range(SSD_GROUPS):
        gs = slice(grp * 256, (grp + 1) * 256)
        ns = slice(grp * SSD_STATE, (grp + 1) * SSD_STATE)
        cb = _dot_nt(cmat[:, ns], bmat[:, ns])
        st = d_ref[grp]
        y_off = _dot(cmat[:, ns], st) * e_cum[:, gs]
        d_ref[grp] = st * decay[:, gs] + _dot_tn(bmat[:, ns], x_end[:, gs])
        for rep in range(SSD_HEADS // SSD_GROUPS):
            hs = slice(grp * 256 + rep * HEAD_DIM, grp * 256 + (rep + 1) * HEAD_DIM)
            cum_h = cum[:, hs]
            seg = jnp.where(causal, jnp.exp(cum_h - cum_h.T), 0.0)
            outs.append(_dot(cb * seg, xdt[:, hs]))
        outs[-4:] = [jnp.concatenate(outs[-4:], axis=-1) + y_off]
    y = jnp.concatenate(outs, axis=-1) + skip_ref[...] * xs
    y = y * _silu(z)
    parts = []
    for grp in range(SSD_GROUPS):
        yg = y[:, grp * 256:(grp + 1) * 256]
        ms = jnp.mean(yg * yg, axis=-1, keepdims=True)
        parts.append(yg * lax.rsqrt(ms + NORM_EPS))
    y_ref[0, :, MIX_HALF:] = (jnp.concatenate(parts, axis=-1) * dng_ref[...]).astype(y_ref.dtype)


def _mix_odd(u3, p, consts):
    bsz, seq, n = u3.shape
    cum_sel, pair_masks, expand = consts
    small = [p["lb"], p["c_norm_gain"], p["conv_w"], p["conv_b"], p["dt_bias"], p["a_log"], p["skip"],
             p["d_norm_gain"], expand, cum_sel, pair_masks]

    def full(arr):
        nd = arr.ndim
        return pl.BlockSpec(arr.shape, lambda b, s, _nd=nd: (0,) * _nd)

    return pl.pallas_call(
        _mix_odd_kernel,
        grid=(bsz, seq // CHUNK),
        in_specs=[pl.BlockSpec((1, CHUNK, n), lambda b, s: (b, s, 0))] + [full(a) for a in small],
        out_specs=pl.BlockSpec((1, CHUNK, D_MODEL), lambda b, s: (b, s, 0)),
        out_shape=jax.ShapeDtypeStruct((bsz, seq, D_MODEL), BF16),
        scratch_shapes=[pltpu.VMEM((4, 128, 64), F32),
                        pltpu.VMEM((SSD_GROUPS, SSD_STATE, 256), F32),
                        pltpu.VMEM((8, SSD_XBC), F32)],
        compiler_params=pltpu.CompilerParams(dimension_semantics=("parallel", "arbitrary"),
                                             vmem_limit_bytes=VMEM_LIMIT),
        name="mix_odd",
    )(u3, *small)


def _out_proj_kernel(y_ref, w_ref, x_ref, g_ref, o_ref):
    o_ref[...] = x_ref[...] + g_ref[0] * jnp.dot(y_ref[...], w_ref[...], preferred_element_type=F32)


def _out_proj(y2, w, x2, gate, seq, tm):
    t, d = x2.shape
    per_b = seq // tm
    return pl.pallas_call(
        _out_proj_kernel,
        grid=(t // tm,),
        in_specs=[pl.BlockSpec((tm, d), lambda i: (i, 0)),
                  pl.BlockSpec((d, d), lambda i: (0, 0)),
                  pl.BlockSpec((tm, d), lambda i: (i, 0)),
                  pl.BlockSpec((1, 1, d), lambda i: (i // per_b, 0, 0))],
        out_specs=pl.BlockSpec((tm, d), lambda i: (i, 0)),
        out_shape=jax.ShapeDtypeStruct((t, d), F32),
        compiler_params=pltpu.CompilerParams(dimension_semantics=("parallel",),
                                             vmem_limit_bytes=VMEM_LIMIT),
        name="out_proj",
    )(y2, w, x2, gate)


def _route_kernel(x_ref, gain_ref, sc_ref, sh_ref, wr_ref, bias_ref, h_ref, e_ref, gt_ref):
    h = _modulated_norm(x_ref[...], gain_ref[...], sc_ref[0], sh_ref[0])
    h_ref[...] = h.astype(h_ref.dtype)
    logits = lax.dot_general(wr_ref[...], h, (((1,), (1,)), ((), ())), precision=HIGHEST,
                             preferred_element_type=F32)
    score = _sigmoid(logits)
    sel = score + bias_ref[...]
    gscore = []
    for grp in range(N_EXPERT_GROUPS):
        a, b, c, d = [sel[grp * 4 + j:grp * 4 + j + 1] for j in range(4)]
        hi1, lo1, hi2, lo2 = jnp.maximum(a, b), jnp.minimum(a, b), jnp.maximum(c, d), jnp.minimum(c, d)
        gscore.append(jnp.maximum(hi1, hi2) + jnp.maximum(jnp.minimum(hi1, hi2), jnp.maximum(lo1, lo2)))
    best, gidx = gscore[0], jnp.zeros_like(gscore[0], dtype=jnp.int32)
    for grp in range(1, N_EXPERT_GROUPS):
        better = gscore[grp] > best
        gidx = jnp.where(better, grp, gidx)
        best = jnp.where(better, gscore[grp], best)
    vals, raw = [], []
    for j in range(EXPERTS_PER_GROUP):
        vj, rj = sel[j:j + 1], score[j:j + 1]
        for grp in range(1, N_EXPERT_GROUPS):
            vj = jnp.where(gidx == grp, sel[grp * 4 + j:grp * 4 + j + 1], vj)
            rj = jnp.where(gidx == grp, score[grp * 4 + j:grp * 4 + j + 1], rj)
        vals.append(vj)
        raw.append(rj)
    i1, m1, g1 = jnp.zeros_like(gidx), vals[0], raw[0]
    for j in range(1, EXPERTS_PER_GROUP):
        better = vals[j] > m1
        i1 = jnp.where(better, j, i1)
        m1 = jnp.where(better, vals[j], m1)
        g1 = jnp.where(better, raw[j], g1)
    i2, m2, g2 = jnp.zeros_like(gidx), jnp.full_like(m1, -jnp.inf), jnp.zeros_like(m1)
    for j in range(EXPERTS_PER_GROUP):
        better = jnp.logical_and(i1 != j, vals[j] > m2)
        i2 = jnp.where(better, j, i2)
        m2 = jnp.where(better, vals[j], m2)
        g2 = jnp.where(better, raw[j], g2)
    total = g1 + g2
    e_ref[0:1, :] = gidx * EXPERTS_PER_GROUP + i1
    e_ref[1:2, :] = gidx * EXPERTS_PER_GROUP + i2
    gt_ref[0:1, :] = g1 / total
    gt_ref[1:2, :] = g2 / total


def _route(x2, gain, scale, shift, wr_t, bias_col, seq, tm):
    t, d = x2.shape
    per_b = seq // tm
    return pl.pallas_call(
        _route_kernel,
        grid=(t // tm,),
        in_specs=[pl.BlockSpec((tm, d), lambda i: (i, 0)),
                  pl.BlockSpec((1, d), lambda i: (0, 0)),
                  pl.BlockSpec((1, 1, d), lambda i: (i // per_b, 0, 0)),
                  pl.BlockSpec((1, 1, d), lambda i: (i // per_b, 0, 0)),
                  pl.BlockSpec((N_EXPERTS, d), lambda i: (0, 0)),
                  pl.BlockSpec((N_EXPERTS, 1), lambda i: (0, 0))],
        out_specs=[pl.BlockSpec((tm, d), lambda i: (i, 0)),
                   pl.BlockSpec((2, tm), lambda i: (0, i)),
                   pl.BlockSpec((2, tm), lambda i: (0, i))],
        out_shape=[jax.ShapeDtypeStruct((t, d), BF16),
                   jax.ShapeDtypeStruct((2, t), jnp.int32),
                   jax.ShapeDtypeStruct((2, t), F32)],
        compiler_params=pltpu.CompilerParams(dimension_semantics=("parallel",),
                                             vmem_limit_bytes=VMEM_LIMIT),
        name="route",
    )(x2, gain, scale, shift, wr_t, bias_col)


def _expert_kernel(be_ref, bv_ref, x_ref, wg_ref, wu_ref, wd_ref, bw_ref, o_ref):
    i = pl.program_id(0)

    @pl.when(bv_ref[i] > 0)
    def _():
        x = x_ref[...]
        gate = jnp.dot(x, wg_ref[0].astype(BF16), preferred_element_type=F32)
        up = jnp.dot(x, wu_ref[0].astype(BF16), preferred_element_type=F32)
        act = (_silu(gate) * up).astype(BF16)
        y = jnp.dot(act, wd_ref[0].astype(BF16), preferred_element_type=F32)
        o_ref[...] = (y * bw_ref[...]).astype(o_ref.dtype)

    @pl.when(bv_ref[i] == 0)
    def _():
        o_ref[...] = jnp.zeros_like(o_ref)


def _experts(xb, w_gate, w_up, w_down, buf_w, block_expert, block_valid):
    n_pad, d = xb.shape
    n_blocks = n_pad // MOE_ROWS
    de = w_gate.shape[-1]
    grid_spec = pltpu.PrefetchScalarGridSpec(
        num_scalar_prefetch=2,
        grid=(n_blocks,),
        in_specs=[pl.BlockSpec((MOE_ROWS, d), lambda i, be, bv: (i, 0)),
                  pl.BlockSpec((1, d, de), lambda i, be, bv: (be[i], 0, 0)),
                  pl.BlockSpec((1, d, de), lambda i, be, bv: (be[i], 0, 0)),
                  pl.BlockSpec((1, de, d), lambda i, be, bv: (be[i], 0, 0)),
                  pl.BlockSpec((MOE_ROWS, 1), lambda i, be, bv: (i, 0))],
        out_specs=pl.BlockSpec((MOE_ROWS, d), lambda i, be, bv: (i, 0)),
    )
    return pl.pallas_call(
        _expert_kernel,
        grid_spec=grid_spec,
        out_shape=jax.ShapeDtypeStruct((n_pad, d), F32),
        compiler_params=pltpu.CompilerParams(dimension_semantics=("arbitrary",),
                                             vmem_limit_bytes=VMEM_LIMIT),
        name="experts",
    )(block_expert, block_valid, xb, w_gate, w_up, w_down, buf_w)


def _combine_kernel(x_ref, y_ref, g_ref, o_ref):
    d = x_ref.shape[1]
    o_ref[...] = x_ref[...] + g_ref[0] * (y_ref[:, 0:d] + y_ref[:, d:])


def _combine(x2, ypair, gate, seq, tm):
    t, d = x2.shape
    per_b = seq // tm
    return pl.pallas_call(
        _combine_kernel,
        grid=(t // tm,),
        in_specs=[pl.BlockSpec((tm, d), lambda i: (i, 0)),
                  pl.BlockSpec((tm, 2 * d), lambda i: (i, 0)),
                  pl.BlockSpec((1, 1, d), lambda i: (i // per_b, 0, 0))],
        out_specs=pl.BlockSpec((tm, d), lambda i: (i, 0)),
        out_shape=jax.ShapeDtypeStruct((t, d), F32),
        compiler_params=pltpu.CompilerParams(dimension_semantics=("parallel",),
                                             vmem_limit_bytes=VMEM_LIMIT),
        name="combine",
    )(x2, ypair, gate)


def _final_norm_kernel(x_ref, gain_ref, o_ref):
    x = x_ref[...]
    ms = jnp.mean(x * x, axis=-1, keepdims=True)
    o_ref[...] = x * lax.rsqrt(ms + NORM_EPS) * gain_ref[...]


def _final_norm(x2, gain, tm):
    t, d = x2.shape
    return pl.pallas_call(
        _final_norm_kernel,
        grid=(t // tm,),
        in_specs=[pl.BlockSpec((tm, d), lambda i: (i, 0)), pl.BlockSpec((1, d), lambda i: (0, 0))],
        out_specs=pl.BlockSpec((tm, d), lambda i: (i, 0)),
        out_shape=jax.ShapeDtypeStruct((t, d), F32),
        compiler_params=pltpu.CompilerParams(dimension_semantics=("parallel",),
                                             vmem_limit_bytes=VMEM_LIMIT),
        name="final_norm",
    )(x2, gain)


def _moe(x2, gain, scale, shift, gate, wr_t, bias_col, w_gate, w_up, w_down, seq, tm):
    t, d = x2.shape
    h, experts, gates = _route(x2, gain, scale, shift, wr_t, bias_col, seq, tm)
    flat_e = experts.T.reshape(-1)
    n_assign = flat_e.shape[0]
    onehot = (flat_e[:, None] == jnp.arange(N_EXPERTS, dtype=jnp.int32)[None, :]).astype(jnp.int32)
    running = jnp.cumsum(onehot, axis=0)
    counts = running[-1]
    rank = jnp.sum((running - onehot) * onehot, axis=1)
    padded = (counts + MOE_ROWS - 1) // MOE_ROWS * MOE_ROWS
    pad_end = jnp.cumsum(padded)
    pad_start = pad_end - padded
    dest = pad_start[flat_e] + rank
    n_blocks = n_assign // MOE_ROWS + N_EXPERTS
    n_pad = n_blocks * MOE_ROWS
    tok = jnp.arange(n_assign, dtype=jnp.int32) // 2
    buf_tok = jnp.zeros((n_pad,), jnp.int32).at[dest].set(tok)
    buf_w = jnp.zeros((n_pad,), F32).at[dest].set(gates.T.reshape(-1))
    starts = jnp.arange(n_blocks, dtype=jnp.int32) * MOE_ROWS
    block_expert = jnp.minimum(jnp.searchsorted(pad_end, starts, side="right"), N_EXPERTS - 1).astype(jnp.int32)
    block_valid = (starts < pad_end[-1]).astype(jnp.int32)
    xb = jnp.take(h, buf_tok, axis=0)
    yb = _experts(xb, w_gate, w_up, w_down, buf_w.reshape(n_pad, 1), block_expert, block_valid)
    ypair = jnp.take(yb, dest, axis=0).reshape(t, 2 * d)
    return _combine(x2, ypair, gate, seq, tm)


def kernel(x, c, norm_gain, w_ada, b_ada, w_in_even, w_in_odd, w_out, a_mu, a_w0, a_w_up, a_a0, a_a_up,
           a_g_up, a_k_k, a_k_a, a_r_k, a_ln_gain, a_ln_bias, b_alpha_up, b_alpha_bias, b_norm_gain, c_lb,
           c_norm_gain, d_conv_w, d_conv_b, d_dt_bias, d_a_log, d_skip, d_norm_gain, w_router, router_bias,
           w_gate, w_up, w_down, final_gain):
    bsz, seq, d = x.shape
    depth = w_ada.shape[0]
    t = bsz * seq
    tm = min(512, seq)
    cum_sel, pair_masks = _chunk_constants()
    ind512 = _head_indicator(MIX_HALF, HEAD_DIM)
    expand = jnp.asarray((np.arange(LANE)[:, None] == np.arange(MIX_HALF)[None, :] // HEAD_DIM).astype(np.float32))

    mods = _ada(c, w_ada, b_ada).reshape(depth, bsz, 6, 1, d)
    lb_p = jax.nn.softmax(c_lb.astype(F32), axis=0)
    lb_cum = jnp.cumsum(lb_p, axis=0)
    lower_bounds = lb_cum - lb_cum[0:1]
    wr_t = w_router.T
    bias_col = router_bias.reshape(N_EXPERTS, 1)

    x2 = x.reshape(t, d)
    for l in range(depth):
        j = l // 2
        sh_m, sc_m, g_m, sh_f, sc_f, g_f = [mods[l, :, i] for i in range(6)]
        gain_m, gain_f = norm_gain[l, 0].reshape(1, d), norm_gain[l, 1].reshape(1, d)
        if l % 2 == 0:
            w = w_in_even[j]
            zpad = jnp.zeros((d, LANE - GLA_GATE_RANK), w.dtype)
            w = jnp.concatenate([w[:, :RWKV_IN + 1024 + GLA_GATE_RANK], zpad,
                                 w[:, RWKV_IN + 1024 + GLA_GATE_RANK:]], axis=1).astype(BF16)
            u = _in_proj(x2, gain_m, sc_m, sh_m, w, seq, tm)
            alpha_up = jnp.concatenate([b_alpha_up[j], jnp.zeros((LANE - GLA_GATE_RANK, 256), F32)], axis=0)
            p = dict(mu=_row(a_mu[j]), w0=_row(a_w0[j]), w_up=a_w_up[j], a0=_row(a_a0[j]), a_up=a_a_up[j],
                     g_up=a_g_up[j], k_k=_row(a_k_k[j]), k_a=_row(a_k_a[j]), r_k=_row(a_r_k[j]),
                     ln_gain=_row(a_ln_gain[j]), ln_bias=_row(a_ln_bias[j]), alpha_up=alpha_up,
                     alpha_bias=_row(b_alpha_bias[j]), b_norm_gain=_row(b_norm_gain[j]))
            y = _mix_even(u.reshape(bsz, seq, EVEN_COLS), p, (cum_sel, pair_masks, ind512))
        else:
            w = w_in_odd[j]
            zpad = jnp.zeros((d, LANE - SSD_HEADS), w.dtype)
            w = jnp.concatenate([w, zpad], axis=1).astype(BF16)
            u = _in_proj(x2, gain_m, sc_m, sh_m, w, seq, tm)
            dt_bias = jnp.concatenate([d_dt_bias[j], jnp.zeros((LANE - SSD_HEADS,), F32)])
            p = dict(lb=_row(lower_bounds[l]), c_norm_gain=_row(c_norm_gain[j]), conv_w=d_conv_w[j],
                     conv_b=_row(d_conv_b[j]), dt_bias=_row(dt_bias),
                     a_log=_row(jnp.repeat(d_a_log[j], HEAD_DIM)), skip=_row(jnp.repeat(d_skip[j], HEAD_DIM)),
                     d_norm_gain=_row(d_norm_gain[j]))
            y = _mix_odd(u.reshape(bsz, seq, ODD_COLS), p, (cum_sel, pair_masks, expand))
        x2 = _out_proj(y.reshape(t, d), w_out[l].astype(BF16), x2, g_m, seq, tm)
        x2 = _moe(x2, gain_f, sc_f, sh_f, g_f, wr_t, bias_col, w_gate[l], w_up[l], w_down[l], seq, tm)
    return _final_norm(x2, final_gain.reshape(1, d), tm).reshape(bsz, seq, d)
```

```python
import functools

import numpy as np
import jax
import jax.numpy as jnp
from jax import lax
from jax.experimental import pallas as pl
from jax.experimental.pallas import tpu as pltpu

F32 = jnp.float32
BF16 = jnp.bfloat16
HIGHEST = lax.Precision.HIGHEST

D_MODEL = 1024
MIX_HALF = 512
HEAD_DIM = 64
CHUNK = 64
NORM_EPS = 1e-6
RWKV_HEADS = 8
RWKV_IN = 1792
RWKV_GN_EPS = 64e-5
RWKV_DECAY_SCALE = float(np.exp(-0.5))
GLA_HEADS = 4
GLA_GATE_RANK = 16
GLA_GATE_NORM = 16.0
SSD_HEADS = 8
SSD_GROUPS = 2
SSD_STATE = 64
SSD_CONV = 4
SSD_XBC = 768
N_EXPERTS = 16
N_EXPERT_GROUPS = 4
EXPERTS_PER_GROUP = 4
D_EXPERT = 512
LANE = 128
EVEN_COLS = 3456
ODD_COLS = 2944
MOE_ROWS = 256
VMEM_LIMIT = 48 * 1024 * 1024


def _dot(a, b):
    return jnp.dot(a.astype(BF16), b.astype(BF16), preferred_element_type=F32)


def _dot_nt(a, b):
    return lax.dot_general(a.astype(BF16), b.astype(BF16), (((1,), (1,)), ((), ())),
                           preferred_element_type=F32)


def _dot_tn(a, b):
    return lax.dot_general(a.astype(BF16), b.astype(BF16), (((0,), (0,)), ((), ())),
                           preferred_element_type=F32)


def _dot_exact(a, b):
    return jnp.dot(a, b, precision=HIGHEST, preferred_element_type=F32)


def _sigmoid(x):
    return 1.0 / (1.0 + jnp.exp(-x))


def _silu(x):
    return x * _sigmoid(x)


def _softplus(x):
    return jnp.maximum(x, 0.0) + jnp.log(1.0 + jnp.exp(-jnp.abs(x)))


def _log_sigmoid(x):
    return jnp.minimum(x, 0.0) - jnp.log(1.0 + jnp.exp(-jnp.abs(x)))


def _chunk_constants():
    t = np.arange(CHUNK)
    tril = (t[None, :] <= t[:, None]).astype(np.float32)
    rows = [tril]
    masks = []
    for shift in range(5, -1, -1):
        n = 2 << shift
        mid = (t & ~(n - 1)) + (n >> 1) - 1
        rows.append(tril[mid])
        masks.append(((t[:, None] > t[None, :]) & (((t[:, None] ^ t[None, :]) >> shift) == 1)))
    masks.append(t[:, None] == t[None, :])
    cum_sel = np.concatenate(rows, axis=0)
    pair_masks = np.stack(masks).astype(np.float32)
    return jnp.asarray(cum_sel), jnp.asarray(pair_masks)


def _head_indicator(width, seg):
    i = np.arange(width)
    return jnp.asarray((i[:, None] // seg == i[None, :] // seg).astype(np.float32), dtype=BF16)


def _ada_kernel(c_ref, w_ref, b_ref, o_ref):
    cond = _silu(c_ref[...])
    o_ref[0] = _dot_exact(cond, w_ref[0]) + b_ref[0]


def _ada(c, w_ada, b_ada):
    depth, d, n = w_ada.shape
    bsz = c.shape[0]
    tn = 1536
    return pl.pallas_call(
        _ada_kernel,
        grid=(depth, n // tn),
        in_specs=[pl.BlockSpec((bsz, d), lambda l, j: (0, 0)),
                  pl.BlockSpec((1, d, tn), lambda l, j: (l, 0, j)),
                  pl.BlockSpec((1, 1, tn), lambda l, j: (l, 0, j))],
        out_specs=pl.BlockSpec((1, bsz, tn), lambda l, j: (l, 0, j)),
        out_shape=jax.ShapeDtypeStruct((depth, bsz, n), F32),
        compiler_params=pltpu.CompilerParams(dimension_semantics=("parallel", "parallel"),
                                             vmem_limit_bytes=VMEM_LIMIT),
        name="ada",
    )(c, w_ada, b_ada.reshape(depth, 1, n))


def _modulated_norm(x, gain, scale, shift):
    ms = jnp.mean(x * x, axis=-1, keepdims=True)
    return (x * lax.rsqrt(ms + NORM_EPS)) * gain * (1.0 + scale) + shift


def _in_proj_kernel(x_ref, gain_ref, sc_ref, sh_ref, w_ref, o_ref, *, col_chunk):
    h = _modulated_norm(x_ref[...], gain_ref[...], sc_ref[0], sh_ref[0]).astype(BF16)
    n = o_ref.shape[1]
    for j in range(0, n, col_chunk):
        o_ref[:, j:j + col_chunk] = jnp.dot(h, w_ref[:, j:j + col_chunk],
                                            preferred_element_type=F32).astype(o_ref.dtype)


def _in_proj(x2, gain, scale, shift, w, seq, tm):
    t, d = x2.shape
    n = w.shape[1]
    col_chunk = n // 3 if (n // 3) % LANE == 0 and n % 3 == 0 else n
    per_b = seq // tm
    return pl.pallas_call(
        functools.partial(_in_proj_kernel, col_chunk=col_chunk),
        grid=(t // tm,),
        in_specs=[pl.BlockSpec((tm, d), lambda i: (i, 0)),
                  pl.BlockSpec((1, d), lambda i: (0, 0)),
                  pl.BlockSpec((1, 1, d), lambda i: (i // per_b, 0, 0)),
                  pl.BlockSpec((1, 1, d), lambda i: (i // per_b, 0, 0)),
                  pl.BlockSpec((d, n), lambda i: (0, 0))],
        out_specs=pl.BlockSpec((tm, n), lambda i: (i, 0)),
        out_shape=jax.ShapeDtypeStruct((t, n), BF16),
        compiler_params=pltpu.CompilerParams(dimension_semantics=("parallel",),
                                             vmem_limit_bytes=VMEM_LIMIT),
        name="in_proj",
    )(x2, gain, scale, shift, w)


def _gla_chunk(q, k, v, g, st_ref, cum_sel, pair_masks, heads, dk, dv):
    ball = _dot_exact(cum_sel, g)
    b = ball[0:CHUNK]
    b_last = b[CHUNK - 1:CHUNK]
    q_in = q * jnp.exp(b)
    k_st = k * jnp.exp(b_last - b)
    decay = jnp.exp(b_last)
    scores = [None] * heads
    for lvl in range(7):
        if lvl < 6:
            e = jnp.exp(-jnp.abs(b - ball[CHUNK * (lvl + 1):CHUNK * (lvl + 2)]))
            qe, ke = q * e, k * e
        else:
            qe, ke = q, k
        keep = pair_masks[lvl] > 0.5
        for h in range(heads):
            p = _dot_nt(qe[:, h * dk:(h + 1) * dk], ke[:, h * dk:(h + 1) * dk])
            p = jnp.where(keep, p, 0.0)
            scores[h] = p if scores[h] is None else scores[h] + p
    hs = range(heads)
    ks = [slice(h * dk, (h + 1) * dk) for h in hs]
    vs = [slice(h * dv, (h + 1) * dv) for h in hs]
    st = [st_ref[h] for h in hs]
    o_inter = [_dot_nt(q_in[:, ks[h]], st[h]) for h in hs]
    o_intra = [_dot(scores[h], v[:, vs[h]]) for h in hs]
    st_new = [st[h] * decay[:, ks[h]] + _dot_tn(v[:, vs[h]], k_st[:, ks[h]]) for h in hs]
    for h in hs:
        st_ref[h] = st_new[h]
    return jnp.concatenate([o_inter[h] + o_intra[h] for h in hs], axis=-1)


def _head_rms(o, gain, heads, dv):
    outs = []
    for h in range(heads):
        oh = o[:, h * dv:(h + 1) * dv]
        ms = jnp.mean(oh * oh, axis=-1, keepdims=True)
        outs.append(oh * lax.rsqrt(ms + NORM_EPS) * gain)
    return jnp.concatenate(outs, axis=-1)


def _rwkv_chunk(r, kk, a, kt, v, logw, s_ref, cum_sel, pair_masks):
    tril = cum_sel[0:CHUNK]
    b = _dot_exact(tril, logw)
    b_last = b[CHUNK - 1:CHUNK]
    e_neg = jnp.exp(-b)
    e_end = jnp.exp(b_last - b)
    decay = jnp.exp(b_last)
    beta = a * kk
    k_bar = kk * jnp.exp(b - logw)
    r_bar = r * jnp.exp(b)
    beta_t, k_t = beta * e_neg, kt * e_neg
    beta_hat, k_hat = beta * e_end, kt * e_end
    row = lax.broadcasted_iota(jnp.int32, (CHUNK, CHUNK), 0)
    col = lax.broadcasted_iota(jnp.int32, (CHUNK, CHUNK), 1)
    strict = col < row
    incl = col <= row
    same_blk = (row >> 4) == (col >> 4)
    eye = (row == col).astype(F32)
    hs = range(RWKV_HEADS)
    sls = [slice(h * HEAD_DIM, (h + 1) * HEAD_DIM) for h in hs]
    kr = [jnp.concatenate([k_bar[:, sl], r_bar[:, sl]], axis=0) for sl in sls]
    bk = [jnp.concatenate([beta_t[:, sl], k_t[:, sl]], axis=0) for sl in sls]
    s0 = [s_ref[h] for h in hs]
    vh = [v[:, sl] for sl in sls]
    m1 = [_dot_nt(kr[h], bk[h]) for h in hs]
    m2 = [_dot_nt(kr[h], s0[h]) for h in hs]
    a_m = [jnp.where(strict, m[0:CHUNK, 0:CHUNK], 0.0) for m in m1]
    b_m = [jnp.where(strict, m[0:CHUNK, CHUNK:], 0.0) for m in m1]
    cb_m = [jnp.where(incl, m[CHUNK:, 0:CHUNK], 0.0) for m in m1]
    ck_m = [jnp.where(incl, m[CHUNK:, CHUNK:], 0.0) for m in m1]
    x1 = [jnp.where(same_blk, -am, 0.0) for am in a_m]
    a_off = [jnp.where(same_blk, 0.0, am) for am in a_m]
    rhs = [m2[h][0:CHUNK] + _dot(b_m[h], vh[h]) for h in hs]
    x2 = [_dot(x, x) for x in x1]
    p = [eye + x1[h] for h in hs]
    p = [p[h] + _dot(p[h], x2[h]) for h in hs]
    x4 = [_dot(x, x) for x in x2]
    p = [p[h] + _dot(p[h], x4[h]) for h in hs]
    x8 = [_dot(x, x) for x in x4]
    t_d = [p[h] + _dot(p[h], x8[h]) for h in hs]
    n_m = [_dot(t_d[h], a_off[h]) for h in hs]
    z = [_dot(t_d[h], rhs[h]) for h in hs]
    n2 = [_dot(n, n) for n in n_m]
    z = [z[h] + _dot(n2[h], z[h]) for h in hs]
    u = [z[h] - _dot(n_m[h], z[h]) for h in hs]
    outs = [m2[h][CHUNK:] + _dot(ck_m[h], vh[h]) - _dot(cb_m[h], u[h]) for h in hs]
    s_new = [s0[h] * decay[:, sls[h]] + _dot_tn(vh[h], k_hat[:, sls[h]]) - _dot_tn(u[h], beta_hat[:, sls[h]])
             for h in hs]
    for h in hs:
        s_ref[h] = s_new[h]
    return jnp.concatenate(outs, axis=-1)


def _mix_even_kernel(u_ref, mu_ref, w0_ref, wup_ref, a0_ref, aup_ref, gup_ref, kk_ref, ka_ref,
                     rk_ref, lng_ref, lnb_ref, alup_ref, albias_ref, bng_ref, ind_ref, csel_ref,
                     pmask_ref, y_ref, s_ref, g_ref, prev_ref):
    @pl.when(pl.program_id(1) == 0)
    def _():
        s_ref[...] = jnp.zeros_like(s_ref)
        g_ref[...] = jnp.zeros_like(g_ref)
        prev_ref[...] = jnp.zeros_like(prev_ref)

    u = u_ref[0].astype(F32)
    cum_sel = csel_ref[...]
    pair_masks = pmask_ref[...]
    ind = ind_ref[...]
    ua = u[:, 0:RWKV_IN]
    rows = lax.broadcasted_iota(jnp.int32, ua.shape, 0)
    shifted = jnp.where(rows == 0, prev_ref[...], pltpu.roll(ua, 1, axis=0))
    prev_ref[...] = ua[CHUNK - 1:CHUNK]
    xa = ua + mu_ref[...] * (shifted - ua)
    r, k, v = xa[:, 0:512], xa[:, 512:1024], xa[:, 1024:1536]
    wd, ad, gd = xa[:, 1536:1600], xa[:, 1600:1664], xa[:, 1664:1792]
    logw = -RWKV_DECAY_SCALE * _sigmoid(w0_ref[...] + _dot(jnp.tanh(wd), wup_ref[...]))
    a = _sigmoid(a0_ref[...] + _dot(ad, aup_ref[...]))
    gate = _dot(_sigmoid(gd), gup_ref[...])
    kk = k * kk_ref[...]
    kk = kk * lax.rsqrt(_dot(kk * kk, ind) + 1e-12)
    kt = k * (1.0 + (a - 1.0) * ka_ref[...])
    y = _rwkv_chunk(r, kk, a, kt, v, logw, s_ref, cum_sel, pair_masks)
    mean = _dot(y, ind) * (1.0 / HEAD_DIM)
    yc = y - mean
    var = _dot(yc * yc, ind) * (1.0 / HEAD_DIM)
    y = yc * lax.rsqrt(var + RWKV_GN_EPS) * lng_ref[...] + lnb_ref[...]
    y = y + _dot(r * kt * rk_ref[...], ind) * v
    y_ref[0, :, 0:MIX_HALF] = (y * gate).astype(y_ref.dtype)
    ub = u[:, RWKV_IN:]
    q, kg, vg = ub[:, 0:256] * (HEAD_DIM ** -0.5), ub[:, 256:512], ub[:, 512:1024]
    alpha, gg = ub[:, 1024:1152], ub[:, 1152:1664]
    log_a = _log_sigmoid(_dot(alpha, alup_ref[...]) + albias_ref[...]) * (1.0 / GLA_GATE_NORM)
    o = _gla_chunk(q, kg, vg, log_a, g_ref, cum_sel, pair_masks, GLA_HEADS, 64, 128)
    o = _head_rms(o, bng_ref[...], GLA_HEADS, 128)
    y_ref[0, :, MIX_HALF:] = (o * _silu(gg)).astype(y_ref.dtype)


def _row(p):
    return p.reshape(1, -1).astype(F32)


def _mix_even(u3, p, consts):
    bsz, seq, n = u3.shape
    cum_sel, pair_masks, ind512 = consts
    small = [p["mu"], p["w0"], p["w_up"], p["a0"], p["a_up"], p["g_up"], p["k_k"], p["k_a"], p["r_k"],
             p["ln_gain"], p["ln_bias"], p["alpha_up"], p["alpha_bias"], p["b_norm_gain"],
             ind512, cum_sel, pair_masks]

    def full(arr):
        nd = arr.ndim
        return pl.BlockSpec(arr.shape, lambda b, s, _nd=nd: (0,) * _nd)

    return pl.pallas_call(
        _mix_even_kernel,
        grid=(bsz, seq // CHUNK),
        in_specs=[pl.BlockSpec((1, CHUNK, n), lambda b, s: (b, s, 0))] + [full(a) for a in small],
        out_specs=pl.BlockSpec((1, CHUNK, D_MODEL), lambda b, s: (b, s, 0)),
        out_shape=jax.ShapeDtypeStruct((bsz, seq, D_MODEL), BF16),
        scratch_shapes=[pltpu.VMEM((RWKV_HEADS, HEAD_DIM, HEAD_DIM), F32),
                        pltpu.VMEM((GLA_HEADS, 128, 64), F32),
                        pltpu.VMEM((1, RWKV_IN), F32)],
        compiler_params=pltpu.CompilerParams(dimension_semantics=("parallel", "arbitrary"),
                                             vmem_limit_bytes=VMEM_LIMIT),
        name="mix_even",
    )(u3, *small)


def _mix_odd_kernel(u_ref, lb_ref, cng_ref, convw_ref, convb_ref, dtb_ref, alog_ref, skip_ref,
                    dng_ref, expand_ref, csel_ref, pmask_ref, y_ref, h_ref, d_ref, tail_ref):
    @pl.when(pl.program_id(1) == 0)
    def _():
        h_ref[...] = jnp.zeros_like(h_ref)
        d_ref[...] = jnp.zeros_like(d_ref)
        tail_ref[...] = jnp.zeros_like(tail_ref)

    u = u_ref[0].astype(F32)
    cum_sel = csel_ref[...]
    pair_masks = pmask_ref[...]
    q, fr, iv, g = u[:, 0:256], u[:, 256:512], u[:, 512:1024], u[:, 1024:1536]
    lb = lb_ref[...]
    f = lb + (1.0 - lb) * _sigmoid(fr)
    o = _gla_chunk(q, 1.0 - f, iv, jnp.log(f), h_ref, cum_sel, pair_masks, 4, 64, 128)
    o = _head_rms(o, cng_ref[...], 4, 128)
    y_ref[0, :, 0:MIX_HALF] = (o * _silu(g)).astype(y_ref.dtype)
    z, xbc, dt_raw = u[:, 1536:2048], u[:, 2048:2816], u[:, 2816:2944]
    tail = tail_ref[...]
    tail_ref[...] = xbc[CHUNK - 8:CHUNK]
    rows8 = lax.broadcasted_iota(jnp.int32, (8, SSD_XBC), 0)
    conv = xbc * convw_ref[SSD_CONV - 1:SSD_CONV] + convb_ref[...]
    for back in range(1, SSD_CONV):
        rolled = pltpu.roll(xbc, back, axis=0)
        head8 = jnp.where(rows8 < back, pltpu.roll(tail, back, axis=0), rolled[0:8])
        shifted = jnp.concatenate([head8, rolled[8:]], axis=0)
        conv = conv + shifted * convw_ref[SSD_CONV - 1 - back:SSD_CONV - back]
    xbc = _silu(conv)
    xs, bmat, cmat = xbc[:, 0:512], xbc[:, 512:640], xbc[:, 640:768]
    tril = cum_sel[0:CHUNK]
    dt = _dot_exact(_softplus(dt_raw + dtb_ref[...]), expand_ref[...])
    da = dt * (-jnp.exp(alog_ref[...]))
    cum = _dot_exact(tril, da)
    cum_last = cum[CHUNK - 1:CHUNK]
    e_cum = jnp.exp(cum)
    xdt = xs * dt
    x_end = xdt * jnp.exp(cum_last - cum)
    decay = jnp.exp(cum_last)
    row = lax.broadcasted_iota(jnp.int32, (CHUNK, CHUNK), 0)
    col = lax.broadcasted_iota(jnp.int32, (CHUNK, CHUNK), 1)
    causal = col <= row
    outs = []
    for grp in range(SSD_GROUPS):
        gs = slice(grp * 256, (grp + 1) * 256)
        ns = slice(grp * SSD_STATE, (grp + 1) * SSD_STATE)
        cb = _dot_nt(cmat[:, ns], bmat[:, ns])
        st = d_ref[grp]
        y_off = _dot(cmat[:, ns], st) * e_cum[:, gs]
        d_ref[grp] = st * decay[:, gs] + _dot_tn(bmat[:, ns], x_end[:, gs])
        for rep in range(SSD_HEADS // SSD_GROUPS):
            hs = slice(grp * 256 + rep * HEAD_DIM, grp * 256 + (rep + 1) * HEAD_DIM)
            cum_h = cum[:, hs]
            seg = jnp.where(causal, jnp.exp(cum_h - cum_h.T), 0.0)
            outs.append(_dot(cb * seg, xdt[:, hs]))
        outs[-4:] = [jnp.concatenate(outs[-4:], axis=-1) + y_off]
    y = jnp.concatenate(outs, axis=-1) + skip_ref[...] * xs
    y = y * _silu(z)
    parts = []
    for grp in range(SSD_GROUPS):
        yg = y[:, grp * 256:(grp + 1) * 256]
        ms = jnp.mean(yg * yg, axis=-1, keepdims=True)
        parts.append(yg * lax.rsqrt(ms + NORM_EPS))
    y_ref[0, :, MIX_HALF:] = (jnp.concatenate(parts, axis=-1) * dng_ref[...]).astype(y_ref.dtype)


def _mix_odd(u3, p, consts):
    bsz, seq, n = u3.shape
    cum_sel, pair_masks, expand = consts
    small = [p["lb"], p["c_norm_gain"], p["conv_w"], p["conv_b"], p["dt_bias"], p["a_log"], p["skip"],
             p["d_norm_gain"], expand, cum_sel, pair_masks]

    def full(arr):
        nd = arr.ndim
        return pl.BlockSpec(arr.shape, lambda b, s, _nd=nd: (0,) * _nd)

    return pl.pallas_call(
        _mix_odd_kernel,
        grid=(bsz, seq // CHUNK),
        in_specs=[pl.BlockSpec((1, CHUNK, n), lambda b, s: (b, s, 0))] + [full(a) for a in small],
        out_specs=pl.BlockSpec((1, CHUNK, D_MODEL), lambda b, s: (b, s, 0)),
        out_shape=jax.ShapeDtypeStruct((bsz, seq, D_MODEL), BF16),
        scratch_shapes=[pltpu.VMEM((4, 128, 64), F32),
                        pltpu.VMEM((SSD_GROUPS, SSD_STATE, 256), F32),
                        pltpu.VMEM((8, SSD_XBC), F32)],
        compiler_params=pltpu.CompilerParams(dimension_semantics=("parallel", "arbitrary"),
                                             vmem_limit_bytes=VMEM_LIMIT),
        name="mix_odd",
    )(u3, *small)


def _out_proj_kernel(y_ref, w_ref, x_ref, g_ref, o_ref):
    o_ref[...] = x_ref[...] + g_ref[0] * jnp.dot(y_ref[...], w_ref[...], preferred_element_type=F32)


def _out_proj(y2, w, x2, gate, seq, tm):
    t, d = x2.shape
    per_b = seq // tm
    return pl.pallas_call(
        _out_proj_kernel,
        grid=(t // tm,),
        in_specs=[pl.BlockSpec((tm, d), lambda i: (i, 0)),
                  pl.BlockSpec((d, d), lambda i: (0, 0)),
                  pl.BlockSpec((tm, d), lambda i: (i, 0)),
                  pl.BlockSpec((1, 1, d), lambda i: (i // per_b, 0, 0))],
        out_specs=pl.BlockSpec((tm, d), lambda i: (i, 0)),
        out_shape=jax.ShapeDtypeStruct((t, d), F32),
        compiler_params=pltpu.CompilerParams(dimension_semantics=("parallel",),
                                             vmem_limit_bytes=VMEM_LIMIT),
        name="out_proj",
    )(y2, w, x2, gate)


def _route_kernel(x_ref, gain_ref, sc_ref, sh_ref, wr_ref, bias_ref, tri_ref, h_ref, e_ref, gt_ref, r_ref,
                  cnt_ref):
    h = _modulated_norm(x_ref[...], gain_ref[...], sc_ref[0], sh_ref[0])
    h_ref[...] = h.astype(h_ref.dtype)
    logits = lax.dot_general(wr_ref[...], h, (((1,), (1,)), ((), ())), precision=HIGHEST,
                             preferred_element_type=F32)
    score = _sigmoid(logits)
    sel = score + bias_ref[...]
    gscore = []
    for grp in range(N_EXPERT_GROUPS):
        a, b, c, d = [sel[grp * 4 + j:grp * 4 + j + 1] for j in range(4)]
        hi1, lo1, hi2, lo2 = jnp.maximum(a, b), jnp.minimum(a, b), jnp.maximum(c, d), jnp.minimum(c, d)
        gscore.append(jnp.maximum(hi1, hi2) + jnp.maximum(jnp.minimum(hi1, hi2), jnp.maximum(lo1, lo2)))
    best, gidx = gscore[0], jnp.zeros_like(gscore[0], dtype=jnp.int32)
    for grp in range(1, N_EXPERT_GROUPS):
        better = gscore[grp] > best
        gidx = jnp.where(better, grp, gidx)
        best = jnp.where(better, gscore[grp], best)
    vals, raw = [], []
    for j in range(EXPERTS_PER_GROUP):
        vj, rj = sel[j:j + 1], score[j:j + 1]
        for grp in range(1, N_EXPERT_GROUPS):
            vj = jnp.where(gidx == grp, sel[grp * 4 + j:grp * 4 + j + 1], vj)
            rj = jnp.where(gidx == grp, score[grp * 4 + j:grp * 4 + j + 1], rj)
        vals.append(vj)
        raw.append(rj)
    i1, m1, g1 = jnp.zeros_like(gidx), vals[0], raw[0]
    for j in range(1, EXPERTS_PER_GROUP):
        better = vals[j] > m1
        i1 = jnp.where(better, j, i1)
        m1 = jnp.where(better, vals[j], m1)
        g1 = jnp.where(better, raw[j], g1)
    i2, m2, g2 = jnp.zeros_like(gidx), jnp.full_like(m1, -jnp.inf), jnp.zeros_like(m1)
    for j in range(EXPERTS_PER_GROUP):
        better = jnp.logical_and(i1 != j, vals[j] > m2)
        i2 = jnp.where(better, j, i2)
        m2 = jnp.where(better, vals[j], m2)
        g2 = jnp.where(better, raw[j], g2)
    total = g1 + g2
    e1 = gidx * EXPERTS_PER_GROUP + i1
    e2 = gidx * EXPERTS_PER_GROUP + i2
    e_ref[0:1, :] = e1
    e_ref[1:2, :] = e2
    gt_ref[0:1, :] = g1 / total
    gt_ref[1:2, :] = g2 / total
    @pl.when(pl.program_id(0) == 0)
    def _():
        cnt_ref[...] = jnp.zeros_like(cnt_ref)

    eid = lax.broadcasted_iota(jnp.int32, logits.shape, 0)
    is1, is2 = eid == e1, eid == e2
    member = jnp.where(jnp.logical_or(is1, is2), 1.0, 0.0)
    before = jnp.dot(member.astype(BF16), tri_ref[...], preferred_element_type=F32) - member
    base = cnt_ref[:, 0:1] + before
    r_ref[0:1, :] = jnp.sum(jnp.where(is1, base, 0.0), axis=0, keepdims=True).astype(jnp.int32)
    r_ref[1:2, :] = jnp.sum(jnp.where(is2, base, 0.0), axis=0, keepdims=True).astype(jnp.int32)
    cnt_ref[...] = cnt_ref[...] + jnp.sum(member, axis=1, keepdims=True)


def _route(x2, gain, scale, shift, wr_t, bias_col, seq, tm):
    t, d = x2.shape
    per_b = seq // tm
    tri = jnp.asarray(np.triu(np.ones((tm, tm), np.float32)), dtype=BF16)
    return pl.pallas_call(
        _route_kernel,
        grid=(t // tm,),
        in_specs=[pl.BlockSpec((tm, d), lambda i: (i, 0)),
                  pl.BlockSpec((1, d), lambda i: (0, 0)),
                  pl.BlockSpec((1, 1, d), lambda i: (i // per_b, 0, 0)),
                  pl.BlockSpec((1, 1, d), lambda i: (i // per_b, 0, 0)),
                  pl.BlockSpec((N_EXPERTS, d), lambda i: (0, 0)),
                  pl.BlockSpec((N_EXPERTS, 1), lambda i: (0, 0)),
                  pl.BlockSpec((tm, tm), lambda i: (0, 0))],
        out_specs=[pl.BlockSpec((tm, d), lambda i: (i, 0)),
                   pl.BlockSpec((2, tm), lambda i: (0, i)),
                   pl.BlockSpec((2, tm), lambda i: (0, i)),
                   pl.BlockSpec((2, tm), lambda i: (0, i)),
                   pl.BlockSpec((N_EXPERTS, LANE), lambda i: (0, 0))],
        out_shape=[jax.ShapeDtypeStruct((t, d), BF16),
                   jax.ShapeDtypeStruct((2, t), jnp.int32),
                   jax.ShapeDtypeStruct((2, t), F32),
                   jax.ShapeDtypeStruct((2, t), jnp.int32),
                   jax.ShapeDtypeStruct((N_EXPERTS, LANE), F32)],
        compiler_params=pltpu.CompilerParams(dimension_semantics=("arbitrary",),
                                             vmem_limit_bytes=VMEM_LIMIT),
        name="route",
    )(x2, gain, scale, shift, wr_t, bias_col, tri)


def _expert_kernel(be_ref, bv_ref, x_ref, wg_ref, wu_ref, wd_ref, o_ref):
    i = pl.program_id(0)

    @pl.when(bv_ref[i] > 0)
    def _():
        x = x_ref[...]
        gate = jnp.dot(x, wg_ref[0, 0].astype(BF16), preferred_element_type=F32)
        up = jnp.dot(x, wu_ref[0, 0].astype(BF16), preferred_element_type=F32)
        act = (_silu(gate) * up).astype(BF16)
        y = jnp.dot(act, wd_ref[0, 0].astype(BF16), preferred_element_type=F32)
        o_ref[...] = y.astype(o_ref.dtype)

    @pl.when(bv_ref[i] == 0)
    def _():
        o_ref[...] = jnp.zeros_like(o_ref)


def _experts(xb, w_gate, w_up, w_down, layer, block_expert, block_valid):
    n_pad, d = xb.shape
    n_blocks = n_pad // MOE_ROWS
    de = w_gate.shape[-1]
    grid_spec = pltpu.PrefetchScalarGridSpec(
        num_scalar_prefetch=2,
        grid=(n_blocks,),
        in_specs=[pl.BlockSpec((MOE_ROWS, d), lambda i, be, bv: (i, 0)),
                  pl.BlockSpec((1, 1, d, de), lambda i, be, bv: (layer, be[i], 0, 0)),
                  pl.BlockSpec((1, 1, d, de), lambda i, be, bv: (layer, be[i], 0, 0)),
                  pl.BlockSpec((1, 1, de, d), lambda i, be, bv: (layer, be[i], 0, 0))],
        out_specs=pl.BlockSpec((MOE_ROWS, d), lambda i, be, bv: (i, 0)),
    )
    return pl.pallas_call(
        _expert_kernel,
        grid_spec=grid_spec,
        out_shape=jax.ShapeDtypeStruct((n_pad, d), BF16),
        compiler_params=pltpu.CompilerParams(dimension_semantics=("arbitrary",),
                                             vmem_limit_bytes=VMEM_LIMIT),
        name="experts",
    )(block_expert, block_valid, xb, w_gate, w_up, w_down)


def _combine_kernel(x_ref, y0_ref, y1_ref, w_ref, g_ref, o_ref):
    w = w_ref[...]
    moe = y0_ref[...].astype(F32) * w[:, 0:1] + y1_ref[...].astype(F32) * w[:, 1:2]
    o_ref[...] = x_ref[...] + g_ref[0] * moe


def _combine(x2, y0, y1, weights, gate, seq, tm):
    t, d = x2.shape
    per_b = seq // tm
    return pl.pallas_call(
        _combine_kernel,
        grid=(t // tm,),
        in_specs=[pl.BlockSpec((tm, d), lambda i: (i, 0)),
                  pl.BlockSpec((tm, d), lambda i: (i, 0)),
                  pl.BlockSpec((tm, d), lambda i: (i, 0)),
                  pl.BlockSpec((tm, 2), lambda i: (i, 0)),
                  pl.BlockSpec((1, 1, d), lambda i: (i // per_b, 0, 0))],
        out_specs=pl.BlockSpec((tm, d), lambda i: (i, 0)),
        out_shape=jax.ShapeDtypeStruct((t, d), F32),
        compiler_params=pltpu.CompilerParams(dimension_semantics=("parallel",),
                                             vmem_limit_bytes=VMEM_LIMIT),
        name="combine",
    )(x2, y0, y1, weights, gate)


def _final_norm_kernel(x_ref, gain_ref, o_ref):
    x = x_ref[...]
    ms = jnp.mean(x * x, axis=-1, keepdims=True)
    o_ref[...] = x * lax.rsqrt(ms + NORM_EPS) * gain_ref[...]


def _final_norm(x2, gain, tm):
    t, d = x2.shape
    return pl.pallas_call(
        _final_norm_kernel,
        grid=(t // tm,),
        in_specs=[pl.BlockSpec((tm, d), lambda i: (i, 0)), pl.BlockSpec((1, d), lambda i: (0, 0))],
        out_specs=pl.BlockSpec((tm, d), lambda i: (i, 0)),
        out_shape=jax.ShapeDtypeStruct((t, d), F32),
        compiler_params=pltpu.CompilerParams(dimension_semantics=("parallel",),
                                             vmem_limit_bytes=VMEM_LIMIT),
        name="final_norm",
    )(x2, gain)


def _moe(x2, gain, scale, shift, gate, wr_t, bias_col, w_gate, w_up, w_down, layer, seq, tm):
    t, d = x2.shape
    h, experts, weights, ranks, cnt = _route(x2, gain, scale, shift, wr_t, bias_col, seq, tm)
    counts = cnt[:, 0].astype(jnp.int32)
    padded = (counts + MOE_ROWS - 1) // MOE_ROWS * MOE_ROWS
    pad_end = jnp.cumsum(padded)
    pad_start = pad_end - padded
    onehot = experts[:, :, None] == jnp.arange(N_EXPERTS, dtype=jnp.int32)[None, None, :]
    dest = ranks + jnp.sum(jnp.where(onehot, pad_start[None, None, :], 0), axis=-1)
    n_blocks = 2 * t // MOE_ROWS + N_EXPERTS
    n_pad = n_blocks * MOE_ROWS
    tok = jnp.arange(t, dtype=jnp.int32)
    buf_tok = jnp.zeros((n_pad,), jnp.int32).at[dest.reshape(-1)].set(jnp.concatenate([tok, tok]))
    starts = jnp.arange(n_blocks, dtype=jnp.int32) * MOE_ROWS
    block_expert = jnp.minimum(jnp.sum((starts[:, None] >= pad_end[None, :]).astype(jnp.int32), axis=1),
                               N_EXPERTS - 1)
    block_valid = (starts < pad_end[-1]).astype(jnp.int32)
    xb = jnp.take(h, buf_tok, axis=0)
    yb = _experts(xb, w_gate, w_up, w_down, layer, block_expert, block_valid)
    y0 = jnp.take(yb, dest[0], axis=0)
    y1 = jnp.take(yb, dest[1], axis=0)
    return _combine(x2, y0, y1, weights.T, gate, seq, tm)


def kernel(x, c, norm_gain, w_ada, b_ada, w_in_even, w_in_odd, w_out, a_mu, a_w0, a_w_up, a_a0, a_a_up,
           a_g_up, a_k_k, a_k_a, a_r_k, a_ln_gain, a_ln_bias, b_alpha_up, b_alpha_bias, b_norm_gain, c_lb,
           c_norm_gain, d_conv_w, d_conv_b, d_dt_bias, d_a_log, d_skip, d_norm_gain, w_router, router_bias,
           w_gate, w_up, w_down, final_gain):
    bsz, seq, d = x.shape
    depth = w_ada.shape[0]
    t = bsz * seq
    tm = min(512, seq)
    cum_sel, pair_masks = _chunk_constants()
    ind512 = _head_indicator(MIX_HALF, HEAD_DIM)
    expand = jnp.asarray((np.arange(LANE)[:, None] == np.arange(MIX_HALF)[None, :] // HEAD_DIM).astype(np.float32))

    mods = _ada(c, w_ada, b_ada).reshape(depth, bsz, 6, 1, d)
    lb_p = jax.nn.softmax(c_lb.astype(F32), axis=0)
    lb_cum = jnp.cumsum(lb_p, axis=0)
    lower_bounds = lb_cum - lb_cum[0:1]
    wr_t = w_router.T
    bias_col = router_bias.reshape(N_EXPERTS, 1)

    x2 = x.reshape(t, d)
    for l in range(depth):
        j = l // 2
        sh_m, sc_m, g_m, sh_f, sc_f, g_f = [mods[l, :, i] for i in range(6)]
        gain_m, gain_f = norm_gain[l, 0].reshape(1, d), norm_gain[l, 1].reshape(1, d)
        if l % 2 == 0:
            w = w_in_even[j]
            zpad = jnp.zeros((d, LANE - GLA_GATE_RANK), w.dtype)
            w = jnp.concatenate([w[:, :RWKV_IN + 1024 + GLA_GATE_RANK], zpad,
                                 w[:, RWKV_IN + 1024 + GLA_GATE_RANK:]], axis=1).astype(BF16)
            u = _in_proj(x2, gain_m, sc_m, sh_m, w, seq, tm)
            alpha_up = jnp.concatenate([b_alpha_up[j], jnp.zeros((LANE - GLA_GATE_RANK, 256), F32)], axis=0)
            p = dict(mu=_row(a_mu[j]), w0=_row(a_w0[j]), w_up=a_w_up[j], a0=_row(a_a0[j]), a_up=a_a_up[j],
                     g_up=a_g_up[j], k_k=_row(a_k_k[j]), k_a=_row(a_k_a[j]), r_k=_row(a_r_k[j]),
                     ln_gain=_row(a_ln_gain[j]), ln_bias=_row(a_ln_bias[j]), alpha_up=alpha_up,
                     alpha_bias=_row(b_alpha_bias[j]), b_norm_gain=_row(b_norm_gain[j]))
            y = _mix_even(u.reshape(bsz, seq, EVEN_COLS), p, (cum_sel, pair_masks, ind512))
        else:
            w = w_in_odd[j]
            zpad = jnp.zeros((d, LANE - SSD_HEADS), w.dtype)
            w = jnp.concatenate([w, zpad], axis=1).astype(BF16)
            u = _in_proj(x2, gain_m, sc_m, sh_m, w, seq, tm)
            dt_bias = jnp.concatenate([d_dt_bias[j], jnp.zeros((LANE - SSD_HEADS,), F32)])
            p = dict(lb=_row(lower_bounds[l]), c_norm_gain=_row(c_norm_gain[j]), conv_w=d_conv_w[j],
                     conv_b=_row(d_conv_b[j]), dt_bias=_row(dt_bias),
                     a_log=_row(jnp.repeat(d_a_log[j], HEAD_DIM)), skip=_row(jnp.repeat(d_skip[j], HEAD_DIM)),
                     d_norm_gain=_row(d_norm_gain[j]))
            y = _mix_odd(u.reshape(bsz, seq, ODD_COLS), p, (cum_sel, pair_masks, expand))
        x2 = _out_proj(y.reshape(t, d), w_out[l].astype(BF16), x2, g_m, seq, tm)
        x2 = _moe(x2, gain_f, sc_f, sh_f, g_f, wr_t, bias_col, w_gate, w_up, w_down, l, seq, tm)
    return _final_norm(x2, final_gain.reshape(1, d), tm).reshape(bsz, seq, d)
```

```python
import functools

import numpy as np
import jax
import jax.numpy as jnp
from jax import lax
from jax.experimental import pallas as pl
from jax.experimental.pallas import tpu as pltpu

F32 = jnp.float32
BF16 = jnp.bfloat16
HIGHEST = lax.Precision.HIGHEST

D_MODEL = 1024
MIX_HALF = 512
HEAD_DIM = 64
CHUNK = 64
NORM_EPS = 1e-6
RWKV_HEADS = 8
RWKV_IN = 1792
RWKV_GN_EPS = 64e-5
RWKV_DECAY_SCALE = float(np.exp(-0.5))
GLA_HEADS = 4
GLA_GATE_RANK = 16
GLA_GATE_NORM = 16.0
SSD_HEADS = 8
SSD_GROUPS = 2
SSD_STATE = 64
SSD_CONV = 4
SSD_XBC = 768
N_EXPERTS = 16
N_EXPERT_GROUPS = 4
EXPERTS_PER_GROUP = 4
D_EXPERT = 512
LANE = 128
EVEN_COLS = 3456
ODD_COLS = 3328
MOE_ROWS = 256
VMEM_LIMIT = 48 * 1024 * 1024


def _dot(a, b):
    return jnp.dot(a.astype(BF16), b.astype(BF16), preferred_element_type=F32)


def _dot_nt(a, b):
    return lax.dot_general(a.astype(BF16), b.astype(BF16), (((1,), (1,)), ((), ())),
                           preferred_element_type=F32)


def _dot_tn(a, b):
    return lax.dot_general(a.astype(BF16), b.astype(BF16), (((0,), (0,)), ((), ())),
                           preferred_element_type=F32)


def _dot_exact(a, b):
    return jnp.dot(a, b, precision=HIGHEST, preferred_element_type=F32)


def _sigmoid(x):
    return 1.0 / (1.0 + jnp.exp(-x))


def _silu(x):
    return x * _sigmoid(x)


def _softplus(x):
    return jnp.maximum(x, 0.0) + jnp.log(1.0 + jnp.exp(-jnp.abs(x)))


def _log_sigmoid(x):
    return jnp.minimum(x, 0.0) - jnp.log(1.0 + jnp.exp(-jnp.abs(x)))


def _chunk_constants():
    t = np.arange(CHUNK)
    masks = []
    for shift in range(5, -1, -1):
        masks.append(((t[:, None] > t[None, :]) & (((t[:, None] ^ t[None, :]) >> shift) == 1)))
    masks.append(t[:, None] == t[None, :])
    return jnp.asarray(np.stack(masks).astype(np.float32))


def _head_indicator(width, seg):
    i = np.arange(width)
    return jnp.asarray((i[:, None] // seg == i[None, :] // seg).astype(np.float32), dtype=BF16)


def _ada_kernel(c_ref, w_ref, b_ref, o_ref):
    cond = _silu(c_ref[...])
    o_ref[0] = _dot_exact(cond, w_ref[0]) + b_ref[0]


def _ada(c, w_ada, b_ada):
    depth, d, n = w_ada.shape
    bsz = c.shape[0]
    tn = 1536
    return pl.pallas_call(
        _ada_kernel,
        grid=(depth, n // tn),
        in_specs=[pl.BlockSpec((bsz, d), lambda l, j: (0, 0)),
                  pl.BlockSpec((1, d, tn), lambda l, j: (l, 0, j)),
                  pl.BlockSpec((1, 1, tn), lambda l, j: (l, 0, j))],
        out_specs=pl.BlockSpec((1, bsz, tn), lambda l, j: (l, 0, j)),
        out_shape=jax.ShapeDtypeStruct((depth, bsz, n), F32),
        compiler_params=pltpu.CompilerParams(dimension_semantics=("parallel", "parallel"),
                                             vmem_limit_bytes=VMEM_LIMIT),
        name="ada",
    )(c, w_ada, b_ada.reshape(depth, 1, n))


def _modulated_norm(x, gain, scale, shift):
    ms = jnp.mean(x * x, axis=-1, keepdims=True)
    return (x * lax.rsqrt(ms + NORM_EPS)) * gain * (1.0 + scale) + shift


def _in_proj_kernel(x_ref, gain_ref, sc_ref, sh_ref, w_ref, o_ref, *, col_chunk):
    h = _modulated_norm(x_ref[...], gain_ref[...], sc_ref[0], sh_ref[0]).astype(BF16)
    n = o_ref.shape[1]
    for j in range(0, n, col_chunk):
        o_ref[:, j:j + col_chunk] = jnp.dot(h, w_ref[:, j:j + col_chunk],
                                            preferred_element_type=F32).astype(o_ref.dtype)


def _in_proj(x2, gain, scale, shift, w, seq, tm):
    t, d = x2.shape
    n = w.shape[1]
    col_chunk = next(n // parts for parts in (3, 2, 1) if n % (parts * LANE) == 0)
    per_b = seq // tm
    return pl.pallas_call(
        functools.partial(_in_proj_kernel, col_chunk=col_chunk),
        grid=(t // tm,),
        in_specs=[pl.BlockSpec((tm, d), lambda i: (i, 0)),
                  pl.BlockSpec((1, d), lambda i: (0, 0)),
                  pl.BlockSpec((1, 1, d), lambda i: (i // per_b, 0, 0)),
                  pl.BlockSpec((1, 1, d), lambda i: (i // per_b, 0, 0)),
                  pl.BlockSpec((d, n), lambda i: (0, 0))],
        out_specs=pl.BlockSpec((tm, n), lambda i: (i, 0)),
        out_shape=jax.ShapeDtypeStruct((t, n), BF16),
        compiler_params=pltpu.CompilerParams(dimension_semantics=("parallel",),
                                             vmem_limit_bytes=VMEM_LIMIT),
        name="in_proj",
    )(x2, gain, scale, shift, w)


def _cumsum_rows(x):
    rows = lax.broadcasted_iota(jnp.int32, x.shape, 0)
    step = 1
    while step < CHUNK:
        x = x + jnp.where(rows >= step, pltpu.roll(x, step, axis=0), 0.0)
        step *= 2
    return x


def _level_refs(b):
    cols = b.shape[1]
    rows = lax.broadcasted_iota(jnp.int32, b.shape, 0)

    def spread(offset, span):
        pieces = [jnp.broadcast_to(b[s + offset:s + offset + 1], (span, cols)) for s in range(0, CHUNK, span)]
        return pieces[0] if len(pieces) == 1 else jnp.concatenate(pieces, axis=0)

    refs = [spread(n // 2 - 1, n) for n in (64, 32, 16, 8)]
    refs.append(jnp.where((rows & 7) < 4, spread(1, 8), spread(5, 8)))
    refs.append(jnp.where((rows & 1) == 1, pltpu.roll(b, 1, axis=0), b))
    return refs


def _gla_chunk(q, k, v, g, st_ref, pair_masks, heads, dk, dv):
    b = _cumsum_rows(g)
    refs = _level_refs(b)
    b_last = b[CHUNK - 1:CHUNK]
    q_in = q * jnp.exp(b)
    k_st = k * jnp.exp(b_last - b)
    decay = jnp.exp(b_last)
    scores = [None] * heads
    for lvl in range(7):
        if lvl < 6:
            e = jnp.exp(-jnp.abs(b - refs[lvl]))
            qe, ke = q * e, k * e
        else:
            qe, ke = q, k
        keep = pair_masks[lvl] > 0.5
        for h in range(heads):
            p = _dot_nt(qe[:, h * dk:(h + 1) * dk], ke[:, h * dk:(h + 1) * dk])
            p = jnp.where(keep, p, 0.0)
            scores[h] = p if scores[h] is None else scores[h] + p
    hs = range(heads)
    ks = [slice(h * dk, (h + 1) * dk) for h in hs]
    vs = [slice(h * dv, (h + 1) * dv) for h in hs]
    st = [st_ref[h] for h in hs]
    o_inter = [_dot_nt(q_in[:, ks[h]], st[h]) for h in hs]
    o_intra = [_dot(scores[h], v[:, vs[h]]) for h in hs]
    st_new = [st[h] * decay[:, ks[h]] + _dot_tn(v[:, vs[h]], k_st[:, ks[h]]) for h in hs]
    for h in hs:
        st_ref[h] = st_new[h]
    return jnp.concatenate([o_inter[h] + o_intra[h] for h in hs], axis=-1)


def _head_rms(o, gain, heads, dv):
    outs = []
    for h in range(heads):
        oh = o[:, h * dv:(h + 1) * dv]
        ms = jnp.mean(oh * oh, axis=-1, keepdims=True)
        outs.append(oh * lax.rsqrt(ms + NORM_EPS) * gain)
    return jnp.concatenate(outs, axis=-1)


def _rwkv_chunk(r, kk, a, kt, v, logw, s_ref):
    b = _cumsum_rows(logw)
    b_last = b[CHUNK - 1:CHUNK]
    e_neg = jnp.exp(-b)
    e_end = jnp.exp(b_last - b)
    decay = jnp.exp(b_last)
    beta = a * kk
    k_bar = kk * jnp.exp(b - logw)
    r_bar = r * jnp.exp(b)
    beta_t, k_t = beta * e_neg, kt * e_neg
    beta_hat, k_hat = beta * e_end, kt * e_end
    row = lax.broadcasted_iota(jnp.int32, (CHUNK, CHUNK), 0)
    col = lax.broadcasted_iota(jnp.int32, (CHUNK, CHUNK), 1)
    strict = col < row
    incl = col <= row
    same_blk = (row >> 4) == (col >> 4)
    eye = (row == col).astype(F32)
    hs = range(RWKV_HEADS)
    sls = [slice(h * HEAD_DIM, (h + 1) * HEAD_DIM) for h in hs]
    kr = [jnp.concatenate([k_bar[:, sl], r_bar[:, sl]], axis=0) for sl in sls]
    bk = [jnp.concatenate([beta_t[:, sl], k_t[:, sl]], axis=0) for sl in sls]
    s0 = [s_ref[h] for h in hs]
    vh = [v[:, sl] for sl in sls]
    m1 = [_dot_nt(kr[h], bk[h]) for h in hs]
    m2 = [_dot_nt(kr[h], s0[h]) for h in hs]
    a_m = [jnp.where(strict, m[0:CHUNK, 0:CHUNK], 0.0) for m in m1]
    b_m = [jnp.where(strict, m[0:CHUNK, CHUNK:], 0.0) for m in m1]
    cb_m = [jnp.where(incl, m[CHUNK:, 0:CHUNK], 0.0) for m in m1]
    ck_m = [jnp.where(incl, m[CHUNK:, CHUNK:], 0.0) for m in m1]
    x1 = [jnp.where(same_blk, -am, 0.0) for am in a_m]
    a_off = [jnp.where(same_blk, 0.0, am) for am in a_m]
    rhs = [m2[h][0:CHUNK] + _dot(b_m[h], vh[h]) for h in hs]
    lo, hi = slice(0, CHUNK), slice(CHUNK, 2 * CHUNK)
    side = lambda left, right: jnp.concatenate([left, right], axis=1)
    x2 = [_dot(x, x) for x in x1]
    p = [eye + x for x in x1]
    w = [_dot(x2[h], side(p[h], x2[h])) for h in hs]
    p = [p[h] + w[h][:, lo] for h in hs]
    x4 = [w[h][:, hi] for h in hs]
    w = [_dot(x4[h], side(p[h], x4[h])) for h in hs]
    p = [p[h] + w[h][:, lo] for h in hs]
    t_d = [p[h] + _dot(w[h][:, hi], p[h]) for h in hs]
    nz = [_dot(t_d[h], side(a_off[h], rhs[h])) for h in hs]
    w = [_dot(nz[h][:, lo], nz[h]) for h in hs]
    y1 = [nz[h][:, hi] - w[h][:, hi] for h in hs]
    u = [y1[h] + _dot(w[h][:, lo], y1[h]) for h in hs]
    outs = [m2[h][CHUNK:] + _dot(side(ck_m[h], -cb_m[h]), jnp.concatenate([vh[h], u[h]], axis=0)) for h in hs]
    s_new = [s0[h] * decay[:, sls[h]]
             + _dot_tn(jnp.concatenate([vh[h], -u[h]], axis=0),
                       jnp.concatenate([k_hat[:, sls[h]], beta_hat[:, sls[h]]], axis=0)) for h in hs]
    for h in hs:
        s_ref[h] = s_new[h]
    return jnp.concatenate(outs, axis=-1)


def _head_sums(x, ind):
    return jnp.concatenate([_dot(x[:, i:i + LANE], ind) for i in range(0, x.shape[1], LANE)], axis=1)


def _mix_even_kernel(u_ref, mu_ref, w0_ref, wup_ref, a0_ref, aup_ref, gup_ref, kk_ref, ka_ref,
                     rk_ref, lng_ref, lnb_ref, alup_ref, albias_ref, bng_ref, ind_ref,
                     pmask_ref, y_ref, s_ref, g_ref, prev_ref):
    @pl.when(pl.program_id(1) == 0)
    def _():
        s_ref[...] = jnp.zeros_like(s_ref)
        g_ref[...] = jnp.zeros_like(g_ref)
        prev_ref[...] = jnp.zeros_like(prev_ref)

    u = u_ref[0].astype(F32)
    pair_masks = pmask_ref[...]
    ind = ind_ref[...]
    ua = u[:, 0:RWKV_IN]
    rows = lax.broadcasted_iota(jnp.int32, ua.shape, 0)
    shifted = jnp.where(rows == 0, prev_ref[...], pltpu.roll(ua, 1, axis=0))
    prev_ref[...] = ua[CHUNK - 1:CHUNK]
    xa = ua + mu_ref[...] * (shifted - ua)
    r, k, v = xa[:, 0:512], xa[:, 512:1024], xa[:, 1024:1536]
    wd, ad, gd = xa[:, 1536:1600], xa[:, 1600:1664], xa[:, 1664:1792]
    logw = -RWKV_DECAY_SCALE * _sigmoid(w0_ref[...] + _dot(jnp.tanh(wd), wup_ref[...]))
    a = _sigmoid(a0_ref[...] + _dot(ad, aup_ref[...]))
    gate = _dot(_sigmoid(gd), gup_ref[...])
    kk = k * kk_ref[...]
    kk = kk * lax.rsqrt(_head_sums(kk * kk, ind) + 1e-12)
    kt = k * (1.0 + (a - 1.0) * ka_ref[...])
    y = _rwkv_chunk(r, kk, a, kt, v, logw, s_ref)
    mean = _head_sums(y, ind) * (1.0 / HEAD_DIM)
    yc = y - mean
    var = _head_sums(yc * yc, ind) * (1.0 / HEAD_DIM)
    y = yc * lax.rsqrt(var + RWKV_GN_EPS) * lng_ref[...] + lnb_ref[...]
    y = y + _head_sums(r * kt * rk_ref[...], ind) * v
    y_ref[0, :, 0:MIX_HALF] = (y * gate).astype(y_ref.dtype)
    ub = u[:, RWKV_IN:]
    q, kg, vg = ub[:, 0:256] * (HEAD_DIM ** -0.5), ub[:, 256:512], ub[:, 512:1024]
    alpha, gg = ub[:, 1024:1152], ub[:, 1152:1664]
    log_a = _log_sigmoid(_dot(alpha, alup_ref[...]) + albias_ref[...]) * (1.0 / GLA_GATE_NORM)
    o = _gla_chunk(q, kg, vg, log_a, g_ref, pair_masks, GLA_HEADS, 64, 128)
    o = _head_rms(o, bng_ref[...], GLA_HEADS, 128)
    y_ref[0, :, MIX_HALF:] = (o * _silu(gg)).astype(y_ref.dtype)


def _row(p):
    return p.reshape(1, -1).astype(F32)


def _mix_even(u3, p, consts):
    bsz, seq, n = u3.shape
    pair_masks, ind = consts
    small = [p["mu"], p["w0"], p["w_up"], p["a0"], p["a_up"], p["g_up"], p["k_k"], p["k_a"], p["r_k"],
             p["ln_gain"], p["ln_bias"], p["alpha_up"], p["alpha_bias"], p["b_norm_gain"],
             ind, pair_masks]

    def full(arr):
        nd = arr.ndim
        return pl.BlockSpec(arr.shape, lambda b, s, _nd=nd: (0,) * _nd)

    return pl.pallas_call(
        _mix_even_kernel,
        grid=(bsz, seq // CHUNK),
        in_specs=[pl.BlockSpec((1, CHUNK, n), lambda b, s: (b, s, 0))] + [full(a) for a in small],
        out_specs=pl.BlockSpec((1, CHUNK, D_MODEL), lambda b, s: (b, s, 0)),
        out_shape=jax.ShapeDtypeStruct((bsz, seq, D_MODEL), BF16),
        scratch_shapes=[pltpu.VMEM((RWKV_HEADS, HEAD_DIM, HEAD_DIM), F32),
                        pltpu.VMEM((GLA_HEADS, 128, 64), F32),
                        pltpu.VMEM((1, RWKV_IN), F32)],
        compiler_params=pltpu.CompilerParams(dimension_semantics=("parallel", "arbitrary"),
                                             vmem_limit_bytes=VMEM_LIMIT),
        name="mix_even",
    )(u3, *small)


def _mix_odd_kernel(u_ref, lb_ref, cng_ref, convw_ref, convb_ref, dtb_ref, alog_ref, skip_ref,
                    dng_ref, pmask_ref, y_ref, h_ref, d_ref, tail_ref):
    @pl.when(pl.program_id(1) == 0)
    def _():
        h_ref[...] = jnp.zeros_like(h_ref)
        d_ref[...] = jnp.zeros_like(d_ref)
        tail_ref[...] = jnp.zeros_like(tail_ref)

    u = u_ref[0].astype(F32)
    pair_masks = pmask_ref[...]
    q, fr, iv, g = u[:, 0:256], u[:, 256:512], u[:, 512:1024], u[:, 1024:1536]
    lb = lb_ref[...]
    f = lb + (1.0 - lb) * _sigmoid(fr)
    o = _gla_chunk(q, 1.0 - f, iv, jnp.log(f), h_ref, pair_masks, 4, 64, 128)
    o = _head_rms(o, cng_ref[...], 4, 128)
    y_ref[0, :, 0:MIX_HALF] = (o * _silu(g)).astype(y_ref.dtype)
    z, xbc, dt_raw = u[:, 1536:2048], u[:, 2048:2816], u[:, 2816:3328]
    tail = tail_ref[...]
    tail_ref[...] = xbc[CHUNK - 8:CHUNK]
    rows8 = lax.broadcasted_iota(jnp.int32, (8, SSD_XBC), 0)
    conv = xbc * convw_ref[SSD_CONV - 1:SSD_CONV] + convb_ref[...]
    for back in range(1, SSD_CONV):
        rolled = pltpu.roll(xbc, back, axis=0)
        head8 = jnp.where(rows8 < back, pltpu.roll(tail, back, axis=0), rolled[0:8])
        shifted = jnp.concatenate([head8, rolled[8:]], axis=0)
        conv = conv + shifted * convw_ref[SSD_CONV - 1 - back:SSD_CONV - back]
    xbc = _silu(conv)
    xs, bmat, cmat = xbc[:, 0:512], xbc[:, 512:640], xbc[:, 640:768]
    dt = _softplus(dt_raw + dtb_ref[...])
    da = dt * (-jnp.exp(alog_ref[...]))
    cum = _cumsum_rows(da)
    cum_last = cum[CHUNK - 1:CHUNK]
    e_cum = jnp.exp(cum)
    xdt = xs * dt
    x_end = xdt * jnp.exp(cum_last - cum)
    decay = jnp.exp(cum_last)
    row = lax.broadcasted_iota(jnp.int32, (CHUNK, CHUNK), 0)
    col = lax.broadcasted_iota(jnp.int32, (CHUNK, CHUNK), 1)
    causal = col <= row
    groups = range(SSD_GROUPS)
    heads = range(SSD_HEADS)
    per_group = SSD_HEADS // SSD_GROUPS
    gs = [slice(grp * 256, (grp + 1) * 256) for grp in groups]
    ns = [slice(grp * SSD_STATE, (grp + 1) * SSD_STATE) for grp in groups]
    hs = [slice(h * HEAD_DIM, (h + 1) * HEAD_DIM) for h in heads]
    st = [d_ref[grp] for grp in groups]
    cb = [_dot_nt(cmat[:, ns[grp]], bmat[:, ns[grp]]) for grp in groups]
    y_off = [_dot(cmat[:, ns[grp]], st[grp]) for grp in groups]
    st_new = [st[grp] * decay[:, gs[grp]] + _dot_tn(bmat[:, ns[grp]], x_end[:, gs[grp]]) for grp in groups]
    cum_h = [cum[:, sl] for sl in hs]
    seg = [jnp.where(causal, jnp.exp(ch - ch.T), 0.0) for ch in cum_h]
    y_diag = [_dot(cb[h // per_group] * seg[h], xdt[:, hs[h]]) for h in heads]
    for grp in groups:
        d_ref[grp] = st_new[grp]
    y = (jnp.concatenate(y_diag, axis=-1) + jnp.concatenate(y_off, axis=-1) * e_cum + skip_ref[...] * xs)
    y = y * _silu(z)
    parts = []
    for grp in range(SSD_GROUPS):
        yg = y[:, grp * 256:(grp + 1) * 256]
        ms = jnp.mean(yg * yg, axis=-1, keepdims=True)
        parts.append(yg * lax.rsqrt(ms + NORM_EPS))
    y_ref[0, :, MIX_HALF:] = (jnp.concatenate(parts, axis=-1) * dng_ref[...]).astype(y_ref.dtype)


def _mix_odd(u3, p, consts):
    bsz, seq, n = u3.shape
    pair_masks = consts
    small = [p["lb"], p["c_norm_gain"], p["conv_w"], p["conv_b"], p["dt_bias"], p["a_log"], p["skip"],
             p["d_norm_gain"], pair_masks]

    def full(arr):
        nd = arr.ndim
        return pl.BlockSpec(arr.shape, lambda b, s, _nd=nd: (0,) * _nd)

    return pl.pallas_call(
        _mix_odd_kernel,
        grid=(bsz, seq // CHUNK),
        in_specs=[pl.BlockSpec((1, CHUNK, n), lambda b, s: (b, s, 0))] + [full(a) for a in small],
        out_specs=pl.BlockSpec((1, CHUNK, D_MODEL), lambda b, s: (b, s, 0)),
        out_shape=jax.ShapeDtypeStruct((bsz, seq, D_MODEL), BF16),
        scratch_shapes=[pltpu.VMEM((4, 128, 64), F32),
                        pltpu.VMEM((SSD_GROUPS, SSD_STATE, 256), F32),
                        pltpu.VMEM((8, SSD_XBC), F32)],
        compiler_params=pltpu.CompilerParams(dimension_semantics=("parallel", "arbitrary"),
                                             vmem_limit_bytes=VMEM_LIMIT),
        name="mix_odd",
    )(u3, *small)


def _out_proj_kernel(y_ref, w_ref, x_ref, g_ref, o_ref):
    o_ref[...] = x_ref[...] + g_ref[0] * jnp.dot(y_ref[...], w_ref[...], preferred_element_type=F32)


def _out_proj(y2, w, x2, gate, seq, tm):
    t, d = x2.shape
    per_b = seq // tm
    return pl.pallas_call(
        _out_proj_kernel,
        grid=(t // tm,),
        in_specs=[pl.BlockSpec((tm, d), lambda i: (i, 0)),
                  pl.BlockSpec((d, d), lambda i: (0, 0)),
                  pl.BlockSpec((tm, d), lambda i: (i, 0)),
                  pl.BlockSpec((1, 1, d), lambda i: (i // per_b, 0, 0))],
        out_specs=pl.BlockSpec((tm, d), lambda i: (i, 0)),
        out_shape=jax.ShapeDtypeStruct((t, d), F32),
        compiler_params=pltpu.CompilerParams(dimension_semantics=("parallel",),
                                             vmem_limit_bytes=VMEM_LIMIT),
        name="out_proj",
    )(y2, w, x2, gate)


def _route_kernel(x_ref, gain_ref, sc_ref, sh_ref, wr_ref, bias_ref, tri_ref, h_ref, e_ref, gt_ref, r_ref,
                  cnt_ref):
    h = _modulated_norm(x_ref[...], gain_ref[...], sc_ref[0], sh_ref[0])
    h_ref[...] = h.astype(h_ref.dtype)
    logits = lax.dot_general(wr_ref[...], h, (((1,), (1,)), ((), ())), precision=HIGHEST,
                             preferred_element_type=F32)
    score = _sigmoid(logits)
    sel = score + bias_ref[...]
    gscore = []
    for grp in range(N_EXPERT_GROUPS):
        a, b, c, d = [sel[grp * 4 + j:grp * 4 + j + 1] for j in range(4)]
        hi1, lo1, hi2, lo2 = jnp.maximum(a, b), jnp.minimum(a, b), jnp.maximum(c, d), jnp.minimum(c, d)
        gscore.append(jnp.maximum(hi1, hi2) + jnp.maximum(jnp.minimum(hi1, hi2), jnp.maximum(lo1, lo2)))
    best, gidx = gscore[0], jnp.zeros_like(gscore[0], dtype=jnp.int32)
    for grp in range(1, N_EXPERT_GROUPS):
        better = gscore[grp] > best
        gidx = jnp.where(better, grp, gidx)
        best = jnp.where(better, gscore[grp], best)
    vals, raw = [], []
    for j in range(EXPERTS_PER_GROUP):
        vj, rj = sel[j:j + 1], score[j:j + 1]
        for grp in range(1, N_EXPERT_GROUPS):
            vj = jnp.where(gidx == grp, sel[grp * 4 + j:grp * 4 + j + 1], vj)
            rj = jnp.where(gidx == grp, score[grp * 4 + j:grp * 4 + j + 1], rj)
        vals.append(vj)
        raw.append(rj)
    i1, m1, g1 = jnp.zeros_like(gidx), vals[0], raw[0]
    for j in range(1, EXPERTS_PER_GROUP):
        better = vals[j] > m1
        i1 = jnp.where(better, j, i1)
        m1 = jnp.where(better, vals[j], m1)
        g1 = jnp.where(better, raw[j], g1)
    i2, m2, g2 = jnp.zeros_like(gidx), jnp.full_like(m1, -jnp.inf), jnp.zeros_like(m1)
    for j in range(EXPERTS_PER_GROUP):
        better = jnp.logical_and(i1 != j, vals[j] > m2)
        i2 = jnp.where(better, j, i2)
        m2 = jnp.where(better, vals[j], m2)
        g2 = jnp.where(better, raw[j], g2)
    total = g1 + g2
    e1 = gidx * EXPERTS_PER_GROUP + i1
    e2 = gidx * EXPERTS_PER_GROUP + i2
    e_ref[0:1, :] = e1
    e_ref[1:2, :] = e2
    gt_ref[0:1, :] = g1 / total
    gt_ref[1:2, :] = g2 / total
    @pl.when(pl.program_id(0) == 0)
    def _():
        cnt_ref[...] = jnp.zeros_like(cnt_ref)

    eid = lax.broadcasted_iota(jnp.int32, logits.shape, 0)
    is1, is2 = eid == e1, eid == e2
    member = jnp.where(jnp.logical_or(is1, is2), 1.0, 0.0)
    before = jnp.dot(member.astype(BF16), tri_ref[...], preferred_element_type=F32) - member
    base = cnt_ref[:, 0:1] + before
    r_ref[0:1, :] = jnp.sum(jnp.where(is1, base, 0.0), axis=0, keepdims=True).astype(jnp.int32)
    r_ref[1:2, :] = jnp.sum(jnp.where(is2, base, 0.0), axis=0, keepdims=True).astype(jnp.int32)
    cnt_ref[...] = cnt_ref[...] + jnp.sum(member, axis=1, keepdims=True)


def _route(x2, gain, scale, shift, wr_t, bias_col, seq, tm):
    t, d = x2.shape
    per_b = seq // tm
    tri = jnp.asarray(np.triu(np.ones((tm, tm), np.float32)), dtype=BF16)
    return pl.pallas_call(
        _route_kernel,
        grid=(t // tm,),
        in_specs=[pl.BlockSpec((tm, d), lambda i: (i, 0)),
                  pl.BlockSpec((1, d), lambda i: (0, 0)),
                  pl.BlockSpec((1, 1, d), lambda i: (i // per_b, 0, 0)),
                  pl.BlockSpec((1, 1, d), lambda i: (i // per_b, 0, 0)),
                  pl.BlockSpec((N_EXPERTS, d), lambda i: (0, 0)),
                  pl.BlockSpec((N_EXPERTS, 1), lambda i: (0, 0)),
                  pl.BlockSpec((tm, tm), lambda i: (0, 0))],
        out_specs=[pl.BlockSpec((tm, d), lambda i: (i, 0)),
                   pl.BlockSpec((2, tm), lambda i: (0, i)),
                   pl.BlockSpec((2, tm), lambda i: (0, i)),
                   pl.BlockSpec((2, tm), lambda i: (0, i)),
                   pl.BlockSpec((N_EXPERTS, LANE), lambda i: (0, 0))],
        out_shape=[jax.ShapeDtypeStruct((t, d), BF16),
                   jax.ShapeDtypeStruct((2, t), jnp.int32),
                   jax.ShapeDtypeStruct((2, t), F32),
                   jax.ShapeDtypeStruct((2, t), jnp.int32),
                   jax.ShapeDtypeStruct((N_EXPERTS, LANE), F32)],
        compiler_params=pltpu.CompilerParams(dimension_semantics=("arbitrary",),
                                             vmem_limit_bytes=VMEM_LIMIT),
        name="route",
    )(x2, gain, scale, shift, wr_t, bias_col, tri)


def _expert_kernel(be_ref, bv_ref, x_ref, wg_ref, wu_ref, wd_ref, o_ref):
    i = pl.program_id(0)

    @pl.when(bv_ref[i] > 0)
    def _():
        x = x_ref[...]
        gate = jnp.dot(x, wg_ref[0, 0].astype(BF16), preferred_element_type=F32)
        up = jnp.dot(x, wu_ref[0, 0].astype(BF16), preferred_element_type=F32)
        act = (_silu(gate) * up).astype(BF16)
        y = jnp.dot(act, wd_ref[0, 0].astype(BF16), preferred_element_type=F32)
        o_ref[...] = y.astype(o_ref.dtype)

    @pl.when(bv_ref[i] == 0)
    def _():
        o_ref[...] = jnp.zeros_like(o_ref)


def _experts(xb, w_gate, w_up, w_down, layer, block_expert, block_valid):
    n_pad, d = xb.shape
    n_blocks = n_pad // MOE_ROWS
    de = w_gate.shape[-1]
    grid_spec = pltpu.PrefetchScalarGridSpec(
        num_scalar_prefetch=2,
        grid=(n_blocks,),
        in_specs=[pl.BlockSpec((MOE_ROWS, d), lambda i, be, bv: (i, 0)),
                  pl.BlockSpec((1, 1, d, de), lambda i, be, bv: (layer, be[i], 0, 0)),
                  pl.BlockSpec((1, 1, d, de), lambda i, be, bv: (layer, be[i], 0, 0)),
                  pl.BlockSpec((1, 1, de, d), lambda i, be, bv: (layer, be[i], 0, 0))],
        out_specs=pl.BlockSpec((MOE_ROWS, d), lambda i, be, bv: (i, 0)),
    )
    return pl.pallas_call(
        _expert_kernel,
        grid_spec=grid_spec,
        out_shape=jax.ShapeDtypeStruct((n_pad, d), BF16),
        compiler_params=pltpu.CompilerParams(dimension_semantics=("arbitrary",),
                                             vmem_limit_bytes=VMEM_LIMIT),
        name="experts",
    )(block_expert, block_valid, xb, w_gate, w_up, w_down)


def _combine_kernel(x_ref, y0_ref, y1_ref, w_ref, g_ref, o_ref):
    w = w_ref[...]
    moe = y0_ref[...].astype(F32) * w[:, 0:1] + y1_ref[...].astype(F32) * w[:, 1:2]
    o_ref[...] = x_ref[...] + g_ref[0] * moe


def _combine(x2, y0, y1, weights, gate, seq, tm):
    t, d = x2.shape
    per_b = seq // tm
    return pl.pallas_call(
        _combine_kernel,
        grid=(t // tm,),
        in_specs=[pl.BlockSpec((tm, d), lambda i: (i, 0)),
                  pl.BlockSpec((tm, d), lambda i: (i, 0)),
                  pl.BlockSpec((tm, d), lambda i: (i, 0)),
                  pl.BlockSpec((tm, 2), lambda i: (i, 0)),
                  pl.BlockSpec((1, 1, d), lambda i: (i // per_b, 0, 0))],
        out_specs=pl.BlockSpec((tm, d), lambda i: (i, 0)),
        out_shape=jax.ShapeDtypeStruct((t, d), F32),
        compiler_params=pltpu.CompilerParams(dimension_semantics=("parallel",),
                                             vmem_limit_bytes=VMEM_LIMIT),
        name="combine",
    )(x2, y0, y1, weights, gate)


def _final_norm_kernel(x_ref, gain_ref, o_ref):
    x = x_ref[...]
    ms = jnp.mean(x * x, axis=-1, keepdims=True)
    o_ref[...] = x * lax.rsqrt(ms + NORM_EPS) * gain_ref[...]


def _final_norm(x2, gain, tm):
    t, d = x2.shape
    return pl.pallas_call(
        _final_norm_kernel,
        grid=(t // tm,),
        in_specs=[pl.BlockSpec((tm, d), lambda i: (i, 0)), pl.BlockSpec((1, d), lambda i: (0, 0))],
        out_specs=pl.BlockSpec((tm, d), lambda i: (i, 0)),
        out_shape=jax.ShapeDtypeStruct((t, d), F32),
        compiler_params=pltpu.CompilerParams(dimension_semantics=("parallel",),
                                             vmem_limit_bytes=VMEM_LIMIT),
        name="final_norm",
    )(x2, gain)


def _moe(x2, gain, scale, shift, gate, wr_t, bias_col, w_gate, w_up, w_down, layer, seq, tm):
    t, d = x2.shape
    h, experts, weights, ranks, cnt = _route(x2, gain, scale, shift, wr_t, bias_col, seq, tm)
    counts = cnt[:, 0].astype(jnp.int32)
    padded = (counts + MOE_ROWS - 1) // MOE_ROWS * MOE_ROWS
    pad_end = jnp.cumsum(padded)
    pad_start = pad_end - padded
    onehot = experts[:, :, None] == jnp.arange(N_EXPERTS, dtype=jnp.int32)[None, None, :]
    dest = ranks + jnp.sum(jnp.where(onehot, pad_start[None, None, :], 0), axis=-1)
    n_blocks = 2 * t // MOE_ROWS + N_EXPERTS
    n_pad = n_blocks * MOE_ROWS
    tok = jnp.arange(t, dtype=jnp.int32)
    buf_tok = jnp.zeros((n_pad,), jnp.int32).at[dest.reshape(-1)].set(jnp.concatenate([tok, tok]))
    starts = jnp.arange(n_blocks, dtype=jnp.int32) * MOE_ROWS
    block_expert = jnp.minimum(jnp.sum((starts[:, None] >= pad_end[None, :]).astype(jnp.int32), axis=1),
                               N_EXPERTS - 1)
    block_valid = (starts < pad_end[-1]).astype(jnp.int32)
    xb = jnp.take(h, buf_tok, axis=0)
    yb = _experts(xb, w_gate, w_up, w_down, layer, block_expert, block_valid)
    y0 = jnp.take(yb, dest[0], axis=0)
    y1 = jnp.take(yb, dest[1], axis=0)
    return _combine(x2, y0, y1, weights.T, gate, seq, tm)


def kernel(x, c, norm_gain, w_ada, b_ada, w_in_even, w_in_odd, w_out, a_mu, a_w0, a_w_up, a_a0, a_a_up,
           a_g_up, a_k_k, a_k_a, a_r_k, a_ln_gain, a_ln_bias, b_alpha_up, b_alpha_bias, b_norm_gain, c_lb,
           c_norm_gain, d_conv_w, d_conv_b, d_dt_bias, d_a_log, d_skip, d_norm_gain, w_router, router_bias,
           w_gate, w_up, w_down, final_gain):
    bsz, seq, d = x.shape
    depth = w_ada.shape[0]
    t = bsz * seq
    tm = min(512, seq)
    pair_masks = _chunk_constants()
    ind = _head_indicator(LANE, HEAD_DIM)

    mods = _ada(c, w_ada, b_ada).reshape(depth, bsz, 6, 1, d)
    lb_p = jax.nn.softmax(c_lb.astype(F32), axis=0)
    lb_cum = jnp.cumsum(lb_p, axis=0)
    lower_bounds = lb_cum - lb_cum[0:1]
    wr_t = w_router.T
    bias_col = router_bias.reshape(N_EXPERTS, 1)

    x2 = x.reshape(t, d)
    for l in range(depth):
        j = l // 2
        sh_m, sc_m, g_m, sh_f, sc_f, g_f = [mods[l, :, i] for i in range(6)]
        gain_m, gain_f = norm_gain[l, 0].reshape(1, d), norm_gain[l, 1].reshape(1, d)
        if l % 2 == 0:
            w = w_in_even[j]
            zpad = jnp.zeros((d, LANE - GLA_GATE_RANK), w.dtype)
            w = jnp.concatenate([w[:, :RWKV_IN + 1024 + GLA_GATE_RANK], zpad,
                                 w[:, RWKV_IN + 1024 + GLA_GATE_RANK:]], axis=1).astype(BF16)
            u = _in_proj(x2, gain_m, sc_m, sh_m, w, seq, tm)
            alpha_up = jnp.concatenate([b_alpha_up[j], jnp.zeros((LANE - GLA_GATE_RANK, 256), F32)], axis=0)
            p = dict(mu=_row(a_mu[j]), w0=_row(a_w0[j]), w_up=a_w_up[j], a0=_row(a_a0[j]), a_up=a_a_up[j],
                     g_up=a_g_up[j], k_k=_row(a_k_k[j]), k_a=_row(a_k_a[j]), r_k=_row(a_r_k[j]),
                     ln_gain=_row(a_ln_gain[j]), ln_bias=_row(a_ln_bias[j]), alpha_up=alpha_up,
                     alpha_bias=_row(b_alpha_bias[j]), b_norm_gain=_row(b_norm_gain[j]))
            y = _mix_even(u.reshape(bsz, seq, EVEN_COLS), p, (pair_masks, ind))
        else:
            w = w_in_odd[j]
            n_main = w.shape[1] - SSD_HEADS
            w = jnp.concatenate([w[:, :n_main], jnp.repeat(w[:, n_main:], HEAD_DIM, axis=1)], axis=1).astype(BF16)
            u = _in_proj(x2, gain_m, sc_m, sh_m, w, seq, tm)
            p = dict(lb=_row(lower_bounds[l]), c_norm_gain=_row(c_norm_gain[j]), conv_w=d_conv_w[j],
                     conv_b=_row(d_conv_b[j]), dt_bias=_row(jnp.repeat(d_dt_bias[j], HEAD_DIM)),
                     a_log=_row(jnp.repeat(d_a_log[j], HEAD_DIM)), skip=_row(jnp.repeat(d_skip[j], HEAD_DIM)),
                     d_norm_gain=_row(d_norm_gain[j]))
            y = _mix_odd(u.reshape(bsz, seq, ODD_COLS), p, pair_masks)
        x2 = _out_proj(y.reshape(t, d), w_out[l].astype(BF16), x2, g_m, seq, tm)
        x2 = _moe(x2, gain_f, sc_f, sh_f, g_f, wr_t, bias_col, w_gate, w_up, w_down, l, seq, tm)
    return _final_norm(x2, final_gain.reshape(1, d), tm).reshape(bsz, seq, d)
```

```python
import functools

import numpy as np
import jax
import jax.numpy as jnp
from jax import lax
from jax.experimental import pallas as pl
from jax.experimental.pallas import tpu as pltpu

F32 = jnp.float32
BF16 = jnp.bfloat16
HIGHEST = lax.Precision.HIGHEST

D_MODEL = 1024
MIX_HALF = 512
HEAD_DIM = 64
CHUNK = 64
NORM_EPS = 1e-6
RWKV_HEADS = 8
RWKV_IN = 1792
RWKV_GN_EPS = 64e-5
RWKV_DECAY_SCALE = float(np.exp(-0.5))
GLA_HEADS = 4
GLA_GATE_RANK = 16
GLA_GATE_NORM = 16.0
SSD_HEADS = 8
SSD_GROUPS = 2
SSD_STATE = 64
SSD_CONV = 4
SSD_XBC = 768
N_EXPERTS = 16
N_EXPERT_GROUPS = 4
EXPERTS_PER_GROUP = 4
D_EXPERT = 512
LANE = 128
EVEN_COLS = 3456
ODD_COLS = 3328
MOE_ROWS = 256
MIX_TILE = 128
VMEM_LIMIT = 48 * 1024 * 1024


def _dot(a, b):
    return jnp.dot(a.astype(BF16), b.astype(BF16), preferred_element_type=F32)


def _dot_nt(a, b):
    return lax.dot_general(a.astype(BF16), b.astype(BF16), (((1,), (1,)), ((), ())),
                           preferred_element_type=F32)


def _dot_tn(a, b):
    return lax.dot_general(a.astype(BF16), b.astype(BF16), (((0,), (0,)), ((), ())),
                           preferred_element_type=F32)


def _dot_exact(a, b):
    return jnp.dot(a, b, precision=HIGHEST, preferred_element_type=F32)


def _sigmoid(x):
    return 1.0 / (1.0 + jnp.exp(-x))


def _silu(x):
    return x * _sigmoid(x)


def _softplus(x):
    return jnp.maximum(x, 0.0) + jnp.log(1.0 + jnp.exp(-jnp.abs(x)))


def _log_sigmoid(x):
    return jnp.minimum(x, 0.0) - jnp.log(1.0 + jnp.exp(-jnp.abs(x)))


def _chunk_constants():
    t = np.arange(CHUNK)
    masks = []
    for shift in range(5, -1, -1):
        masks.append(((t[:, None] > t[None, :]) & (((t[:, None] ^ t[None, :]) >> shift) == 1)))
    masks.append(t[:, None] == t[None, :])
    return jnp.asarray(np.stack(masks).astype(np.float32))


def _head_indicator(width, seg):
    i = np.arange(width)
    return jnp.asarray((i[:, None] // seg == i[None, :] // seg).astype(np.float32), dtype=BF16)


def _ada_kernel(c_ref, w_ref, b_ref, o_ref):
    cond = _silu(c_ref[...])
    o_ref[0] = _dot_exact(cond, w_ref[0]) + b_ref[0]


def _ada(c, w_ada, b_ada):
    depth, d, n = w_ada.shape
    bsz = c.shape[0]
    tn = 1536
    return pl.pallas_call(
        _ada_kernel,
        grid=(depth, n // tn),
        in_specs=[pl.BlockSpec((bsz, d), lambda l, j: (0, 0)),
                  pl.BlockSpec((1, d, tn), lambda l, j: (l, 0, j)),
                  pl.BlockSpec((1, 1, tn), lambda l, j: (l, 0, j))],
        out_specs=pl.BlockSpec((1, bsz, tn), lambda l, j: (l, 0, j)),
        out_shape=jax.ShapeDtypeStruct((depth, bsz, n), F32),
        compiler_params=pltpu.CompilerParams(dimension_semantics=("parallel", "parallel"),
                                             vmem_limit_bytes=VMEM_LIMIT),
        name="ada",
    )(c, w_ada, b_ada.reshape(depth, 1, n))


def _modulated_norm(x, gain, scale, shift):
    ms = jnp.mean(x * x, axis=-1, keepdims=True)
    return (x * lax.rsqrt(ms + NORM_EPS)) * gain * (1.0 + scale) + shift


def _in_proj_kernel(x_ref, gain_ref, sc_ref, sh_ref, w_ref, o_ref, *, col_chunk):
    h = _modulated_norm(x_ref[...], gain_ref[...], sc_ref[0], sh_ref[0]).astype(BF16)
    n = o_ref.shape[1]
    for j in range(0, n, col_chunk):
        o_ref[:, j:j + col_chunk] = jnp.dot(h, w_ref[:, j:j + col_chunk],
                                            preferred_element_type=F32).astype(o_ref.dtype)


def _in_proj(x2, gain, scale, shift, w, seq, tm):
    t, d = x2.shape
    n = w.shape[1]
    col_chunk = next(n // parts for parts in (3, 2, 1) if n % (parts * LANE) == 0)
    per_b = seq // tm
    return pl.pallas_call(
        functools.partial(_in_proj_kernel, col_chunk=col_chunk),
        grid=(t // tm,),
        in_specs=[pl.BlockSpec((tm, d), lambda i: (i, 0)),
                  pl.BlockSpec((1, d), lambda i: (0, 0)),
                  pl.BlockSpec((1, 1, d), lambda i: (i // per_b, 0, 0)),
                  pl.BlockSpec((1, 1, d), lambda i: (i // per_b, 0, 0)),
                  pl.BlockSpec((d, n), lambda i: (0, 0))],
        out_specs=pl.BlockSpec((tm, n), lambda i: (i, 0)),
        out_shape=jax.ShapeDtypeStruct((t, n), BF16),
        compiler_params=pltpu.CompilerParams(dimension_semantics=("parallel",),
                                             vmem_limit_bytes=VMEM_LIMIT),
        name="in_proj",
    )(x2, gain, scale, shift, w)


def _cumsum_rows(x):
    rows = lax.broadcasted_iota(jnp.int32, x.shape, 0)
    step = 1
    while step < CHUNK:
        x = x + jnp.where(rows >= step, pltpu.roll(x, step, axis=0), 0.0)
        step *= 2
    return x


def _level_refs(b):
    cols = b.shape[1]
    rows = lax.broadcasted_iota(jnp.int32, b.shape, 0)

    def spread(offset, span):
        pieces = [jnp.broadcast_to(b[s + offset:s + offset + 1], (span, cols)) for s in range(0, CHUNK, span)]
        return pieces[0] if len(pieces) == 1 else jnp.concatenate(pieces, axis=0)

    refs = [spread(n // 2 - 1, n) for n in (64, 32, 16, 8)]
    refs.append(jnp.where((rows & 7) < 4, spread(1, 8), spread(5, 8)))
    refs.append(jnp.where((rows & 1) == 1, pltpu.roll(b, 1, axis=0), b))
    return refs


def _gla_chunk(q, k, v, g, st, pair_masks, heads, dk, dv):
    b = _cumsum_rows(g)
    refs = _level_refs(b)
    b_last = b[CHUNK - 1:CHUNK]
    q_in = q * jnp.exp(b)
    k_st = k * jnp.exp(b_last - b)
    decay = jnp.exp(b_last)
    yield
    scores = [None] * heads
    for lvl in range(7):
        if lvl < 6:
            e = jnp.exp(-jnp.abs(b - refs[lvl]))
            qe, ke = q * e, k * e
        else:
            qe, ke = q, k
        keep = pair_masks[lvl] > 0.5
        for h in range(heads):
            p = _dot_nt(qe[:, h * dk:(h + 1) * dk], ke[:, h * dk:(h + 1) * dk])
            p = jnp.where(keep, p, 0.0)
            scores[h] = p if scores[h] is None else scores[h] + p
        yield
    hs = range(heads)
    ks = [slice(h * dk, (h + 1) * dk) for h in hs]
    vs = [slice(h * dv, (h + 1) * dv) for h in hs]
    o_inter = [_dot_nt(q_in[:, ks[h]], st[h]) for h in hs]
    o_intra = [_dot(scores[h], v[:, vs[h]]) for h in hs]
    yield
    st_new = [st[h] * decay[:, ks[h]] + _dot_tn(v[:, vs[h]], k_st[:, ks[h]]) for h in hs]
    return jnp.concatenate([o_inter[h] + o_intra[h] for h in hs], axis=-1), st_new


def _interleave(*stages):
    results = [None] * len(stages)
    live = list(range(len(stages)))
    while live:
        for i in list(live):
            try:
                next(stages[i])
            except StopIteration as stop:
                results[i] = stop.value
                live.remove(i)
    return results


def _head_rms(o, gain, heads, dv):
    outs = []
    for h in range(heads):
        oh = o[:, h * dv:(h + 1) * dv]
        ms = jnp.mean(oh * oh, axis=-1, keepdims=True)
        outs.append(oh * lax.rsqrt(ms + NORM_EPS) * gain)
    return jnp.concatenate(outs, axis=-1)


def _rwkv_chunk(r, kk, a, kt, v, logw, s0):
    b = _cumsum_rows(logw)
    b_last = b[CHUNK - 1:CHUNK]
    e_neg = jnp.exp(-b)
    e_end = jnp.exp(b_last - b)
    decay = jnp.exp(b_last)
    beta = a * kk
    k_bar = kk * jnp.exp(b - logw)
    r_bar = r * jnp.exp(b)
    beta_t, k_t = beta * e_neg, kt * e_neg
    beta_hat, k_hat = beta * e_end, kt * e_end
    row = lax.broadcasted_iota(jnp.int32, (CHUNK, CHUNK), 0)
    col = lax.broadcasted_iota(jnp.int32, (CHUNK, CHUNK), 1)
    strict = col < row
    incl = col <= row
    same_blk = (row >> 4) == (col >> 4)
    eye = (row == col).astype(F32)
    hs = range(RWKV_HEADS)
    sls = [slice(h * HEAD_DIM, (h + 1) * HEAD_DIM) for h in hs]
    kr = [jnp.concatenate([k_bar[:, sl], r_bar[:, sl]], axis=0) for sl in sls]
    bk = [jnp.concatenate([beta_t[:, sl], k_t[:, sl]], axis=0) for sl in sls]
    vh = [v[:, sl] for sl in sls]
    yield
    m1 = [_dot_nt(kr[h], bk[h]) for h in hs]
    yield
    m2 = [_dot_nt(kr[h], s0[h]) for h in hs]
    yield
    a_m = [jnp.where(strict, m[0:CHUNK, 0:CHUNK], 0.0) for m in m1]
    b_m = [jnp.where(strict, m[0:CHUNK, CHUNK:], 0.0) for m in m1]
    cb_m = [jnp.where(incl, m[CHUNK:, 0:CHUNK], 0.0) for m in m1]
    ck_m = [jnp.where(incl, m[CHUNK:, CHUNK:], 0.0) for m in m1]
    x1 = [jnp.where(same_blk, -am, 0.0) for am in a_m]
    a_off = [jnp.where(same_blk, 0.0, am) for am in a_m]
    rhs = [m2[h][0:CHUNK] + _dot(b_m[h], vh[h]) for h in hs]
    lo, hi = slice(0, CHUNK), slice(CHUNK, 2 * CHUNK)
    side = lambda left, right: jnp.concatenate([left, right], axis=1)
    x2 = [_dot(x, x) for x in x1]
    yield
    p = [eye + x for x in x1]
    w = [_dot(x2[h], side(p[h], x2[h])) for h in hs]
    yield
    p = [p[h] + w[h][:, lo] for h in hs]
    x4 = [w[h][:, hi] for h in hs]
    w = [_dot(x4[h], side(p[h], x4[h])) for h in hs]
    yield
    p = [p[h] + w[h][:, lo] for h in hs]
    t_d = [p[h] + _dot(w[h][:, hi], p[h]) for h in hs]
    yield
    nz = [_dot(t_d[h], side(a_off[h], rhs[h])) for h in hs]
    yield
    w = [_dot(nz[h][:, lo], nz[h]) for h in hs]
    yield
    y1 = [nz[h][:, hi] - w[h][:, hi] for h in hs]
    u = [y1[h] + _dot(w[h][:, lo], y1[h]) for h in hs]
    yield
    outs = [m2[h][CHUNK:] + _dot(side(ck_m[h], -cb_m[h]), jnp.concatenate([vh[h], u[h]], axis=0)) for h in hs]
    yield
    s_new = [s0[h] * decay[:, sls[h]]
             + _dot_tn(jnp.concatenate([vh[h], -u[h]], axis=0),
                       jnp.concatenate([k_hat[:, sls[h]], beta_hat[:, sls[h]]], axis=0)) for h in hs]
    return jnp.concatenate(outs, axis=-1), s_new


def _head_sums(x, ind):
    return jnp.concatenate([_dot(x[:, i:i + LANE], ind) for i in range(0, x.shape[1], LANE)], axis=1)


def _mix_even_kernel(u_ref, mu_ref, w0_ref, wup_ref, a0_ref, aup_ref, gup_ref, kk_ref, ka_ref,
                     rk_ref, lng_ref, lnb_ref, alup_ref, albias_ref, bng_ref, ind_ref,
                     pmask_ref, y_ref, s_ref, g_ref, prev_ref):
    @pl.when(pl.program_id(1) == 0)
    def _():
        s_ref[...] = jnp.zeros_like(s_ref)
        g_ref[...] = jnp.zeros_like(g_ref)
        prev_ref[...] = jnp.zeros_like(prev_ref)

    pair_masks = pmask_ref[...]
    ind = ind_ref[...]
    s_state = [s_ref[h] for h in range(RWKV_HEADS)]
    g_state = [g_ref[h] for h in range(GLA_HEADS)]
    prev = prev_ref[...]
    def rwkv_stages(rs, prev, s_state):
        ua = u_ref[0, rs, 0:RWKV_IN].astype(F32)
        rows = lax.broadcasted_iota(jnp.int32, ua.shape, 0)
        shifted = jnp.where(rows == 0, prev, pltpu.roll(ua, 1, axis=0))
        xa = ua + mu_ref[...] * (shifted - ua)
        r, k, v = xa[:, 0:512], xa[:, 512:1024], xa[:, 1024:1536]
        wd, ad, gd = xa[:, 1536:1600], xa[:, 1600:1664], xa[:, 1664:1792]
        logw = -RWKV_DECAY_SCALE * _sigmoid(w0_ref[...] + _dot(jnp.tanh(wd), wup_ref[...]))
        a = _sigmoid(a0_ref[...] + _dot(ad, aup_ref[...]))
        gate = _dot(_sigmoid(gd), gup_ref[...])
        yield
        kk = k * kk_ref[...]
        kk = kk * lax.rsqrt(_head_sums(kk * kk, ind) + 1e-12)
        kt = k * (1.0 + (a - 1.0) * ka_ref[...])
        yield
        y, s_state = yield from _rwkv_chunk(r, kk, a, kt, v, logw, s_state)
        yield
        mean = _head_sums(y, ind) * (1.0 / HEAD_DIM)
        yc = y - mean
        yield
        var = _head_sums(yc * yc, ind) * (1.0 / HEAD_DIM)
        y = yc * lax.rsqrt(var + RWKV_GN_EPS) * lng_ref[...] + lnb_ref[...]
        y = y + _head_sums(r * kt * rk_ref[...], ind) * v
        y_ref[0, rs, 0:MIX_HALF] = (y * gate).astype(y_ref.dtype)
        return ua[CHUNK - 1:CHUNK], s_state

    def gla_stages(rs, g_state):
        ub = u_ref[0, rs, RWKV_IN:].astype(F32)
        q, kg, vg = ub[:, 0:256] * (HEAD_DIM ** -0.5), ub[:, 256:512], ub[:, 512:1024]
        alpha, gg = ub[:, 1024:1152], ub[:, 1152:1664]
        log_a = _log_sigmoid(_dot(alpha, alup_ref[...]) + albias_ref[...]) * (1.0 / GLA_GATE_NORM)
        yield
        o, g_state = yield from _gla_chunk(q, kg, vg, log_a, g_state, pair_masks, GLA_HEADS, 64, 128)
        o = _head_rms(o, bng_ref[...], GLA_HEADS, 128)
        y_ref[0, rs, MIX_HALF:] = (o * _silu(gg)).astype(y_ref.dtype)
        return g_state

    for c in range(u_ref.shape[1] // CHUNK):
        rs = slice(c * CHUNK, (c + 1) * CHUNK)
        (prev, s_state), g_state = _interleave(rwkv_stages(rs, prev, s_state), gla_stages(rs, g_state))
    for h in range(RWKV_HEADS):
        s_ref[h] = s_state[h]
    for h in range(GLA_HEADS):
        g_ref[h] = g_state[h]
    prev_ref[...] = prev


def _row(p):
    return p.reshape(1, -1).astype(F32)


def _mix_even(u3, p, consts):
    bsz, seq, n = u3.shape
    pair_masks, ind = consts
    small = [p["mu"], p["w0"], p["w_up"], p["a0"], p["a_up"], p["g_up"], p["k_k"], p["k_a"], p["r_k"],
             p["ln_gain"], p["ln_bias"], p["alpha_up"], p["alpha_bias"], p["b_norm_gain"],
             ind, pair_masks]

    def full(arr):
        nd = arr.ndim
        return pl.BlockSpec(arr.shape, lambda b, s, _nd=nd: (0,) * _nd)

    return pl.pallas_call(
        _mix_even_kernel,
        grid=(bsz, seq // MIX_TILE),
        in_specs=[pl.BlockSpec((1, MIX_TILE, n), lambda b, s: (b, s, 0))] + [full(a) for a in small],
        out_specs=pl.BlockSpec((1, MIX_TILE, D_MODEL), lambda b, s: (b, s, 0)),
        out_shape=jax.ShapeDtypeStruct((bsz, seq, D_MODEL), BF16),
        scratch_shapes=[pltpu.VMEM((RWKV_HEADS, HEAD_DIM, HEAD_DIM), F32),
                        pltpu.VMEM((GLA_HEADS, 128, 64), F32),
                        pltpu.VMEM((1, RWKV_IN), F32)],
        compiler_params=pltpu.CompilerParams(dimension_semantics=("parallel", "arbitrary"),
                                             vmem_limit_bytes=VMEM_LIMIT),
        name="mix_even",
    )(u3, *small)


def _mix_odd_kernel(u_ref, clb_ref, cng_ref, convw_ref, convb_ref, dtb_ref, alog_ref, skip_ref,
                    dng_ref, pmask_ref, y_ref, h_ref, d_ref, tail_ref, *, layer):
    @pl.when(pl.program_id(1) == 0)
    def _():
        h_ref[...] = jnp.zeros_like(h_ref)
        d_ref[...] = jnp.zeros_like(d_ref)
        tail_ref[...] = jnp.zeros_like(tail_ref)

    pair_masks = pmask_ref[...]
    c_lb = clb_ref[...]
    c_exp = jnp.exp(c_lb - jnp.max(c_lb, axis=0, keepdims=True))
    lb = jnp.sum(c_exp[1:layer + 1], axis=0, keepdims=True) / jnp.sum(c_exp, axis=0, keepdims=True)
    h_state = [h_ref[h] for h in range(4)]
    d_state = [d_ref[grp] for grp in range(SSD_GROUPS)]
    tail = tail_ref[...]
    def hgrn_stages(rs, h_state):
        u = u_ref[0, rs, 0:1536].astype(F32)
        q, fr, iv, g = u[:, 0:256], u[:, 256:512], u[:, 512:1024], u[:, 1024:1536]
        f = lb + (1.0 - lb) * _sigmoid(fr)
        yield
        o, h_state = yield from _gla_chunk(q, 1.0 - f, iv, jnp.log(f), h_state, pair_masks, 4, 64, 128)
        y_ref[0, rs, 0:MIX_HALF] = (_head_rms(o, cng_ref[...], 4, 128) * _silu(g)).astype(y_ref.dtype)
        return h_state

    def ssd_stages(rs, tail, d_state):
        u = u_ref[0, rs, 1536:].astype(F32)
        y, d_state, tail = yield from _ssd_chunk(u, tail, d_state, convw_ref, convb_ref[...], dtb_ref[...],
                                                 alog_ref[...], skip_ref[...], dng_ref[...])
        y_ref[0, rs, MIX_HALF:] = y.astype(y_ref.dtype)
        return tail, d_state

    for c in range(u_ref.shape[1] // CHUNK):
        rs = slice(c * CHUNK, (c + 1) * CHUNK)
        (h_state,) = _interleave(hgrn_stages(rs, h_state))
        ((tail, d_state),) = _interleave(ssd_stages(rs, tail, d_state))
    for h in range(4):
        h_ref[h] = h_state[h]
    for grp in range(SSD_GROUPS):
        d_ref[grp] = d_state[grp]
    tail_ref[...] = tail


def _ssd_chunk(u, tail, st, convw_ref, conv_b, dt_bias, a_log, skip, norm_gain):
    z, xbc, dt_raw = u[:, 0:512], u[:, 512:1280], u[:, 1280:1792]
    new_tail = xbc[CHUNK - 8:CHUNK]
    rows8 = lax.broadcasted_iota(jnp.int32, (8, SSD_XBC), 0)
    conv = xbc * convw_ref[SSD_CONV - 1:SSD_CONV] + conv_b
    for back in range(1, SSD_CONV):
        rolled = pltpu.roll(xbc, back, axis=0)
        head8 = jnp.where(rows8 < back, pltpu.roll(tail, back, axis=0), rolled[0:8])
        shifted = jnp.concatenate([head8, rolled[8:]], axis=0)
        conv = conv + shifted * convw_ref[SSD_CONV - 1 - back:SSD_CONV - back]
    xbc = _silu(conv)
    yield
    xs, bmat, cmat = xbc[:, 0:512], xbc[:, 512:640], xbc[:, 640:768]
    dt = _softplus(dt_raw + dt_bias)
    da = dt * (-jnp.exp(a_log))
    cum = _cumsum_rows(da)
    yield
    cum_last = cum[CHUNK - 1:CHUNK]
    e_cum = jnp.exp(cum)
    xdt = xs * dt
    x_end = xdt * jnp.exp(cum_last - cum)
    decay = jnp.exp(cum_last)
    yield
    row = lax.broadcasted_iota(jnp.int32, (CHUNK, CHUNK), 0)
    col = lax.broadcasted_iota(jnp.int32, (CHUNK, CHUNK), 1)
    causal = col <= row
    groups = range(SSD_GROUPS)
    heads = range(SSD_HEADS)
    per_group = SSD_HEADS // SSD_GROUPS
    gs = [slice(grp * 256, (grp + 1) * 256) for grp in groups]
    ns = [slice(grp * SSD_STATE, (grp + 1) * SSD_STATE) for grp in groups]
    hs = [slice(h * HEAD_DIM, (h + 1) * HEAD_DIM) for h in heads]
    cb = [_dot_nt(cmat[:, ns[grp]], bmat[:, ns[grp]]) for grp in groups]
    y_off = [_dot(cmat[:, ns[grp]], st[grp]) for grp in groups]
    st_new = [st[grp] * decay[:, gs[grp]] + _dot_tn(bmat[:, ns[grp]], x_end[:, gs[grp]]) for grp in groups]
    yield
    cum_h = [cum[:, sl] for sl in hs]
    seg = [jnp.where(causal, jnp.exp(ch - ch.T), 0.0) for ch in cum_h]
    yield
    y_diag = [_dot(cb[h // per_group] * seg[h], xdt[:, hs[h]]) for h in heads]
    yield
    y = (jnp.concatenate(y_diag, axis=-1) + jnp.concatenate(y_off, axis=-1) * e_cum + skip * xs)
    y = y * _silu(z)
    parts = []
    for grp in range(SSD_GROUPS):
        yg = y[:, grp * 256:(grp + 1) * 256]
        ms = jnp.mean(yg * yg, axis=-1, keepdims=True)
        parts.append(yg * lax.rsqrt(ms + NORM_EPS))
    return jnp.concatenate(parts, axis=-1) * norm_gain, st_new, new_tail


def _mix_odd(u3, p, layer, consts):
    bsz, seq, n = u3.shape
    pair_masks = consts
    small = [p["c_lb"], p["c_norm_gain"], p["conv_w"], p["conv_b"], p["dt_bias"], p["a_log"], p["skip"],
             p["d_norm_gain"], pair_masks]

    def full(arr):
        nd = arr.ndim
        return pl.BlockSpec(arr.shape, lambda b, s, _nd=nd: (0,) * _nd)

    return pl.pallas_call(
        functools.partial(_mix_odd_kernel, layer=layer),
        grid=(bsz, seq // MIX_TILE),
        in_specs=[pl.BlockSpec((1, MIX_TILE, n), lambda b, s: (b, s, 0))] + [full(a) for a in small],
        out_specs=pl.BlockSpec((1, MIX_TILE, D_MODEL), lambda b, s: (b, s, 0)),
        out_shape=jax.ShapeDtypeStruct((bsz, seq, D_MODEL), BF16),
        scratch_shapes=[pltpu.VMEM((4, 128, 64), F32),
                        pltpu.VMEM((SSD_GROUPS, SSD_STATE, 256), F32),
                        pltpu.VMEM((8, SSD_XBC), F32)],
        compiler_params=pltpu.CompilerParams(dimension_semantics=("parallel", "arbitrary"),
                                             vmem_limit_bytes=VMEM_LIMIT),
        name="mix_odd",
    )(u3, *small)


def _out_proj_kernel(y_ref, w_ref, x_ref, g_ref, o_ref):
    o_ref[...] = x_ref[...] + g_ref[0] * jnp.dot(y_ref[...], w_ref[...], preferred_element_type=F32)


def _out_proj(y2, w, x2, gate, seq, tm):
    t, d = x2.shape
    per_b = seq // tm
    return pl.pallas_call(
        _out_proj_kernel,
        grid=(t // tm,),
        in_specs=[pl.BlockSpec((tm, d), lambda i: (i, 0)),
                  pl.BlockSpec((d, d), lambda i: (0, 0)),
                  pl.BlockSpec((tm, d), lambda i: (i, 0)),
                  pl.BlockSpec((1, 1, d), lambda i: (i // per_b, 0, 0))],
        out_specs=pl.BlockSpec((tm, d), lambda i: (i, 0)),
        out_shape=jax.ShapeDtypeStruct((t, d), F32),
        compiler_params=pltpu.CompilerParams(dimension_semantics=("parallel",),
                                             vmem_limit_bytes=VMEM_LIMIT),
        name="out_proj",
    )(y2, w, x2, gate)


def _route_kernel(x_ref, gain_ref, sc_ref, sh_ref, wr_ref, bias_ref, tri_ref, h_ref, e_ref, gt_ref, r_ref,
                  cnt_ref):
    h = _modulated_norm(x_ref[...], gain_ref[...], sc_ref[0], sh_ref[0])
    h_ref[...] = h.astype(h_ref.dtype)
    logits = lax.dot_general(wr_ref[...], h, (((1,), (1,)), ((), ())), precision=HIGHEST,
                             preferred_element_type=F32)
    score = _sigmoid(logits)
    sel = score + bias_ref[...]
    gscore = []
    for grp in range(N_EXPERT_GROUPS):
        a, b, c, d = [sel[grp * 4 + j:grp * 4 + j + 1] for j in range(4)]
        hi1, lo1, hi2, lo2 = jnp.maximum(a, b), jnp.minimum(a, b), jnp.maximum(c, d), jnp.minimum(c, d)
        gscore.append(jnp.maximum(hi1, hi2) + jnp.maximum(jnp.minimum(hi1, hi2), jnp.maximum(lo1, lo2)))
    best, gidx = gscore[0], jnp.zeros_like(gscore[0], dtype=jnp.int32)
    for grp in range(1, N_EXPERT_GROUPS):
        better = gscore[grp] > best
        gidx = jnp.where(better, grp, gidx)
        best = jnp.where(better, gscore[grp], best)
    vals, raw = [], []
    for j in range(EXPERTS_PER_GROUP):
        vj, rj = sel[j:j + 1], score[j:j + 1]
        for grp in range(1, N_EXPERT_GROUPS):
            vj = jnp.where(gidx == grp, sel[grp * 4 + j:grp * 4 + j + 1], vj)
            rj = jnp.where(gidx == grp, score[grp * 4 + j:grp * 4 + j + 1], rj)
        vals.append(vj)
        raw.append(rj)
    i1, m1, g1 = jnp.zeros_like(gidx), vals[0], raw[0]
    for j in range(1, EXPERTS_PER_GROUP):
        better = vals[j] > m1
        i1 = jnp.where(better, j, i1)
        m1 = jnp.where(better, vals[j], m1)
        g1 = jnp.where(better, raw[j], g1)
    i2, m2, g2 = jnp.zeros_like(gidx), jnp.full_like(m1, -jnp.inf), jnp.zeros_like(m1)
    for j in range(EXPERTS_PER_GROUP):
        better = jnp.logical_and(i1 != j, vals[j] > m2)
        i2 = jnp.where(better, j, i2)
        m2 = jnp.where(better, vals[j], m2)
        g2 = jnp.where(better, raw[j], g2)
    total = g1 + g2
    e1 = gidx * EXPERTS_PER_GROUP + i1
    e2 = gidx * EXPERTS_PER_GROUP + i2
    e_ref[0:1, :] = e1
    e_ref[1:2, :] = e2
    gt_ref[0:1, :] = g1 / total
    gt_ref[1:2, :] = g2 / total
    @pl.when(pl.program_id(0) == 0)
    def _():
        cnt_ref[...] = jnp.zeros_like(cnt_ref)

    eid = lax.broadcasted_iota(jnp.int32, logits.shape, 0)
    is1, is2 = eid == e1, eid == e2
    member = jnp.where(jnp.logical_or(is1, is2), 1.0, 0.0)
    before = jnp.dot(member.astype(BF16), tri_ref[...], preferred_element_type=F32) - member
    base = cnt_ref[:, 0:1] + before
    r_ref[0:1, :] = jnp.sum(jnp.where(is1, base, 0.0), axis=0, keepdims=True).astype(jnp.int32)
    r_ref[1:2, :] = jnp.sum(jnp.where(is2, base, 0.0), axis=0, keepdims=True).astype(jnp.int32)
    cnt_ref[...] = cnt_ref[...] + jnp.sum(member, axis=1, keepdims=True)


def _route(x2, gain, scale, shift, wr_t, bias_col, seq, tm):
    t, d = x2.shape
    per_b = seq // tm
    tri = jnp.asarray(np.triu(np.ones((tm, tm), np.float32)), dtype=BF16)
    return pl.pallas_call(
        _route_kernel,
        grid=(t // tm,),
        in_specs=[pl.BlockSpec((tm, d), lambda i: (i, 0)),
                  pl.BlockSpec((1, d), lambda i: (0, 0)),
                  pl.BlockSpec((1, 1, d), lambda i: (i // per_b, 0, 0)),
                  pl.BlockSpec((1, 1, d), lambda i: (i // per_b, 0, 0)),
                  pl.BlockSpec((N_EXPERTS, d), lambda i: (0, 0)),
                  pl.BlockSpec((N_EXPERTS, 1), lambda i: (0, 0)),
                  pl.BlockSpec((tm, tm), lambda i: (0, 0))],
        out_specs=[pl.BlockSpec((tm, d), lambda i: (i, 0)),
                   pl.BlockSpec((2, tm), lambda i: (0, i)),
                   pl.BlockSpec((2, tm), lambda i: (0, i)),
                   pl.BlockSpec((2, tm), lambda i: (0, i)),
                   pl.BlockSpec((N_EXPERTS, LANE), lambda i: (0, 0))],
        out_shape=[jax.ShapeDtypeStruct((t, d), BF16),
                   jax.ShapeDtypeStruct((2, t), jnp.int32),
                   jax.ShapeDtypeStruct((2, t), F32),
                   jax.ShapeDtypeStruct((2, t), jnp.int32),
                   jax.ShapeDtypeStruct((N_EXPERTS, LANE), F32)],
        compiler_params=pltpu.CompilerParams(dimension_semantics=("arbitrary",),
                                             vmem_limit_bytes=VMEM_LIMIT),
        name="route",
    )(x2, gain, scale, shift, wr_t, bias_col, tri)


def _expert_kernel(be_ref, bv_ref, x_ref, wg_ref, wu_ref, wd_ref, o_ref):
    i = pl.program_id(0)

    @pl.when(bv_ref[i] > 0)
    def _():
        x = x_ref[...]
        gate = jnp.dot(x, wg_ref[0, 0].astype(BF16), preferred_element_type=F32)
        up = jnp.dot(x, wu_ref[0, 0].astype(BF16), preferred_element_type=F32)
        act = (_silu(gate) * up).astype(BF16)
        y = jnp.dot(act, wd_ref[0, 0].astype(BF16), preferred_element_type=F32)
        o_ref[...] = y.astype(o_ref.dtype)

    @pl.when(bv_ref[i] == 0)
    def _():
        o_ref[...] = jnp.zeros_like(o_ref)


def _experts(xb, w_gate, w_up, w_down, layer, block_expert, block_valid):
    n_pad, d = xb.shape
    n_blocks = n_pad // MOE_ROWS
    de = w_gate.shape[-1]
    grid_spec = pltpu.PrefetchScalarGridSpec(
        num_scalar_prefetch=2,
        grid=(n_blocks,),
        in_specs=[pl.BlockSpec((MOE_ROWS, d), lambda i, be, bv: (i, 0)),
                  pl.BlockSpec((1, 1, d, de), lambda i, be, bv: (layer, be[i], 0, 0)),
                  pl.BlockSpec((1, 1, d, de), lambda i, be, bv: (layer, be[i], 0, 0)),
                  pl.BlockSpec((1, 1, de, d), lambda i, be, bv: (layer, be[i], 0, 0))],
        out_specs=pl.BlockSpec((MOE_ROWS, d), lambda i, be, bv: (i, 0)),
    )
    return pl.pallas_call(
        _expert_kernel,
        grid_spec=grid_spec,
        out_shape=jax.ShapeDtypeStruct((n_pad, d), BF16),
        compiler_params=pltpu.CompilerParams(dimension_semantics=("arbitrary",),
                                             vmem_limit_bytes=VMEM_LIMIT),
        name="experts",
    )(block_expert, block_valid, xb, w_gate, w_up, w_down)


def _combine_kernel(x_ref, y0_ref, y1_ref, w_ref, g_ref, fg_ref, o_ref, *, final):
    w = w_ref[...]
    moe = y0_ref[...].astype(F32) * w[:, 0:1] + y1_ref[...].astype(F32) * w[:, 1:2]
    x = x_ref[...] + g_ref[0] * moe
    if final:
        x = x * lax.rsqrt(jnp.mean(x * x, axis=-1, keepdims=True) + NORM_EPS) * fg_ref[...]
    o_ref[...] = x


def _combine(x2, y0, y1, weights, gate, final_gain, final, seq, tm):
    t, d = x2.shape
    per_b = seq // tm
    return pl.pallas_call(
        functools.partial(_combine_kernel, final=final),
        grid=(t // tm,),
        in_specs=[pl.BlockSpec((tm, d), lambda i: (i, 0)),
                  pl.BlockSpec((tm, d), lambda i: (i, 0)),
                  pl.BlockSpec((tm, d), lambda i: (i, 0)),
                  pl.BlockSpec((tm, 2), lambda i: (i, 0)),
                  pl.BlockSpec((1, 1, d), lambda i: (i // per_b, 0, 0)),
                  pl.BlockSpec((1, d), lambda i: (0, 0))],
        out_specs=pl.BlockSpec((tm, d), lambda i: (i, 0)),
        out_shape=jax.ShapeDtypeStruct((t, d), F32),
        compiler_params=pltpu.CompilerParams(dimension_semantics=("parallel",),
                                             vmem_limit_bytes=VMEM_LIMIT),
        name="combine",
    )(x2, y0, y1, weights, gate, final_gain)


def _moe(x2, gain, scale, shift, gate, wr_t, bias_col, w_gate, w_up, w_down, layer, final_gain, final, seq, tm):
    t, d = x2.shape
    h, experts, weights, ranks, cnt = _route(x2, gain, scale, shift, wr_t, bias_col, seq, tm)
    counts = cnt[:, 0].astype(jnp.int32)
    padded = (counts + MOE_ROWS - 1) // MOE_ROWS * MOE_ROWS
    pad_end = jnp.cumsum(padded)
    pad_start = pad_end - padded
    n_blocks = 2 * t // MOE_ROWS + N_EXPERTS
    n_pad = n_blocks * MOE_ROWS
    starts = jnp.arange(n_blocks, dtype=jnp.int32) * MOE_ROWS
    dest = ranks
    block_expert = jnp.zeros((n_blocks,), jnp.int32)
    for e in range(N_EXPERTS):
        dest = dest + jnp.where(experts == e, pad_start[e], 0)
        block_expert = block_expert + (starts >= pad_end[e]).astype(jnp.int32)
    block_expert = jnp.minimum(block_expert, N_EXPERTS - 1)
    block_valid = (starts < pad_end[-1]).astype(jnp.int32)
    xb = jnp.zeros((n_pad, d), h.dtype).at[dest.reshape(-1)].set(jnp.concatenate([h, h], axis=0))
    yb = _experts(xb, w_gate, w_up, w_down, layer, block_expert, block_valid)
    y0 = jnp.take(yb, dest[0], axis=0)
    y1 = jnp.take(yb, dest[1], axis=0)
    return _combine(x2, y0, y1, weights.T, gate, final_gain, final, seq, tm)


def kernel(x, c, norm_gain, w_ada, b_ada, w_in_even, w_in_odd, w_out, a_mu, a_w0, a_w_up, a_a0, a_a_up,
           a_g_up, a_k_k, a_k_a, a_r_k, a_ln_gain, a_ln_bias, b_alpha_up, b_alpha_bias, b_norm_gain, c_lb,
           c_norm_gain, d_conv_w, d_conv_b, d_dt_bias, d_a_log, d_skip, d_norm_gain, w_router, router_bias,
           w_gate, w_up, w_down, final_gain):
    bsz, seq, d = x.shape
    depth = w_ada.shape[0]
    t = bsz * seq
    tm = min(512, seq)
    pair_masks = _chunk_constants()
    ind = _head_indicator(LANE, HEAD_DIM)

    mods = _ada(c, w_ada, b_ada).reshape(depth, bsz, 6, 1, d)
    wr_t = w_router.T
    bias_col = router_bias.reshape(N_EXPERTS, 1)

    x2 = x.reshape(t, d)
    for l in range(depth):
        j = l // 2
        sh_m, sc_m, g_m, sh_f, sc_f, g_f = [mods[l, :, i] for i in range(6)]
        gain_m, gain_f = norm_gain[l, 0].reshape(1, d), norm_gain[l, 1].reshape(1, d)
        if l % 2 == 0:
            w = w_in_even[j]
            zpad = jnp.zeros((d, LANE - GLA_GATE_RANK), w.dtype)
            w = jnp.concatenate([w[:, :RWKV_IN + 1024 + GLA_GATE_RANK], zpad,
                                 w[:, RWKV_IN + 1024 + GLA_GATE_RANK:]], axis=1).astype(BF16)
            u = _in_proj(x2, gain_m, sc_m, sh_m, w, seq, tm)
            alpha_up = jnp.concatenate([b_alpha_up[j], jnp.zeros((LANE - GLA_GATE_RANK, 256), F32)], axis=0)
            p = dict(mu=_row(a_mu[j]), w0=_row(a_w0[j]), w_up=a_w_up[j], a0=_row(a_a0[j]), a_up=a_a_up[j],
                     g_up=a_g_up[j], k_k=_row(a_k_k[j]), k_a=_row(a_k_a[j]), r_k=_row(a_r_k[j]),
                     ln_gain=_row(a_ln_gain[j]), ln_bias=_row(a_ln_bias[j]), alpha_up=alpha_up,
                     alpha_bias=_row(b_alpha_bias[j]), b_norm_gain=_row(b_norm_gain[j]))
            y = _mix_even(u.reshape(bsz, seq, EVEN_COLS), p, (pair_masks, ind))
        else:
            w = w_in_odd[j]
            n_main = w.shape[1] - SSD_HEADS
            w = jnp.concatenate([w[:, :n_main], jnp.repeat(w[:, n_main:], HEAD_DIM, axis=1)], axis=1).astype(BF16)
            u = _in_proj(x2, gain_m, sc_m, sh_m, w, seq, tm)
            p = dict(c_lb=c_lb.astype(F32), c_norm_gain=_row(c_norm_gain[j]), conv_w=d_conv_w[j],
                     conv_b=_row(d_conv_b[j]), dt_bias=_row(jnp.repeat(d_dt_bias[j], HEAD_DIM)),
                     a_log=_row(jnp.repeat(d_a_log[j], HEAD_DIM)), skip=_row(jnp.repeat(d_skip[j], HEAD_DIM)),
                     d_norm_gain=_row(d_norm_gain[j]))
            y = _mix_odd(u.reshape(bsz, seq, ODD_COLS), p, l, pair_masks)
        x2 = _out_proj(y.reshape(t, d), w_out[l].astype(BF16), x2, g_m, seq, tm)
        x2 = _moe(x2, gain_f, sc_f, sh_f, g_f, wr_t, bias_col, w_gate, w_up, w_down, l,
                  final_gain.reshape(1, d), l == depth - 1, seq, tm)
    return x2.reshape(bsz, seq, d)
```

```python
import functools

import numpy as np
import jax
import jax.numpy as jnp
from jax import lax
from jax.experimental import pallas as pl
from jax.experimental.pallas import tpu as pltpu

F32 = jnp.float32
BF16 = jnp.bfloat16
HIGHEST = lax.Precision.HIGHEST

D_MODEL = 1024
MIX_HALF = 512
HEAD_DIM = 64
CHUNK = 64
NORM_EPS = 1e-6
RWKV_HEADS = 8
RWKV_IN = 1792
RWKV_GN_EPS = 64e-5
RWKV_DECAY_SCALE = float(np.exp(-0.5))
GLA_HEADS = 4
GLA_GATE_RANK = 16
GLA_GATE_NORM = 16.0
SSD_HEADS = 8
SSD_GROUPS = 2
SSD_STATE = 64
SSD_CONV = 4
SSD_XBC = 768
N_EXPERTS = 16
N_EXPERT_GROUPS = 4
EXPERTS_PER_GROUP = 4
D_EXPERT = 512
LANE = 128
EVEN_COLS = 3456
ODD_COLS = 3328
MOE_ROWS = 512
MIX_TILE = 128
VMEM_LIMIT = 48 * 1024 * 1024


def _dot(a, b):
    return jnp.dot(a.astype(BF16), b.astype(BF16), preferred_element_type=F32)


def _dot_nt(a, b):
    return lax.dot_general(a.astype(BF16), b.astype(BF16), (((1,), (1,)), ((), ())),
                           preferred_element_type=F32)


def _dot_tn(a, b):
    return lax.dot_general(a.astype(BF16), b.astype(BF16), (((0,), (0,)), ((), ())),
                           preferred_element_type=F32)


def _dot_exact(a, b):
    return jnp.dot(a, b, precision=HIGHEST, preferred_element_type=F32)


def _sigmoid(x):
    return 1.0 / (1.0 + jnp.exp(-x))


def _silu(x):
    return x * _sigmoid(x)


def _softplus(x):
    return jnp.maximum(x, 0.0) + jnp.log(1.0 + jnp.exp(-jnp.abs(x)))


def _log_sigmoid(x):
    return jnp.minimum(x, 0.0) - jnp.log(1.0 + jnp.exp(-jnp.abs(x)))


def _chunk_constants():
    t = np.arange(CHUNK)
    masks = []
    for shift in range(5, -1, -1):
        masks.append(((t[:, None] > t[None, :]) & (((t[:, None] ^ t[None, :]) >> shift) == 1)))
    masks.append(t[:, None] == t[None, :])
    return jnp.asarray(np.stack(masks).astype(np.float32))


def _head_indicator(width, seg):
    i = np.arange(width)
    return jnp.asarray((i[:, None] // seg == i[None, :] // seg).astype(np.float32), dtype=BF16)


def _ada_kernel(c_ref, w_ref, b_ref, o_ref):
    cond = _silu(c_ref[...])
    o_ref[0] = _dot_exact(cond, w_ref[0]) + b_ref[0]


def _ada(c, w_ada, b_ada):
    depth, d, n = w_ada.shape
    bsz = c.shape[0]
    tn = 1536
    return pl.pallas_call(
        _ada_kernel,
        grid=(depth, n // tn),
        in_specs=[pl.BlockSpec((bsz, d), lambda l, j: (0, 0)),
                  pl.BlockSpec((1, d, tn), lambda l, j: (l, 0, j)),
                  pl.BlockSpec((1, 1, tn), lambda l, j: (l, 0, j))],
        out_specs=pl.BlockSpec((1, bsz, tn), lambda l, j: (l, 0, j)),
        out_shape=jax.ShapeDtypeStruct((depth, bsz, n), F32),
        compiler_params=pltpu.CompilerParams(dimension_semantics=("parallel", "parallel"),
                                             vmem_limit_bytes=VMEM_LIMIT),
        name="ada",
    )(c, w_ada, b_ada.reshape(depth, 1, n))


def _modulated_norm(x, gain, scale, shift):
    ms = jnp.mean(x * x, axis=-1, keepdims=True)
    return (x * lax.rsqrt(ms + NORM_EPS)) * gain * (1.0 + scale) + shift


def _in_proj_kernel(x_ref, gain_ref, sc_ref, sh_ref, w_ref, o_ref, *, col_chunk):
    h = _modulated_norm(x_ref[...], gain_ref[...], sc_ref[0], sh_ref[0]).astype(BF16)
    n = o_ref.shape[1]
    for j in range(0, n, col_chunk):
        o_ref[:, j:j + col_chunk] = jnp.dot(h, w_ref[:, j:j + col_chunk],
                                            preferred_element_type=F32).astype(o_ref.dtype)


def _in_proj(x2, gain, scale, shift, w, seq, tm):
    t, d = x2.shape
    n = w.shape[1]
    col_chunk = next(n // parts for parts in (3, 2, 1) if n % (parts * LANE) == 0)
    per_b = seq // tm
    return pl.pallas_call(
        functools.partial(_in_proj_kernel, col_chunk=col_chunk),
        grid=(t // tm,),
        in_specs=[pl.BlockSpec((tm, d), lambda i: (i, 0)),
                  pl.BlockSpec((1, d), lambda i: (0, 0)),
                  pl.BlockSpec((1, 1, d), lambda i: (i // per_b, 0, 0)),
                  pl.BlockSpec((1, 1, d), lambda i: (i // per_b, 0, 0)),
                  pl.BlockSpec((d, n), lambda i: (0, 0))],
        out_specs=pl.BlockSpec((tm, n), lambda i: (i, 0)),
        out_shape=jax.ShapeDtypeStruct((t, n), BF16),
        compiler_params=pltpu.CompilerParams(dimension_semantics=("parallel",),
                                             vmem_limit_bytes=VMEM_LIMIT),
        name="in_proj",
    )(x2, gain, scale, shift, w)


def _cumsum_rows(x):
    rows = lax.broadcasted_iota(jnp.int32, x.shape, 0)
    step = 1
    while step < CHUNK:
        x = x + jnp.where(rows >= step, pltpu.roll(x, step, axis=0), 0.0)
        step *= 2
    return x


def _level_refs(b):
    cols = b.shape[1]
    rows = lax.broadcasted_iota(jnp.int32, b.shape, 0)

    def spread(offset, span):
        pieces = [jnp.broadcast_to(b[s + offset:s + offset + 1], (span, cols)) for s in range(0, CHUNK, span)]
        return pieces[0] if len(pieces) == 1 else jnp.concatenate(pieces, axis=0)

    refs = [spread(n // 2 - 1, n) for n in (64, 32, 16, 8)]
    refs.append(jnp.where((rows & 7) < 4, spread(1, 8), spread(5, 8)))
    refs.append(jnp.where((rows & 1) == 1, pltpu.roll(b, 1, axis=0), b))
    return refs


def _gla_chunk(q, k, v, g, st, pair_masks, heads, dk, dv):
    b = _cumsum_rows(g)
    refs = _level_refs(b)
    b_last = b[CHUNK - 1:CHUNK]
    q_in = q * jnp.exp(b)
    k_st = k * jnp.exp(b_last - b)
    decay = jnp.exp(b_last)
    yield
    scores = [None] * heads
    for lvl in range(7):
        if lvl < 6:
            e = jnp.exp(-jnp.abs(b - refs[lvl]))
            qe, ke = q * e, k * e
        else:
            qe, ke = q, k
        keep = pair_masks[lvl] > 0.5
        for h in range(heads):
            p = _dot_nt(qe[:, h * dk:(h + 1) * dk], ke[:, h * dk:(h + 1) * dk])
            p = jnp.where(keep, p, 0.0)
            scores[h] = p if scores[h] is None else scores[h] + p
        yield
    hs = range(heads)
    ks = [slice(h * dk, (h + 1) * dk) for h in hs]
    vs = [slice(h * dv, (h + 1) * dv) for h in hs]
    o_inter = [_dot_nt(q_in[:, ks[h]], st[h]) for h in hs]
    o_intra = [_dot(scores[h], v[:, vs[h]]) for h in hs]
    yield
    st_new = [st[h] * decay[:, ks[h]] + _dot_tn(v[:, vs[h]], k_st[:, ks[h]]) for h in hs]
    return jnp.concatenate([o_inter[h] + o_intra[h] for h in hs], axis=-1), st_new


def _interleave(*stages):
    results = [None] * len(stages)
    live = list(range(len(stages)))
    while live:
        for i in list(live):
            try:
                next(stages[i])
            except StopIteration as stop:
                results[i] = stop.value
                live.remove(i)
    return results


def _head_rms(o, gain, heads, dv):
    outs = []
    for h in range(heads):
        oh = o[:, h * dv:(h + 1) * dv]
        ms = jnp.mean(oh * oh, axis=-1, keepdims=True)
        outs.append(oh * lax.rsqrt(ms + NORM_EPS) * gain)
    return jnp.concatenate(outs, axis=-1)


def _rwkv_chunk(r, kk, a, kt, v, logw, s0):
    b = _cumsum_rows(logw)
    b_last = b[CHUNK - 1:CHUNK]
    e_neg = jnp.exp(-b)
    e_end = jnp.exp(b_last - b)
    decay = jnp.exp(b_last)
    beta = a * kk
    k_bar = kk * jnp.exp(b - logw)
    r_bar = r * jnp.exp(b)
    beta_t, k_t = beta * e_neg, kt * e_neg
    beta_hat, k_hat = beta * e_end, kt * e_end
    row = lax.broadcasted_iota(jnp.int32, (CHUNK, CHUNK), 0)
    col = lax.broadcasted_iota(jnp.int32, (CHUNK, CHUNK), 1)
    strict = col < row
    incl = col <= row
    same_blk = (row >> 4) == (col >> 4)
    eye = (row == col).astype(F32)
    hs = range(RWKV_HEADS)
    sls = [slice(h * HEAD_DIM, (h + 1) * HEAD_DIM) for h in hs]
    kr = [jnp.concatenate([k_bar[:, sl], r_bar[:, sl]], axis=0) for sl in sls]
    bk = [jnp.concatenate([beta_t[:, sl], k_t[:, sl]], axis=0) for sl in sls]
    vh = [v[:, sl] for sl in sls]
    yield
    m1 = [_dot_nt(kr[h], bk[h]) for h in hs]
    yield
    m2 = [_dot_nt(kr[h], s0[h]) for h in hs]
    yield
    a_m = [jnp.where(strict, m[0:CHUNK, 0:CHUNK], 0.0) for m in m1]
    b_m = [jnp.where(strict, m[0:CHUNK, CHUNK:], 0.0) for m in m1]
    cb_m = [jnp.where(incl, m[CHUNK:, 0:CHUNK], 0.0) for m in m1]
    ck_m = [jnp.where(incl, m[CHUNK:, CHUNK:], 0.0) for m in m1]
    x1 = [jnp.where(same_blk, -am, 0.0) for am in a_m]
    a_off = [jnp.where(same_blk, 0.0, am) for am in a_m]
    rhs = [m2[h][0:CHUNK] + _dot(b_m[h], vh[h]) for h in hs]
    lo, hi = slice(0, CHUNK), slice(CHUNK, 2 * CHUNK)
    side = lambda left, right: jnp.concatenate([left, right], axis=1)
    x2 = [_dot(x, x) for x in x1]
    yield
    p = [eye + x for x in x1]
    w = [_dot(x2[h], side(p[h], x2[h])) for h in hs]
    yield
    p = [p[h] + w[h][:, lo] for h in hs]
    x4 = [w[h][:, hi] for h in hs]
    w = [_dot(x4[h], side(p[h], x4[h])) for h in hs]
    yield
    p = [p[h] + w[h][:, lo] for h in hs]
    t_d = [p[h] + _dot(w[h][:, hi], p[h]) for h in hs]
    yield
    nz = [_dot(t_d[h], side(a_off[h], rhs[h])) for h in hs]
    yield
    w = [_dot(nz[h][:, lo], nz[h]) for h in hs]
    yield
    y1 = [nz[h][:, hi] - w[h][:, hi] for h in hs]
    u = [y1[h] + _dot(w[h][:, lo], y1[h]) for h in hs]
    yield
    outs = [m2[h][CHUNK:] + _dot(side(ck_m[h], -cb_m[h]), jnp.concatenate([vh[h], u[h]], axis=0)) for h in hs]
    yield
    s_new = [s0[h] * decay[:, sls[h]]
             + _dot_tn(jnp.concatenate([vh[h], -u[h]], axis=0),
                       jnp.concatenate([k_hat[:, sls[h]], beta_hat[:, sls[h]]], axis=0)) for h in hs]
    return jnp.concatenate(outs, axis=-1), s_new


def _head_sums(x, ind):
    return jnp.concatenate([_dot(x[:, i:i + LANE], ind) for i in range(0, x.shape[1], LANE)], axis=1)


def _mix_even_kernel(u_ref, mu_ref, w0_ref, wup_ref, a0_ref, aup_ref, gup_ref, kk_ref, ka_ref,
                     rk_ref, lng_ref, lnb_ref, alup_ref, albias_ref, bng_ref, ind_ref,
                     pmask_ref, y_ref, s_ref, g_ref, prev_ref):
    @pl.when(pl.program_id(1) == 0)
    def _():
        s_ref[...] = jnp.zeros_like(s_ref)
        g_ref[...] = jnp.zeros_like(g_ref)
        prev_ref[...] = jnp.zeros_like(prev_ref)

    pair_masks = pmask_ref[...]
    ind = ind_ref[...]
    s_state = [s_ref[h] for h in range(RWKV_HEADS)]
    g_state = [g_ref[h] for h in range(GLA_HEADS)]
    prev = prev_ref[...]
    def rwkv_stages(rs, prev, s_state):
        ua = u_ref[0, rs, 0:RWKV_IN].astype(F32)
        rows = lax.broadcasted_iota(jnp.int32, ua.shape, 0)
        shifted = jnp.where(rows == 0, prev, pltpu.roll(ua, 1, axis=0))
        xa = ua + mu_ref[...] * (shifted - ua)
        r, k, v = xa[:, 0:512], xa[:, 512:1024], xa[:, 1024:1536]
        wd, ad, gd = xa[:, 1536:1600], xa[:, 1600:1664], xa[:, 1664:1792]
        logw = -RWKV_DECAY_SCALE * _sigmoid(w0_ref[...] + _dot(jnp.tanh(wd), wup_ref[...]))
        a = _sigmoid(a0_ref[...] + _dot(ad, aup_ref[...]))
        gate = _dot(_sigmoid(gd), gup_ref[...])
        yield
        kk = k * kk_ref[...]
        kk = kk * lax.rsqrt(_head_sums(kk * kk, ind) + 1e-12)
        kt = k * (1.0 + (a - 1.0) * ka_ref[...])
        yield
        y, s_state = yield from _rwkv_chunk(r, kk, a, kt, v, logw, s_state)
        yield
        mean = _head_sums(y, ind) * (1.0 / HEAD_DIM)
        yc = y - mean
        yield
        var = _head_sums(yc * yc, ind) * (1.0 / HEAD_DIM)
        y = yc * lax.rsqrt(var + RWKV_GN_EPS) * lng_ref[...] + lnb_ref[...]
        y = y + _head_sums(r * kt * rk_ref[...], ind) * v
        y_ref[0, rs, 0:MIX_HALF] = (y * gate).astype(y_ref.dtype)
        return ua[CHUNK - 1:CHUNK], s_state

    def gla_stages(rs, g_state):
        ub = u_ref[0, rs, RWKV_IN:].astype(F32)
        q, kg, vg = ub[:, 0:256] * (HEAD_DIM ** -0.5), ub[:, 256:512], ub[:, 512:1024]
        alpha, gg = ub[:, 1024:1152], ub[:, 1152:1664]
        log_a = _log_sigmoid(_dot(alpha, alup_ref[...]) + albias_ref[...]) * (1.0 / GLA_GATE_NORM)
        yield
        o, g_state = yield from _gla_chunk(q, kg, vg, log_a, g_state, pair_masks, GLA_HEADS, 64, 128)
        o = _head_rms(o, bng_ref[...], GLA_HEADS, 128)
        y_ref[0, rs, MIX_HALF:] = (o * _silu(gg)).astype(y_ref.dtype)
        return g_state

    for c in range(u_ref.shape[1] // CHUNK):
        rs = slice(c * CHUNK, (c + 1) * CHUNK)
        (prev, s_state), g_state = _interleave(rwkv_stages(rs, prev, s_state), gla_stages(rs, g_state))
    for h in range(RWKV_HEADS):
        s_ref[h] = s_state[h]
    for h in range(GLA_HEADS):
        g_ref[h] = g_state[h]
    prev_ref[...] = prev


def _row(p):
    return p.reshape(1, -1).astype(F32)


def _mix_even(u3, p, consts):
    bsz, seq, n = u3.shape
    pair_masks, ind = consts
    small = [p["mu"], p["w0"], p["w_up"], p["a0"], p["a_up"], p["g_up"], p["k_k"], p["k_a"], p["r_k"],
             p["ln_gain"], p["ln_bias"], p["alpha_up"], p["alpha_bias"], p["b_norm_gain"],
             ind, pair_masks]

    def full(arr):
        nd = arr.ndim
        return pl.BlockSpec(arr.shape, lambda b, s, _nd=nd: (0,) * _nd)

    return pl.pallas_call(
        _mix_even_kernel,
        grid=(bsz, seq // MIX_TILE),
        in_specs=[pl.BlockSpec((1, MIX_TILE, n), lambda b, s: (b, s, 0))] + [full(a) for a in small],
        out_specs=pl.BlockSpec((1, MIX_TILE, D_MODEL), lambda b, s: (b, s, 0)),
        out_shape=jax.ShapeDtypeStruct((bsz, seq, D_MODEL), BF16),
        scratch_shapes=[pltpu.VMEM((RWKV_HEADS, HEAD_DIM, HEAD_DIM), F32),
                        pltpu.VMEM((GLA_HEADS, 128, 64), F32),
                        pltpu.VMEM((1, RWKV_IN), F32)],
        compiler_params=pltpu.CompilerParams(dimension_semantics=("parallel", "arbitrary"),
                                             vmem_limit_bytes=VMEM_LIMIT),
        name="mix_even",
    )(u3, *small)


def _mix_odd_kernel(u_ref, clb_ref, cng_ref, convw_ref, convb_ref, dtb_ref, alog_ref, skip_ref,
                    dng_ref, pmask_ref, y_ref, h_ref, d_ref, tail_ref, *, layer):
    @pl.when(pl.program_id(1) == 0)
    def _():
        h_ref[...] = jnp.zeros_like(h_ref)
        d_ref[...] = jnp.zeros_like(d_ref)
        tail_ref[...] = jnp.zeros_like(tail_ref)

    pair_masks = pmask_ref[...]
    c_lb = clb_ref[...]
    c_exp = jnp.exp(c_lb - jnp.max(c_lb, axis=0, keepdims=True))
    lb = jnp.sum(c_exp[1:layer + 1], axis=0, keepdims=True) / jnp.sum(c_exp, axis=0, keepdims=True)
    h_state = [h_ref[h] for h in range(4)]
    d_state = [d_ref[grp] for grp in range(SSD_GROUPS)]
    tail = tail_ref[...]
    def hgrn_stages(rs, h_state):
        u = u_ref[0, rs, 0:1536].astype(F32)
        q, fr, iv, g = u[:, 0:256], u[:, 256:512], u[:, 512:1024], u[:, 1024:1536]
        f = lb + (1.0 - lb) * _sigmoid(fr)
        yield
        o, h_state = yield from _gla_chunk(q, 1.0 - f, iv, jnp.log(f), h_state, pair_masks, 4, 64, 128)
        y_ref[0, rs, 0:MIX_HALF] = (_head_rms(o, cng_ref[...], 4, 128) * _silu(g)).astype(y_ref.dtype)
        return h_state

    def ssd_stages(rs, tail, d_state):
        u = u_ref[0, rs, 1536:].astype(F32)
        y, d_state, tail = yield from _ssd_chunk(u, tail, d_state, convw_ref, convb_ref[...], dtb_ref[...],
                                                 alog_ref[...], skip_ref[...], dng_ref[...])
        y_ref[0, rs, MIX_HALF:] = y.astype(y_ref.dtype)
        return tail, d_state

    for c in range(u_ref.shape[1] // CHUNK):
        rs = slice(c * CHUNK, (c + 1) * CHUNK)
        (h_state,) = _interleave(hgrn_stages(rs, h_state))
        ((tail, d_state),) = _interleave(ssd_stages(rs, tail, d_state))
    for h in range(4):
        h_ref[h] = h_state[h]
    for grp in range(SSD_GROUPS):
        d_ref[grp] = d_state[grp]
    tail_ref[...] = tail


def _ssd_chunk(u, tail, st, convw_ref, conv_b, dt_bias, a_log, skip, norm_gain):
    z, xbc, dt_raw = u[:, 0:512], u[:, 512:1280], u[:, 1280:1792]
    new_tail = xbc[CHUNK - 8:CHUNK]
    rows8 = lax.broadcasted_iota(jnp.int32, (8, SSD_XBC), 0)
    conv = xbc * convw_ref[SSD_CONV - 1:SSD_CONV] + conv_b
    for back in range(1, SSD_CONV):
        rolled = pltpu.roll(xbc, back, axis=0)
        head8 = jnp.where(rows8 < back, pltpu.roll(tail, back, axis=0), rolled[0:8])
        shifted = jnp.concatenate([head8, rolled[8:]], axis=0)
        conv = conv + shifted * convw_ref[SSD_CONV - 1 - back:SSD_CONV - back]
    xbc = _silu(conv)
    yield
    xs, bmat, cmat = xbc[:, 0:512], xbc[:, 512:640], xbc[:, 640:768]
    dt = _softplus(dt_raw + dt_bias)
    da = dt * (-jnp.exp(a_log))
    cum = _cumsum_rows(da)
    yield
    cum_last = cum[CHUNK - 1:CHUNK]
    e_cum = jnp.exp(cum)
    xdt = xs * dt
    x_end = xdt * jnp.exp(cum_last - cum)
    decay = jnp.exp(cum_last)
    yield
    row = lax.broadcasted_iota(jnp.int32, (CHUNK, CHUNK), 0)
    col = lax.broadcasted_iota(jnp.int32, (CHUNK, CHUNK), 1)
    causal = col <= row
    groups = range(SSD_GROUPS)
    heads = range(SSD_HEADS)
    per_group = SSD_HEADS // SSD_GROUPS
    gs = [slice(grp * 256, (grp + 1) * 256) for grp in groups]
    ns = [slice(grp * SSD_STATE, (grp + 1) * SSD_STATE) for grp in groups]
    hs = [slice(h * HEAD_DIM, (h + 1) * HEAD_DIM) for h in heads]
    cb = [_dot_nt(cmat[:, ns[grp]], bmat[:, ns[grp]]) for grp in groups]
    y_off = [_dot(cmat[:, ns[grp]], st[grp]) for grp in groups]
    st_new = [st[grp] * decay[:, gs[grp]] + _dot_tn(bmat[:, ns[grp]], x_end[:, gs[grp]]) for grp in groups]
    yield
    cum_h = [cum[:, sl] for sl in hs]
    seg = [jnp.where(causal, jnp.exp(ch - ch.T), 0.0) for ch in cum_h]
    yield
    y_diag = [_dot(cb[h // per_group] * seg[h], xdt[:, hs[h]]) for h in heads]
    yield
    y = (jnp.concatenate(y_diag, axis=-1) + jnp.concatenate(y_off, axis=-1) * e_cum + skip * xs)
    y = y * _silu(z)
    parts = []
    for grp in range(SSD_GROUPS):
        yg = y[:, grp * 256:(grp + 1) * 256]
        ms = jnp.mean(yg * yg, axis=-1, keepdims=True)
        parts.append(yg * lax.rsqrt(ms + NORM_EPS))
    return jnp.concatenate(parts, axis=-1) * norm_gain, st_new, new_tail


def _mix_odd(u3, p, layer, consts):
    bsz, seq, n = u3.shape
    pair_masks = consts
    small = [p["c_lb"], p["c_norm_gain"], p["conv_w"], p["conv_b"], p["dt_bias"], p["a_log"], p["skip"],
             p["d_norm_gain"], pair_masks]

    def full(arr):
        nd = arr.ndim
        return pl.BlockSpec(arr.shape, lambda b, s, _nd=nd: (0,) * _nd)

    return pl.pallas_call(
        functools.partial(_mix_odd_kernel, layer=layer),
        grid=(bsz, seq // MIX_TILE),
        in_specs=[pl.BlockSpec((1, MIX_TILE, n), lambda b, s: (b, s, 0))] + [full(a) for a in small],
        out_specs=pl.BlockSpec((1, MIX_TILE, D_MODEL), lambda b, s: (b, s, 0)),
        out_shape=jax.ShapeDtypeStruct((bsz, seq, D_MODEL), BF16),
        scratch_shapes=[pltpu.VMEM((4, 128, 64), F32),
                        pltpu.VMEM((SSD_GROUPS, SSD_STATE, 256), F32),
                        pltpu.VMEM((8, SSD_XBC), F32)],
        compiler_params=pltpu.CompilerParams(dimension_semantics=("parallel", "arbitrary"),
                                             vmem_limit_bytes=VMEM_LIMIT),
        name="mix_odd",
    )(u3, *small)


def _out_proj_kernel(y_ref, w_ref, x_ref, g_ref, o_ref):
    o_ref[...] = x_ref[...] + g_ref[0] * jnp.dot(y_ref[...], w_ref[...], preferred_element_type=F32)


def _out_proj(y2, w, x2, gate, seq, tm):
    t, d = x2.shape
    per_b = seq // tm
    return pl.pallas_call(
        _out_proj_kernel,
        grid=(t // tm,),
        in_specs=[pl.BlockSpec((tm, d), lambda i: (i, 0)),
                  pl.BlockSpec((d, d), lambda i: (0, 0)),
                  pl.BlockSpec((tm, d), lambda i: (i, 0)),
                  pl.BlockSpec((1, 1, d), lambda i: (i // per_b, 0, 0))],
        out_specs=pl.BlockSpec((tm, d), lambda i: (i, 0)),
        out_shape=jax.ShapeDtypeStruct((t, d), F32),
        compiler_params=pltpu.CompilerParams(dimension_semantics=("parallel",),
                                             vmem_limit_bytes=VMEM_LIMIT),
        name="out_proj",
    )(y2, w, x2, gate)


def _route_kernel(x_ref, gain_ref, sc_ref, sh_ref, wr_ref, bias_ref, tri_ref, h_ref, e_ref, gt_ref, r_ref,
                  cnt_ref):
    h = _modulated_norm(x_ref[...], gain_ref[...], sc_ref[0], sh_ref[0])
    h_ref[...] = h.astype(h_ref.dtype)
    logits = lax.dot_general(wr_ref[...], h, (((1,), (1,)), ((), ())), precision=HIGHEST,
                             preferred_element_type=F32)
    score = _sigmoid(logits)
    sel = score + bias_ref[...]
    gscore = []
    for grp in range(N_EXPERT_GROUPS):
        a, b, c, d = [sel[grp * 4 + j:grp * 4 + j + 1] for j in range(4)]
        hi1, lo1, hi2, lo2 = jnp.maximum(a, b), jnp.minimum(a, b), jnp.maximum(c, d), jnp.minimum(c, d)
        gscore.append(jnp.maximum(hi1, hi2) + jnp.maximum(jnp.minimum(hi1, hi2), jnp.maximum(lo1, lo2)))
    best, gidx = gscore[0], jnp.zeros_like(gscore[0], dtype=jnp.int32)
    for grp in range(1, N_EXPERT_GROUPS):
        better = gscore[grp] > best
        gidx = jnp.where(better, grp, gidx)
        best = jnp.where(better, gscore[grp], best)
    vals, raw = [], []
    for j in range(EXPERTS_PER_GROUP):
        vj, rj = sel[j:j + 1], score[j:j + 1]
        for grp in range(1, N_EXPERT_GROUPS):
            vj = jnp.where(gidx == grp, sel[grp * 4 + j:grp * 4 + j + 1], vj)
            rj = jnp.where(gidx == grp, score[grp * 4 + j:grp * 4 + j + 1], rj)
        vals.append(vj)
        raw.append(rj)
    i1, m1, g1 = jnp.zeros_like(gidx), vals[0], raw[0]
    for j in range(1, EXPERTS_PER_GROUP):
        better = vals[j] > m1
        i1 = jnp.where(better, j, i1)
        m1 = jnp.where(better, vals[j], m1)
        g1 = jnp.where(better, raw[j], g1)
    i2, m2, g2 = jnp.zeros_like(gidx), jnp.full_like(m1, -jnp.inf), jnp.zeros_like(m1)
    for j in range(EXPERTS_PER_GROUP):
        better = jnp.logical_and(i1 != j, vals[j] > m2)
        i2 = jnp.where(better, j, i2)
        m2 = jnp.where(better, vals[j], m2)
        g2 = jnp.where(better, raw[j], g2)
    total = g1 + g2
    e1 = gidx * EXPERTS_PER_GROUP + i1
    e2 = gidx * EXPERTS_PER_GROUP + i2
    e_ref[0:1, :] = e1
    e_ref[1:2, :] = e2
    gt_ref[0:1, :] = g1 / total
    gt_ref[1:2, :] = g2 / total
    @pl.when(pl.program_id(0) == 0)
    def _():
        cnt_ref[...] = jnp.zeros_like(cnt_ref)

    eid = lax.broadcasted_iota(jnp.int32, logits.shape, 0)
    is1, is2 = eid == e1, eid == e2
    member = jnp.where(jnp.logical_or(is1, is2), 1.0, 0.0)
    before = jnp.dot(member.astype(BF16), tri_ref[...], preferred_element_type=F32) - member
    base = cnt_ref[:, 0:1] + before
    r_ref[0:1, :] = jnp.sum(jnp.where(is1, base, 0.0), axis=0, keepdims=True).astype(jnp.int32)
    r_ref[1:2, :] = jnp.sum(jnp.where(is2, base, 0.0), axis=0, keepdims=True).astype(jnp.int32)
    cnt_ref[...] = cnt_ref[...] + jnp.sum(member, axis=1, keepdims=True)


def _route(x2, gain, scale, shift, wr_t, bias_col, seq, tm):
    t, d = x2.shape
    per_b = seq // tm
    tri = jnp.asarray(np.triu(np.ones((tm, tm), np.float32)), dtype=BF16)
    return pl.pallas_call(
        _route_kernel,
        grid=(t // tm,),
        in_specs=[pl.BlockSpec((tm, d), lambda i: (i, 0)),
                  pl.BlockSpec((1, d), lambda i: (0, 0)),
                  pl.BlockSpec((1, 1, d), lambda i: (i // per_b, 0, 0)),
                  pl.BlockSpec((1, 1, d), lambda i: (i // per_b, 0, 0)),
                  pl.BlockSpec((N_EXPERTS, d), lambda i: (0, 0)),
                  pl.BlockSpec((N_EXPERTS, 1), lambda i: (0, 0)),
                  pl.BlockSpec((tm, tm), lambda i: (0, 0))],
        out_specs=[pl.BlockSpec((tm, d), lambda i: (i, 0)),
                   pl.BlockSpec((2, tm), lambda i: (0, i)),
                   pl.BlockSpec((2, tm), lambda i: (0, i)),
                   pl.BlockSpec((2, tm), lambda i: (0, i)),
                   pl.BlockSpec((N_EXPERTS, LANE), lambda i: (0, 0))],
        out_shape=[jax.ShapeDtypeStruct((t, d), BF16),
                   jax.ShapeDtypeStruct((2, t), jnp.int32),
                   jax.ShapeDtypeStruct((2, t), F32),
                   jax.ShapeDtypeStruct((2, t), jnp.int32),
                   jax.ShapeDtypeStruct((N_EXPERTS, LANE), F32)],
        compiler_params=pltpu.CompilerParams(dimension_semantics=("arbitrary",),
                                             vmem_limit_bytes=VMEM_LIMIT),
        name="route",
    )(x2, gain, scale, shift, wr_t, bias_col, tri)


def _expert_kernel(be_ref, bv_ref, x_ref, wg_ref, wu_ref, wd_ref, o_ref, wg_s, wu_s, wd_s):
    i = pl.program_id(0)

    @pl.when(jnp.logical_or(i == 0, be_ref[i] != be_ref[jnp.maximum(i - 1, 0)]))
    def _():
        wg_s[...] = wg_ref[0, 0].astype(BF16)
        wu_s[...] = wu_ref[0, 0].astype(BF16)
        wd_s[...] = wd_ref[0, 0].astype(BF16)

    @pl.when(bv_ref[i] > 0)
    def _():
        x = x_ref[...]
        gate = jnp.dot(x, wg_s[...], preferred_element_type=F32)
        up = jnp.dot(x, wu_s[...], preferred_element_type=F32)
        act = (_silu(gate) * up).astype(BF16)
        y = jnp.dot(act, wd_s[...], preferred_element_type=F32)
        o_ref[...] = y.astype(o_ref.dtype)

    @pl.when(bv_ref[i] == 0)
    def _():
        o_ref[...] = jnp.zeros_like(o_ref)


def _experts(xb, w_gate, w_up, w_down, layer, block_expert, block_valid):
    n_pad, d = xb.shape
    n_blocks = n_pad // MOE_ROWS
    de = w_gate.shape[-1]
    grid_spec = pltpu.PrefetchScalarGridSpec(
        num_scalar_prefetch=2,
        grid=(n_blocks,),
        in_specs=[pl.BlockSpec((MOE_ROWS, d), lambda i, be, bv: (i, 0)),
                  pl.BlockSpec((1, 1, d, de), lambda i, be, bv: (layer, be[i], 0, 0)),
                  pl.BlockSpec((1, 1, d, de), lambda i, be, bv: (layer, be[i], 0, 0)),
                  pl.BlockSpec((1, 1, de, d), lambda i, be, bv: (layer, be[i], 0, 0))],
        out_specs=pl.BlockSpec((MOE_ROWS, d), lambda i, be, bv: (i, 0)),
        scratch_shapes=[pltpu.VMEM((d, de), BF16), pltpu.VMEM((d, de), BF16), pltpu.VMEM((de, d), BF16)],
    )
    return pl.pallas_call(
        _expert_kernel,
        grid_spec=grid_spec,
        out_shape=jax.ShapeDtypeStruct((n_pad, d), BF16),
        compiler_params=pltpu.CompilerParams(dimension_semantics=("arbitrary",),
                                             vmem_limit_bytes=VMEM_LIMIT),
        name="experts",
    )(block_expert, block_valid, xb, w_gate, w_up, w_down)


def _combine_kernel(x_ref, y0_ref, y1_ref, w_ref, g_ref, fg_ref, o_ref, *, final):
    w = w_ref[...]
    moe = y0_ref[...].astype(F32) * w[:, 0:1] + y1_ref[...].astype(F32) * w[:, 1:2]
    x = x_ref[...] + g_ref[0] * moe
    if final:
        x = x * lax.rsqrt(jnp.mean(x * x, axis=-1, keepdims=True) + NORM_EPS) * fg_ref[...]
    o_ref[...] = x


def _combine(x2, y0, y1, weights, gate, final_gain, final, seq, tm):
    t, d = x2.shape
    per_b = seq // tm
    return pl.pallas_call(
        functools.partial(_combine_kernel, final=final),
        grid=(t // tm,),
        in_specs=[pl.BlockSpec((tm, d), lambda i: (i, 0)),
                  pl.BlockSpec((tm, d), lambda i: (i, 0)),
                  pl.BlockSpec((tm, d), lambda i: (i, 0)),
                  pl.BlockSpec((tm, 2), lambda i: (i, 0)),
                  pl.BlockSpec((1, 1, d), lambda i: (i // per_b, 0, 0)),
                  pl.BlockSpec((1, d), lambda i: (0, 0))],
        out_specs=pl.BlockSpec((tm, d), lambda i: (i, 0)),
        out_shape=jax.ShapeDtypeStruct((t, d), F32),
        compiler_params=pltpu.CompilerParams(dimension_semantics=("parallel",),
                                             vmem_limit_bytes=VMEM_LIMIT),
        name="combine",
    )(x2, y0, y1, weights, gate, final_gain)


def _moe(x2, gain, scale, shift, gate, wr_t, bias_col, w_gate, w_up, w_down, layer, final_gain, final, seq, tm):
    t, d = x2.shape
    h, experts, weights, ranks, cnt = _route(x2, gain, scale, shift, wr_t, bias_col, seq, tm)
    counts = cnt[:, 0].astype(jnp.int32)
    padded = (counts + MOE_ROWS - 1) // MOE_ROWS * MOE_ROWS
    pad_end = jnp.cumsum(padded)
    pad_start = pad_end - padded
    n_blocks = 2 * t // MOE_ROWS + N_EXPERTS
    n_pad = n_blocks * MOE_ROWS
    starts = jnp.arange(n_blocks, dtype=jnp.int32) * MOE_ROWS
    dest = ranks
    block_expert = jnp.zeros((n_blocks,), jnp.int32)
    for e in range(N_EXPERTS):
        dest = dest + jnp.where(experts == e, pad_start[e], 0)
        block_expert = block_expert + (starts >= pad_end[e]).astype(jnp.int32)
    block_expert = jnp.minimum(block_expert, N_EXPERTS - 1)
    block_valid = (starts < pad_end[-1]).astype(jnp.int32)
    tok = jnp.arange(t, dtype=jnp.int32)
    buf_tok = (jnp.arange(n_pad, dtype=jnp.int32) % t).at[dest.reshape(-1)].set(jnp.concatenate([tok, tok]))
    xb = jnp.take(h, buf_tok, axis=0)
    yb = _experts(xb, w_gate, w_up, w_down, layer, block_expert, block_valid)
    y0 = jnp.take(yb, dest[0], axis=0)
    y1 = jnp.take(yb, dest[1], axis=0)
    return _combine(x2, y0, y1, weights.T, gate, final_gain, final, seq, tm)


def kernel(x, c, norm_gain, w_ada, b_ada, w_in_even, w_in_odd, w_out, a_mu, a_w0, a_w_up, a_a0, a_a_up,
           a_g_up, a_k_k, a_k_a, a_r_k, a_ln_gain, a_ln_bias, b_alpha_up, b_alpha_bias, b_norm_gain, c_lb,
           c_norm_gain, d_conv_w, d_conv_b, d_dt_bias, d_a_log, d_skip, d_norm_gain, w_router, router_bias,
           w_gate, w_up, w_down, final_gain):
    bsz, seq, d = x.shape
    depth = w_ada.shape[0]
    t = bsz * seq
    tm = min(512, seq)
    pair_masks = _chunk_constants()
    ind = _head_indicator(LANE, HEAD_DIM)

    mods = _ada(c, w_ada, b_ada).reshape(depth, bsz, 6, 1, d)
    wr_t = w_router.T
    bias_col = router_bias.reshape(N_EXPERTS, 1)

    x2 = x.reshape(t, d)
    for l in range(depth):
        j = l // 2
        sh_m, sc_m, g_m, sh_f, sc_f, g_f = [mods[l, :, i] for i in range(6)]
        gain_m, gain_f = norm_gain[l, 0].reshape(1, d), norm_gain[l, 1].reshape(1, d)
        if l % 2 == 0:
            w = w_in_even[j]
            zpad = jnp.zeros((d, LANE - GLA_GATE_RANK), w.dtype)
            w = jnp.concatenate([w[:, :RWKV_IN + 1024 + GLA_GATE_RANK], zpad,
                                 w[:, RWKV_IN + 1024 + GLA_GATE_RANK:]], axis=1).astype(BF16)
            u = _in_proj(x2, gain_m, sc_m, sh_m, w, seq, tm)
            alpha_up = jnp.concatenate([b_alpha_up[j], jnp.zeros((LANE - GLA_GATE_RANK, 256), F32)], axis=0)
            p = dict(mu=_row(a_mu[j]), w0=_row(a_w0[j]), w_up=a_w_up[j], a0=_row(a_a0[j]), a_up=a_a_up[j],
                     g_up=a_g_up[j], k_k=_row(a_k_k[j]), k_a=_row(a_k_a[j]), r_k=_row(a_r_k[j]),
                     ln_gain=_row(a_ln_gain[j]), ln_bias=_row(a_ln_bias[j]), alpha_up=alpha_up,
                     alpha_bias=_row(b_alpha_bias[j]), b_norm_gain=_row(b_norm_gain[j]))
            y = _mix_even(u.reshape(bsz, seq, EVEN_COLS), p, (pair_masks, ind))
        else:
            w = w_in_odd[j]
            n_main = w.shape[1] - SSD_HEADS
            w = jnp.concatenate([w[:, :n_main], jnp.repeat(w[:, n_main:], HEAD_DIM, axis=1)], axis=1).astype(BF16)
            u = _in_proj(x2, gain_m, sc_m, sh_m, w, seq, tm)
            p = dict(c_lb=c_lb.astype(F32), c_norm_gain=_row(c_norm_gain[j]), conv_w=d_conv_w[j],
                     conv_b=_row(d_conv_b[j]), dt_bias=_row(jnp.repeat(d_dt_bias[j], HEAD_DIM)),
                     a_log=_row(jnp.repeat(d_a_log[j], HEAD_DIM)), skip=_row(jnp.repeat(d_skip[j], HEAD_DIM)),
                     d_norm_gain=_row(d_norm_gain[j]))
            y = _mix_odd(u.reshape(bsz, seq, ODD_COLS), p, l, pair_masks)
        x2 = _out_proj(y.reshape(t, d), w_out[l].astype(BF16), x2, g_m, seq, tm)
        x2 = _moe(x2, gain_f, sc_f, sh_f, g_f, wr_t, bias_col, w_gate, w_up, w_down, l,
                  final_gain.reshape(1, d), l == depth - 1, seq, tm)
    return x2.reshape(bsz, seq, d)
```

```python
import functools

import numpy as np
import jax
import jax.numpy as jnp
from jax import lax
from jax.experimental import pallas as pl
from jax.experimental.pallas import tpu as pltpu

F32 = jnp.float32
BF16 = jnp.bfloat16
HIGHEST = lax.Precision.HIGHEST

D_MODEL = 1024
MIX_HALF = 512
HEAD_DIM = 64
CHUNK = 64
NORM_EPS = 1e-6
RWKV_HEADS = 8
RWKV_IN = 1792
RWKV_GN_EPS = 64e-5
RWKV_DECAY_SCALE = float(np.exp(-0.5))
GLA_HEADS = 4
GLA_GATE_RANK = 16
GLA_GATE_NORM = 16.0
SSD_HEADS = 8
SSD_GROUPS = 2
SSD_STATE = 64
SSD_CONV = 4
SSD_XBC = 768
N_EXPERTS = 16
N_EXPERT_GROUPS = 4
EXPERTS_PER_GROUP = 4
D_EXPERT = 512
LANE = 128
EVEN_COLS = 3456
ODD_COLS = 3328
MOE_ROWS = 512
MIX_TILE = 128
VMEM_LIMIT = 48 * 1024 * 1024


def _dot(a, b):
    return jnp.dot(a.astype(BF16), b.astype(BF16), preferred_element_type=F32)


def _dot_nt(a, b):
    return lax.dot_general(a.astype(BF16), b.astype(BF16), (((1,), (1,)), ((), ())),
                           preferred_element_type=F32)


def _dot_tn(a, b):
    return lax.dot_general(a.astype(BF16), b.astype(BF16), (((0,), (0,)), ((), ())),
                           preferred_element_type=F32)


def _dot_exact(a, b):
    return jnp.dot(a, b, precision=HIGHEST, preferred_element_type=F32)


def _sigmoid(x):
    return 1.0 / (1.0 + jnp.exp(-x))


def _silu(x):
    return x * _sigmoid(x)


def _softplus(x):
    return jnp.maximum(x, 0.0) + jnp.log(1.0 + jnp.exp(-jnp.abs(x)))


def _log_sigmoid(x):
    return jnp.minimum(x, 0.0) - jnp.log(1.0 + jnp.exp(-jnp.abs(x)))


def _chunk_constants():
    t = np.arange(CHUNK)
    masks = []
    for shift in range(5, -1, -1):
        masks.append(((t[:, None] > t[None, :]) & (((t[:, None] ^ t[None, :]) >> shift) == 1)))
    masks.append(t[:, None] == t[None, :])
    return jnp.asarray(np.stack(masks).astype(np.float32))


def _head_indicator(width, seg):
    i = np.arange(width)
    return jnp.asarray((i[:, None] // seg == i[None, :] // seg).astype(np.float32), dtype=BF16)


def _ada_kernel(c_ref, w_ref, b_ref, o_ref):
    cond = _silu(c_ref[...])
    o_ref[0] = _dot_exact(cond, w_ref[0]) + b_ref[0]


def _ada(c, w_ada, b_ada):
    depth, d, n = w_ada.shape
    bsz = c.shape[0]
    tn = 1536
    return pl.pallas_call(
        _ada_kernel,
        grid=(depth, n // tn),
        in_specs=[pl.BlockSpec((bsz, d), lambda l, j: (0, 0)),
                  pl.BlockSpec((1, d, tn), lambda l, j: (l, 0, j)),
                  pl.BlockSpec((1, 1, tn), lambda l, j: (l, 0, j))],
        out_specs=pl.BlockSpec((1, bsz, tn), lambda l, j: (l, 0, j)),
        out_shape=jax.ShapeDtypeStruct((depth, bsz, n), F32),
        compiler_params=pltpu.CompilerParams(dimension_semantics=("parallel", "parallel"),
                                             vmem_limit_bytes=VMEM_LIMIT),
        name="ada",
    )(c, w_ada, b_ada.reshape(depth, 1, n))


def _modulated_norm(x, gain, scale, shift):
    ms = jnp.mean(x * x, axis=-1, keepdims=True)
    return (x * lax.rsqrt(ms + NORM_EPS)) * gain * (1.0 + scale) + shift


def _in_proj_kernel(x_ref, gain_ref, sc_ref, sh_ref, w_ref, o_ref, *, col_chunk):
    h = _modulated_norm(x_ref[...], gain_ref[...], sc_ref[0], sh_ref[0]).astype(BF16)
    n = o_ref.shape[1]
    for j in range(0, n, col_chunk):
        o_ref[:, j:j + col_chunk] = jnp.dot(h, w_ref[:, j:j + col_chunk],
                                            preferred_element_type=F32).astype(o_ref.dtype)


def _in_proj(x2, gain, scale, shift, w, seq, tm):
    t, d = x2.shape
    n = w.shape[1]
    col_chunk = next(n // parts for parts in (3, 2, 1) if n % (parts * LANE) == 0)
    per_b = seq // tm
    return pl.pallas_call(
        functools.partial(_in_proj_kernel, col_chunk=col_chunk),
        grid=(t // tm,),
        in_specs=[pl.BlockSpec((tm, d), lambda i: (i, 0)),
                  pl.BlockSpec((1, d), lambda i: (0, 0)),
                  pl.BlockSpec((1, 1, d), lambda i: (i // per_b, 0, 0)),
                  pl.BlockSpec((1, 1, d), lambda i: (i // per_b, 0, 0)),
                  pl.BlockSpec((d, n), lambda i: (0, 0))],
        out_specs=pl.BlockSpec((tm, n), lambda i: (i, 0)),
        out_shape=jax.ShapeDtypeStruct((t, n), BF16),
        compiler_params=pltpu.CompilerParams(dimension_semantics=("parallel",),
                                             vmem_limit_bytes=VMEM_LIMIT),
        name="in_proj",
    )(x2, gain, scale, shift, w)


def _cumsum_rows(x):
    rows = lax.broadcasted_iota(jnp.int32, x.shape, 0)
    step = 1
    while step < CHUNK:
        x = x + jnp.where(rows >= step, pltpu.roll(x, step, axis=0), 0.0)
        step *= 2
    return x


def _level_refs(b):
    cols = b.shape[1]
    rows = lax.broadcasted_iota(jnp.int32, b.shape, 0)

    def spread(offset, span):
        pieces = [jnp.broadcast_to(b[s + offset:s + offset + 1], (span, cols)) for s in range(0, CHUNK, span)]
        return pieces[0] if len(pieces) == 1 else jnp.concatenate(pieces, axis=0)

    refs = [spread(n // 2 - 1, n) for n in (64, 32, 16, 8)]
    refs.append(jnp.where((rows & 7) < 4, spread(1, 8), spread(5, 8)))
    refs.append(jnp.where((rows & 1) == 1, pltpu.roll(b, 1, axis=0), b))
    return refs


def _gla_chunk(q, k, v, g, st, pair_masks, heads, dk, dv):
    b = _cumsum_rows(g)
    refs = _level_refs(b)
    b_last = b[CHUNK - 1:CHUNK]
    q_in = q * jnp.exp(b)
    k_st = k * jnp.exp(b_last - b)
    decay = jnp.exp(b_last)
    yield
    scores = [None] * heads
    for lvl in range(7):
        if lvl < 6:
            e = jnp.exp(-jnp.abs(b - refs[lvl]))
            qe, ke = q * e, k * e
        else:
            qe, ke = q, k
        keep = pair_masks[lvl] > 0.5
        for h in range(heads):
            p = _dot_nt(qe[:, h * dk:(h + 1) * dk], ke[:, h * dk:(h + 1) * dk])
            p = jnp.where(keep, p, 0.0)
            scores[h] = p if scores[h] is None else scores[h] + p
        yield
    hs = range(heads)
    ks = [slice(h * dk, (h + 1) * dk) for h in hs]
    vs = [slice(h * dv, (h + 1) * dv) for h in hs]
    o_inter = [_dot_nt(q_in[:, ks[h]], st[h]) for h in hs]
    o_intra = [_dot(scores[h], v[:, vs[h]]) for h in hs]
    yield
    st_new = [st[h] * decay[:, ks[h]] + _dot_tn(v[:, vs[h]], k_st[:, ks[h]]) for h in hs]
    return jnp.concatenate([o_inter[h] + o_intra[h] for h in hs], axis=-1), st_new


def _interleave(*stages):
    results = [None] * len(stages)
    live = list(range(len(stages)))
    while live:
        for i in list(live):
            try:
                next(stages[i])
            except StopIteration as stop:
                results[i] = stop.value
                live.remove(i)
    return results


def _head_rms(o, gain, heads, dv):
    outs = []
    for h in range(heads):
        oh = o[:, h * dv:(h + 1) * dv]
        ms = jnp.mean(oh * oh, axis=-1, keepdims=True)
        outs.append(oh * lax.rsqrt(ms + NORM_EPS) * gain)
    return jnp.concatenate(outs, axis=-1)


def _rwkv_chunk(r, kk, a, kt, v, logw, s0):
    b = _cumsum_rows(logw)
    b_last = b[CHUNK - 1:CHUNK]
    e_neg = jnp.exp(-b)
    e_end = jnp.exp(b_last - b)
    decay = jnp.exp(b_last)
    beta = a * kk
    k_bar = kk * jnp.exp(b - logw)
    r_bar = r * jnp.exp(b)
    beta_t, k_t = beta * e_neg, kt * e_neg
    beta_hat, k_hat = beta * e_end, kt * e_end
    row = lax.broadcasted_iota(jnp.int32, (CHUNK, CHUNK), 0)
    col = lax.broadcasted_iota(jnp.int32, (CHUNK, CHUNK), 1)
    strict = col < row
    incl = col <= row
    same_blk = (row >> 4) == (col >> 4)
    eye = (row == col).astype(F32)
    hs = range(RWKV_HEADS)
    sls = [slice(h * HEAD_DIM, (h + 1) * HEAD_DIM) for h in hs]
    kr = [jnp.concatenate([k_bar[:, sl], r_bar[:, sl]], axis=0) for sl in sls]
    bk = [jnp.concatenate([beta_t[:, sl], k_t[:, sl]], axis=0) for sl in sls]
    vh = [v[:, sl] for sl in sls]
    yield
    m1 = [_dot_nt(kr[h], bk[h]) for h in hs]
    yield
    m2 = [_dot_nt(kr[h], s0[h]) for h in hs]
    yield
    a_m = [jnp.where(strict, m[0:CHUNK, 0:CHUNK], 0.0) for m in m1]
    b_m = [jnp.where(strict, m[0:CHUNK, CHUNK:], 0.0) for m in m1]
    cb_m = [jnp.where(incl, m[CHUNK:, 0:CHUNK], 0.0) for m in m1]
    ck_m = [jnp.where(incl, m[CHUNK:, CHUNK:], 0.0) for m in m1]
    x1 = [jnp.where(same_blk, -am, 0.0) for am in a_m]
    a_off = [jnp.where(same_blk, 0.0, am) for am in a_m]
    rhs = [m2[h][0:CHUNK] + _dot(b_m[h], vh[h]) for h in hs]
    lo, hi = slice(0, CHUNK), slice(CHUNK, 2 * CHUNK)
    side = lambda left, right: jnp.concatenate([left, right], axis=1)
    x2 = [_dot(x, x) for x in x1]
    yield
    p = [eye + x for x in x1]
    w = [_dot(x2[h], side(p[h], x2[h])) for h in hs]
    yield
    p = [p[h] + w[h][:, lo] for h in hs]
    x4 = [w[h][:, hi] for h in hs]
    w = [_dot(x4[h], side(p[h], x4[h])) for h in hs]
    yield
    p = [p[h] + w[h][:, lo] for h in hs]
    t_d = [p[h] + _dot(w[h][:, hi], p[h]) for h in hs]
    yield
    nz = [_dot(t_d[h], side(a_off[h], rhs[h])) for h in hs]
    yield
    w = [_dot(nz[h][:, lo], nz[h]) for h in hs]
    yield
    y1 = [nz[h][:, hi] - w[h][:, hi] for h in hs]
    u = [y1[h] + _dot(w[h][:, lo], y1[h]) for h in hs]
    yield
    outs = [m2[h][CHUNK:] + _dot(side(ck_m[h], -cb_m[h]), jnp.concatenate([vh[h], u[h]], axis=0)) for h in hs]
    yield
    s_new = [s0[h] * decay[:, sls[h]]
             + _dot_tn(jnp.concatenate([vh[h], -u[h]], axis=0),
                       jnp.concatenate([k_hat[:, sls[h]], beta_hat[:, sls[h]]], axis=0)) for h in hs]
    return jnp.concatenate(outs, axis=-1), s_new


def _head_sums(x, ind):
    return jnp.concatenate([_dot(x[:, i:i + LANE], ind) for i in range(0, x.shape[1], LANE)], axis=1)


def _mix_even_kernel(u_ref, mu_ref, w0_ref, wup_ref, a0_ref, aup_ref, gup_ref, kk_ref, ka_ref,
                     rk_ref, lng_ref, lnb_ref, alup_ref, albias_ref, bng_ref, ind_ref,
                     pmask_ref, y_ref, s_ref, g_ref, prev_ref):
    @pl.when(pl.program_id(1) == 0)
    def _():
        s_ref[...] = jnp.zeros_like(s_ref)
        g_ref[...] = jnp.zeros_like(g_ref)
        prev_ref[...] = jnp.zeros_like(prev_ref)

    pair_masks = pmask_ref[...]
    ind = ind_ref[...]
    s_state = [s_ref[h] for h in range(RWKV_HEADS)]
    g_state = [g_ref[h] for h in range(GLA_HEADS)]
    prev = prev_ref[...]
    def rwkv_stages(rs, prev, s_state):
        ua = u_ref[0, rs, 0:RWKV_IN].astype(F32)
        rows = lax.broadcasted_iota(jnp.int32, ua.shape, 0)
        shifted = jnp.where(rows == 0, prev, pltpu.roll(ua, 1, axis=0))
        xa = ua + mu_ref[...] * (shifted - ua)
        r, k, v = xa[:, 0:512], xa[:, 512:1024], xa[:, 1024:1536]
        wd, ad, gd = xa[:, 1536:1600], xa[:, 1600:1664], xa[:, 1664:1792]
        logw = -RWKV_DECAY_SCALE * _sigmoid(w0_ref[...] + _dot(jnp.tanh(wd), wup_ref[...]))
        a = _sigmoid(a0_ref[...] + _dot(ad, aup_ref[...]))
        gate = _dot(_sigmoid(gd), gup_ref[...])
        yield
        kk = k * kk_ref[...]
        kk = kk * lax.rsqrt(_head_sums(kk * kk, ind) + 1e-12)
        kt = k * (1.0 + (a - 1.0) * ka_ref[...])
        yield
        y, s_state = yield from _rwkv_chunk(r, kk, a, kt, v, logw, s_state)
        yield
        mean = _head_sums(y, ind) * (1.0 / HEAD_DIM)
        yc = y - mean
        yield
        var = _head_sums(yc * yc, ind) * (1.0 / HEAD_DIM)
        y = yc * lax.rsqrt(var + RWKV_GN_EPS) * lng_ref[...] + lnb_ref[...]
        y = y + _head_sums(r * kt * rk_ref[...], ind) * v
        y_ref[0, rs, 0:MIX_HALF] = (y * gate).astype(y_ref.dtype)
        return ua[CHUNK - 1:CHUNK], s_state

    def gla_stages(rs, g_state):
        ub = u_ref[0, rs, RWKV_IN:].astype(F32)
        q, kg, vg = ub[:, 0:256] * (HEAD_DIM ** -0.5), ub[:, 256:512], ub[:, 512:1024]
        alpha, gg = ub[:, 1024:1152], ub[:, 1152:1664]
        log_a = _log_sigmoid(_dot(alpha, alup_ref[...]) + albias_ref[...]) * (1.0 / GLA_GATE_NORM)
        yield
        o, g_state = yield from _gla_chunk(q, kg, vg, log_a, g_state, pair_masks, GLA_HEADS, 64, 128)
        o = _head_rms(o, bng_ref[...], GLA_HEADS, 128)
        y_ref[0, rs, MIX_HALF:] = (o * _silu(gg)).astype(y_ref.dtype)
        return g_state

    for c in range(u_ref.shape[1] // CHUNK):
        rs = slice(c * CHUNK, (c + 1) * CHUNK)
        (prev, s_state), g_state = _interleave(rwkv_stages(rs, prev, s_state), gla_stages(rs, g_state))
    for h in range(RWKV_HEADS):
        s_ref[h] = s_state[h]
    for h in range(GLA_HEADS):
        g_ref[h] = g_state[h]
    prev_ref[...] = prev


def _row(p):
    return p.reshape(1, -1).astype(F32)


def _mix_even(u3, p, consts):
    bsz, seq, n = u3.shape
    pair_masks, ind = consts
    small = [p["mu"], p["w0"], p["w_up"], p["a0"], p["a_up"], p["g_up"], p["k_k"], p["k_a"], p["r_k"],
             p["ln_gain"], p["ln_bias"], p["alpha_up"], p["alpha_bias"], p["b_norm_gain"],
             ind, pair_masks]

    def full(arr):
        nd = arr.ndim
        return pl.BlockSpec(arr.shape, lambda b, s, _nd=nd: (0,) * _nd)

    return pl.pallas_call(
        _mix_even_kernel,
        grid=(bsz, seq // MIX_TILE),
        in_specs=[pl.BlockSpec((1, MIX_TILE, n), lambda b, s: (b, s, 0))] + [full(a) for a in small],
        out_specs=pl.BlockSpec((1, MIX_TILE, D_MODEL), lambda b, s: (b, s, 0)),
        out_shape=jax.ShapeDtypeStruct((bsz, seq, D_MODEL), BF16),
        scratch_shapes=[pltpu.VMEM((RWKV_HEADS, HEAD_DIM, HEAD_DIM), F32),
                        pltpu.VMEM((GLA_HEADS, 128, 64), F32),
                        pltpu.VMEM((1, RWKV_IN), F32)],
        compiler_params=pltpu.CompilerParams(dimension_semantics=("parallel", "arbitrary"),
                                             vmem_limit_bytes=VMEM_LIMIT),
        name="mix_even",
    )(u3, *small)


def _mix_odd_kernel(u_ref, clb_ref, cng_ref, convw_ref, convb_ref, dtb_ref, alog_ref, skip_ref,
                    dng_ref, pmask_ref, y_ref, h_ref, d_ref, tail_ref, *, layer):
    @pl.when(pl.program_id(1) == 0)
    def _():
        h_ref[...] = jnp.zeros_like(h_ref)
        d_ref[...] = jnp.zeros_like(d_ref)
        tail_ref[...] = jnp.zeros_like(tail_ref)

    pair_masks = pmask_ref[...]
    c_lb = clb_ref[...]
    c_exp = jnp.exp(c_lb - jnp.max(c_lb, axis=0, keepdims=True))
    lb = jnp.sum(c_exp[1:layer + 1], axis=0, keepdims=True) / jnp.sum(c_exp, axis=0, keepdims=True)
    h_state = [h_ref[h] for h in range(4)]
    d_state = [d_ref[grp] for grp in range(SSD_GROUPS)]
    tail = tail_ref[...]
    def hgrn_stages(rs, h_state):
        u = u_ref[0, rs, 0:1536].astype(F32)
        q, fr, iv, g = u[:, 0:256], u[:, 256:512], u[:, 512:1024], u[:, 1024:1536]
        f = lb + (1.0 - lb) * _sigmoid(fr)
        yield
        o, h_state = yield from _gla_chunk(q, 1.0 - f, iv, jnp.log(f), h_state, pair_masks, 4, 64, 128)
        y_ref[0, rs, 0:MIX_HALF] = (_head_rms(o, cng_ref[...], 4, 128) * _silu(g)).astype(y_ref.dtype)
        return h_state

    def ssd_stages(rs, tail, d_state):
        u = u_ref[0, rs, 1536:].astype(F32)
        y, d_state, tail = yield from _ssd_chunk(u, tail, d_state, convw_ref, convb_ref[...], dtb_ref[...],
                                                 alog_ref[...], skip_ref[...], dng_ref[...])
        y_ref[0, rs, MIX_HALF:] = y.astype(y_ref.dtype)
        return tail, d_state

    for c in range(u_ref.shape[1] // CHUNK):
        rs = slice(c * CHUNK, (c + 1) * CHUNK)
        (h_state,) = _interleave(hgrn_stages(rs, h_state))
        ((tail, d_state),) = _interleave(ssd_stages(rs, tail, d_state))
    for h in range(4):
        h_ref[h] = h_state[h]
    for grp in range(SSD_GROUPS):
        d_ref[grp] = d_state[grp]
    tail_ref[...] = tail


def _ssd_chunk(u, tail, st, convw_ref, conv_b, dt_bias, a_log, skip, norm_gain):
    z, xbc, dt_raw = u[:, 0:512], u[:, 512:1280], u[:, 1280:1792]
    new_tail = xbc[CHUNK - 8:CHUNK]
    rows8 = lax.broadcasted_iota(jnp.int32, (8, SSD_XBC), 0)
    conv = xbc * convw_ref[SSD_CONV - 1:SSD_CONV] + conv_b
    for back in range(1, SSD_CONV):
        rolled = pltpu.roll(xbc, back, axis=0)
        head8 = jnp.where(rows8 < back, pltpu.roll(tail, back, axis=0), rolled[0:8])
        shifted = jnp.concatenate([head8, rolled[8:]], axis=0)
        conv = conv + shifted * convw_ref[SSD_CONV - 1 - back:SSD_CONV - back]
    xbc = _silu(conv)
    yield
    xs, bmat, cmat = xbc[:, 0:512], xbc[:, 512:640], xbc[:, 640:768]
    dt = _softplus(dt_raw + dt_bias)
    da = dt * (-jnp.exp(a_log))
    cum = _cumsum_rows(da)
    yield
    cum_last = cum[CHUNK - 1:CHUNK]
    e_cum = jnp.exp(cum)
    xdt = xs * dt
    x_end = xdt * jnp.exp(cum_last - cum)
    decay = jnp.exp(cum_last)
    yield
    row = lax.broadcasted_iota(jnp.int32, (CHUNK, CHUNK), 0)
    col = lax.broadcasted_iota(jnp.int32, (CHUNK, CHUNK), 1)
    causal = col <= row
    groups = range(SSD_GROUPS)
    heads = range(SSD_HEADS)
    per_group = SSD_HEADS // SSD_GROUPS
    gs = [slice(grp * 256, (grp + 1) * 256) for grp in groups]
    ns = [slice(grp * SSD_STATE, (grp + 1) * SSD_STATE) for grp in groups]
    hs = [slice(h * HEAD_DIM, (h + 1) * HEAD_DIM) for h in heads]
    cb = [_dot_nt(cmat[:, ns[grp]], bmat[:, ns[grp]]) for grp in groups]
    y_off = [_dot(cmat[:, ns[grp]], st[grp]) for grp in groups]
    st_new = [st[grp] * decay[:, gs[grp]] + _dot_tn(bmat[:, ns[grp]], x_end[:, gs[grp]]) for grp in groups]
    yield
    cum_h = [cum[:, sl] for sl in hs]
    seg = [jnp.where(causal, jnp.exp(ch - ch.T), 0.0) for ch in cum_h]
    yield
    y_diag = [_dot(cb[h // per_group] * seg[h], xdt[:, hs[h]]) for h in heads]
    yield
    y = (jnp.concatenate(y_diag, axis=-1) + jnp.concatenate(y_off, axis=-1) * e_cum + skip * xs)
    y = y * _silu(z)
    parts = []
    for grp in range(SSD_GROUPS):
        yg = y[:, grp * 256:(grp + 1) * 256]
        ms = jnp.mean(yg * yg, axis=-1, keepdims=True)
        parts.append(yg * lax.rsqrt(ms + NORM_EPS))
    return jnp.concatenate(parts, axis=-1) * norm_gain, st_new, new_tail


def _mix_odd(u3, p, layer, consts):
    bsz, seq, n = u3.shape
    pair_masks = consts
    small = [p["c_lb"], p["c_norm_gain"], p["conv_w"], p["conv_b"], p["dt_bias"], p["a_log"], p["skip"],
             p["d_norm_gain"], pair_masks]

    def full(arr):
        nd = arr.ndim
        return pl.BlockSpec(arr.shape, lambda b, s, _nd=nd: (0,) * _nd)

    return pl.pallas_call(
        functools.partial(_mix_odd_kernel, layer=layer),
        grid=(bsz, seq // MIX_TILE),
        in_specs=[pl.BlockSpec((1, MIX_TILE, n), lambda b, s: (b, s, 0))] + [full(a) for a in small],
        out_specs=pl.BlockSpec((1, MIX_TILE, D_MODEL), lambda b, s: (b, s, 0)),
        out_shape=jax.ShapeDtypeStruct((bsz, seq, D_MODEL), BF16),
        scratch_shapes=[pltpu.VMEM((4, 128, 64), F32),
                        pltpu.VMEM((SSD_GROUPS, SSD_STATE, 256), F32),
                        pltpu.VMEM((8, SSD_XBC), F32)],
        compiler_params=pltpu.CompilerParams(dimension_semantics=("parallel", "arbitrary"),
                                             vmem_limit_bytes=VMEM_LIMIT),
        name="mix_odd",
    )(u3, *small)


def _out_proj_kernel(y_ref, w_ref, x_ref, g_ref, o_ref):
    o_ref[...] = x_ref[...] + g_ref[0] * jnp.dot(y_ref[...], w_ref[...], preferred_element_type=F32)


def _out_proj(y2, w, x2, gate, seq, tm):
    t, d = x2.shape
    per_b = seq // tm
    return pl.pallas_call(
        _out_proj_kernel,
        grid=(t // tm,),
        in_specs=[pl.BlockSpec((tm, d), lambda i: (i, 0)),
                  pl.BlockSpec((d, d), lambda i: (0, 0)),
                  pl.BlockSpec((tm, d), lambda i: (i, 0)),
                  pl.BlockSpec((1, 1, d), lambda i: (i // per_b, 0, 0))],
        out_specs=pl.BlockSpec((tm, d), lambda i: (i, 0)),
        out_shape=jax.ShapeDtypeStruct((t, d), F32),
        compiler_params=pltpu.CompilerParams(dimension_semantics=("parallel",),
                                             vmem_limit_bytes=VMEM_LIMIT),
        name="out_proj",
    )(y2, w, x2, gate)


def _route_kernel(x_ref, gain_ref, sc_ref, sh_ref, wr_ref, bias_ref, tri_ref, h_ref, e_ref, gt_ref, r_ref,
                  cnt_ref):
    h = _modulated_norm(x_ref[...], gain_ref[...], sc_ref[0], sh_ref[0])
    wr = wr_ref[...]
    wr_hi = wr.astype(BF16)
    wr_lo = (wr - wr_hi.astype(F32)).astype(BF16)
    h_hi = h.astype(BF16)
    h_ref[...] = h_hi
    h_lo = (h - h_hi.astype(F32)).astype(BF16)
    both = _dot_nt(jnp.concatenate([wr_hi, wr_lo], axis=0), h_hi)
    logits = both[0:N_EXPERTS] + both[N_EXPERTS:] + _dot_nt(wr_hi, h_lo)
    score = _sigmoid(logits)
    sel = score + bias_ref[...]
    gscore = []
    for grp in range(N_EXPERT_GROUPS):
        a, b, c, d = [sel[grp * 4 + j:grp * 4 + j + 1] for j in range(4)]
        hi1, lo1, hi2, lo2 = jnp.maximum(a, b), jnp.minimum(a, b), jnp.maximum(c, d), jnp.minimum(c, d)
        gscore.append(jnp.maximum(hi1, hi2) + jnp.maximum(jnp.minimum(hi1, hi2), jnp.maximum(lo1, lo2)))
    best, gidx = gscore[0], jnp.zeros_like(gscore[0], dtype=jnp.int32)
    for grp in range(1, N_EXPERT_GROUPS):
        better = gscore[grp] > best
        gidx = jnp.where(better, grp, gidx)
        best = jnp.where(better, gscore[grp], best)
    vals, raw = [], []
    for j in range(EXPERTS_PER_GROUP):
        vj, rj = sel[j:j + 1], score[j:j + 1]
        for grp in range(1, N_EXPERT_GROUPS):
            vj = jnp.where(gidx == grp, sel[grp * 4 + j:grp * 4 + j + 1], vj)
            rj = jnp.where(gidx == grp, score[grp * 4 + j:grp * 4 + j + 1], rj)
        vals.append(vj)
        raw.append(rj)
    i1, m1, g1 = jnp.zeros_like(gidx), vals[0], raw[0]
    for j in range(1, EXPERTS_PER_GROUP):
        better = vals[j] > m1
        i1 = jnp.where(better, j, i1)
        m1 = jnp.where(better, vals[j], m1)
        g1 = jnp.where(better, raw[j], g1)
    i2, m2, g2 = jnp.zeros_like(gidx), jnp.full_like(m1, -jnp.inf), jnp.zeros_like(m1)
    for j in range(EXPERTS_PER_GROUP):
        better = jnp.logical_and(i1 != j, vals[j] > m2)
        i2 = jnp.where(better, j, i2)
        m2 = jnp.where(better, vals[j], m2)
        g2 = jnp.where(better, raw[j], g2)
    total = g1 + g2
    e1 = gidx * EXPERTS_PER_GROUP + i1
    e2 = gidx * EXPERTS_PER_GROUP + i2
    e_ref[0:1, :] = e1
    e_ref[1:2, :] = e2
    gt_ref[0:1, :] = g1 / total
    gt_ref[1:2, :] = g2 / total
    @pl.when(pl.program_id(0) == 0)
    def _():
        cnt_ref[...] = jnp.zeros_like(cnt_ref)

    eid = lax.broadcasted_iota(jnp.int32, logits.shape, 0)
    is1, is2 = eid == e1, eid == e2
    member = jnp.where(jnp.logical_or(is1, is2), 1.0, 0.0)
    before = jnp.dot(member.astype(BF16), tri_ref[...], preferred_element_type=F32) - member
    base = cnt_ref[:, 0:1] + before
    r_ref[0:1, :] = jnp.sum(jnp.where(is1, base, 0.0), axis=0, keepdims=True).astype(jnp.int32)
    r_ref[1:2, :] = jnp.sum(jnp.where(is2, base, 0.0), axis=0, keepdims=True).astype(jnp.int32)
    cnt_ref[...] = cnt_ref[...] + jnp.sum(member, axis=1, keepdims=True)


def _route(x2, gain, scale, shift, wr_t, bias_col, seq, tm):
    t, d = x2.shape
    per_b = seq // tm
    tri = jnp.asarray(np.triu(np.ones((tm, tm), np.float32)), dtype=BF16)
    return pl.pallas_call(
        _route_kernel,
        grid=(t // tm,),
        in_specs=[pl.BlockSpec((tm, d), lambda i: (i, 0)),
                  pl.BlockSpec((1, d), lambda i: (0, 0)),
                  pl.BlockSpec((1, 1, d), lambda i: (i // per_b, 0, 0)),
                  pl.BlockSpec((1, 1, d), lambda i: (i // per_b, 0, 0)),
                  pl.BlockSpec((N_EXPERTS, d), lambda i: (0, 0)),
                  pl.BlockSpec((N_EXPERTS, 1), lambda i: (0, 0)),
                  pl.BlockSpec((tm, tm), lambda i: (0, 0))],
        out_specs=[pl.BlockSpec((tm, d), lambda i: (i, 0)),
                   pl.BlockSpec((2, tm), lambda i: (0, i)),
                   pl.BlockSpec((2, tm), lambda i: (0, i)),
                   pl.BlockSpec((2, tm), lambda i: (0, i)),
                   pl.BlockSpec((N_EXPERTS, LANE), lambda i: (0, 0))],
        out_shape=[jax.ShapeDtypeStruct((t, d), BF16),
                   jax.ShapeDtypeStruct((2, t), jnp.int32),
                   jax.ShapeDtypeStruct((2, t), F32),
                   jax.ShapeDtypeStruct((2, t), jnp.int32),
                   jax.ShapeDtypeStruct((N_EXPERTS, LANE), F32)],
        compiler_params=pltpu.CompilerParams(dimension_semantics=("arbitrary",),
                                             vmem_limit_bytes=VMEM_LIMIT),
        name="route",
    )(x2, gain, scale, shift, wr_t, bias_col, tri)


def _expert_kernel(be_ref, bv_ref, x_ref, wg_ref, wu_ref, wd_ref, o_ref, wg_s, wu_s, wd_s):
    i = pl.program_id(0)

    @pl.when(jnp.logical_or(i == 0, be_ref[i] != be_ref[jnp.maximum(i - 1, 0)]))
    def _():
        wg_s[...] = wg_ref[0, 0].astype(BF16)
        wu_s[...] = wu_ref[0, 0].astype(BF16)
        wd_s[...] = wd_ref[0, 0].astype(BF16)

    @pl.when(bv_ref[i] > 0)
    def _():
        x = x_ref[...]
        gate = jnp.dot(x, wg_s[...], preferred_element_type=F32)
        up = jnp.dot(x, wu_s[...], preferred_element_type=F32)
        act = (_silu(gate) * up).astype(BF16)
        y = jnp.dot(act, wd_s[...], preferred_element_type=F32)
        o_ref[...] = y.astype(o_ref.dtype)

    @pl.when(bv_ref[i] == 0)
    def _():
        o_ref[...] = jnp.zeros_like(o_ref)


def _experts(xb, w_gate, w_up, w_down, layer, block_expert, block_valid):
    n_pad, d = xb.shape
    n_blocks = n_pad // MOE_ROWS
    de = w_gate.shape[-1]
    grid_spec = pltpu.PrefetchScalarGridSpec(
        num_scalar_prefetch=2,
        grid=(n_blocks,),
        in_specs=[pl.BlockSpec((MOE_ROWS, d), lambda i, be, bv: (i, 0)),
                  pl.BlockSpec((1, 1, d, de), lambda i, be, bv: (layer, be[i], 0, 0)),
                  pl.BlockSpec((1, 1, d, de), lambda i, be, bv: (layer, be[i], 0, 0)),
                  pl.BlockSpec((1, 1, de, d), lambda i, be, bv: (layer, be[i], 0, 0))],
        out_specs=pl.BlockSpec((MOE_ROWS, d), lambda i, be, bv: (i, 0)),
        scratch_shapes=[pltpu.VMEM((d, de), BF16), pltpu.VMEM((d, de), BF16), pltpu.VMEM((de, d), BF16)],
    )
    return pl.pallas_call(
        _expert_kernel,
        grid_spec=grid_spec,
        out_shape=jax.ShapeDtypeStruct((n_pad, d), BF16),
        compiler_params=pltpu.CompilerParams(dimension_semantics=("arbitrary",),
                                             vmem_limit_bytes=VMEM_LIMIT),
        name="experts",
    )(block_expert, block_valid, xb, w_gate, w_up, w_down)


def _combine_kernel(x_ref, y0_ref, y1_ref, w_ref, g_ref, fg_ref, o_ref, *, final):
    w = w_ref[...]
    moe = y0_ref[...].astype(F32) * w[:, 0:1] + y1_ref[...].astype(F32) * w[:, 1:2]
    x = x_ref[...] + g_ref[0] * moe
    if final:
        x = x * lax.rsqrt(jnp.mean(x * x, axis=-1, keepdims=True) + NORM_EPS) * fg_ref[...]
    o_ref[...] = x


def _combine(x2, y0, y1, weights, gate, final_gain, final, seq, tm):
    t, d = x2.shape
    per_b = seq // tm
    return pl.pallas_call(
        functools.partial(_combine_kernel, final=final),
        grid=(t // tm,),
        in_specs=[pl.BlockSpec((tm, d), lambda i: (i, 0)),
                  pl.BlockSpec((tm, d), lambda i: (i, 0)),
                  pl.BlockSpec((tm, d), lambda i: (i, 0)),
                  pl.BlockSpec((tm, 2), lambda i: (i, 0)),
                  pl.BlockSpec((1, 1, d), lambda i: (i // per_b, 0, 0)),
                  pl.BlockSpec((1, d), lambda i: (0, 0))],
        out_specs=pl.BlockSpec((tm, d), lambda i: (i, 0)),
        out_shape=jax.ShapeDtypeStruct((t, d), F32),
        compiler_params=pltpu.CompilerParams(dimension_semantics=("parallel",),
                                             vmem_limit_bytes=VMEM_LIMIT),
        name="combine",
    )(x2, y0, y1, weights, gate, final_gain)


def _moe(x2, gain, scale, shift, gate, wr_t, bias_col, w_gate, w_up, w_down, layer, final_gain, final, seq, tm):
    t, d = x2.shape
    h, experts, weights, ranks, cnt = _route(x2, gain, scale, shift, wr_t, bias_col, seq, tm)
    counts = cnt[:, 0].astype(jnp.int32)
    padded = (counts + MOE_ROWS - 1) // MOE_ROWS * MOE_ROWS
    pad_end = jnp.cumsum(padded)
    pad_start = pad_end - padded
    n_blocks = 2 * t // MOE_ROWS + N_EXPERTS
    n_pad = n_blocks * MOE_ROWS
    starts = jnp.arange(n_blocks, dtype=jnp.int32) * MOE_ROWS
    dest = ranks
    block_expert = jnp.zeros((n_blocks,), jnp.int32)
    for e in range(N_EXPERTS):
        dest = dest + jnp.where(experts == e, pad_start[e], 0)
        block_expert = block_expert + (starts >= pad_end[e]).astype(jnp.int32)
    block_expert = jnp.minimum(block_expert, N_EXPERTS - 1)
    block_valid = (starts < pad_end[-1]).astype(jnp.int32)
    tok = jnp.arange(t, dtype=jnp.int32)
    buf_tok = (jnp.arange(n_pad, dtype=jnp.int32) % t).at[dest.reshape(-1)].set(
        jnp.concatenate([tok, tok]), unique_indices=True, mode="promise_in_bounds")
    xb = jnp.take(h, buf_tok, axis=0, mode="clip")
    yb = _experts(xb, w_gate, w_up, w_down, layer, block_expert, block_valid)
    y0 = jnp.take(yb, dest[0], axis=0, mode="clip", unique_indices=True)
    y1 = jnp.take(yb, dest[1], axis=0, mode="clip", unique_indices=True)
    return _combine(x2, y0, y1, weights.T, gate, final_gain, final, seq, tm)


def kernel(x, c, norm_gain, w_ada, b_ada, w_in_even, w_in_odd, w_out, a_mu, a_w0, a_w_up, a_a0, a_a_up,
           a_g_up, a_k_k, a_k_a, a_r_k, a_ln_gain, a_ln_bias, b_alpha_up, b_alpha_bias, b_norm_gain, c_lb,
           c_norm_gain, d_conv_w, d_conv_b, d_dt_bias, d_a_log, d_skip, d_norm_gain, w_router, router_bias,
           w_gate, w_up, w_down, final_gain):
    bsz, seq, d = x.shape
    depth = w_ada.shape[0]
    t = bsz * seq
    tm = min(512, seq)
    pair_masks = _chunk_constants()
    ind = _head_indicator(LANE, HEAD_DIM)

    mods = _ada(c, w_ada, b_ada).reshape(depth, bsz, 6, 1, d)
    wr_t = w_router.T
    bias_col = router_bias.reshape(N_EXPERTS, 1)

    x2 = x.reshape(t, d)
    for l in range(depth):
        j = l // 2
        sh_m, sc_m, g_m, sh_f, sc_f, g_f = [mods[l, :, i] for i in range(6)]
        gain_m, gain_f = norm_gain[l, 0].reshape(1, d), norm_gain[l, 1].reshape(1, d)
        if l % 2 == 0:
            w = w_in_even[j]
            zpad = jnp.zeros((d, LANE - GLA_GATE_RANK), w.dtype)
            w = jnp.concatenate([w[:, :RWKV_IN + 1024 + GLA_GATE_RANK], zpad,
                                 w[:, RWKV_IN + 1024 + GLA_GATE_RANK:]], axis=1).astype(BF16)
            u = _in_proj(x2, gain_m, sc_m, sh_m, w, seq, tm)
            alpha_up = jnp.concatenate([b_alpha_up[j], jnp.zeros((LANE - GLA_GATE_RANK, 256), F32)], axis=0)
            p = dict(mu=_row(a_mu[j]), w0=_row(a_w0[j]), w_up=a_w_up[j], a0=_row(a_a0[j]), a_up=a_a_up[j],
                     g_up=a_g_up[j], k_k=_row(a_k_k[j]), k_a=_row(a_k_a[j]), r_k=_row(a_r_k[j]),
                     ln_gain=_row(a_ln_gain[j]), ln_bias=_row(a_ln_bias[j]), alpha_up=alpha_up,
                     alpha_bias=_row(b_alpha_bias[j]), b_norm_gain=_row(b_norm_gain[j]))
            y = _mix_even(u.reshape(bsz, seq, EVEN_COLS), p, (pair_masks, ind))
        else:
            w = w_in_odd[j]
            n_main = w.shape[1] - SSD_HEADS
            w = jnp.concatenate([w[:, :n_main], jnp.repeat(w[:, n_main:], HEAD_DIM, axis=1)], axis=1).astype(BF16)
            u = _in_proj(x2, gain_m, sc_m, sh_m, w, seq, tm)
            p = dict(c_lb=c_lb.astype(F32), c_norm_gain=_row(c_norm_gain[j]), conv_w=d_conv_w[j],
                     conv_b=_row(d_conv_b[j]), dt_bias=_row(jnp.repeat(d_dt_bias[j], HEAD_DIM)),
                     a_log=_row(jnp.repeat(d_a_log[j], HEAD_DIM)), skip=_row(jnp.repeat(d_skip[j], HEAD_DIM)),
                     d_norm_gain=_row(d_norm_gain[j]))
            y = _mix_odd(u.reshape(bsz, seq, ODD_COLS), p, l, pair_masks)
        x2 = _out_proj(y.reshape(t, d), w_out[l].astype(BF16), x2, g_m, seq, tm)
        x2 = _moe(x2, gain_f, sc_f, sh_f, g_f, wr_t, bias_col, w_gate, w_up, w_down, l,
                  final_gain.reshape(1, d), l == depth - 1, seq, tm)
    return x2.reshape(bsz, seq, d)
```

```python
import functools

import numpy as np
import jax
import jax.numpy as jnp
from jax import lax
from jax.experimental import pallas as pl
from jax.experimental.pallas import tpu as pltpu

F32 = jnp.float32
BF16 = jnp.bfloat16
HIGHEST = lax.Precision.HIGHEST

D_MODEL = 1024
MIX_HALF = 512
HEAD_DIM = 64
CHUNK = 64
NORM_EPS = 1e-6
RWKV_HEADS = 8
RWKV_IN = 1792
RWKV_GN_EPS = 64e-5
RWKV_DECAY_SCALE = float(np.exp(-0.5))
GLA_HEADS = 4
GLA_GATE_RANK = 16
GLA_GATE_NORM = 16.0
SSD_HEADS = 8
SSD_GROUPS = 2
SSD_STATE = 64
SSD_CONV = 4
SSD_XBC = 768
N_EXPERTS = 16
N_EXPERT_GROUPS = 4
EXPERTS_PER_GROUP = 4
D_EXPERT = 512
LANE = 128
EVEN_COLS = 3456
ODD_COLS = 3328
MOE_ROWS = 512
MIX_TILE = 128
VMEM_LIMIT = 48 * 1024 * 1024


def _dot(a, b):
    return jnp.dot(a.astype(BF16), b.astype(BF16), preferred_element_type=F32)


def _dot_nt(a, b):
    return lax.dot_general(a.astype(BF16), b.astype(BF16), (((1,), (1,)), ((), ())),
                           preferred_element_type=F32)


def _dot_tn(a, b):
    return lax.dot_general(a.astype(BF16), b.astype(BF16), (((0,), (0,)), ((), ())),
                           preferred_element_type=F32)


def _dot_exact(a, b):
    return jnp.dot(a, b, precision=HIGHEST, preferred_element_type=F32)


def _sigmoid(x):
    return 1.0 / (1.0 + jnp.exp(-x))


def _silu(x):
    return x * _sigmoid(x)


def _softplus(x):
    return jnp.maximum(x, 0.0) + jnp.log(1.0 + jnp.exp(-jnp.abs(x)))


def _log_sigmoid(x):
    return jnp.minimum(x, 0.0) - jnp.log(1.0 + jnp.exp(-jnp.abs(x)))


def _chunk_constants():
    t = np.arange(CHUNK)
    masks = []
    for shift in range(5, -1, -1):
        masks.append(((t[:, None] > t[None, :]) & (((t[:, None] ^ t[None, :]) >> shift) == 1)))
    masks.append(t[:, None] == t[None, :])
    return jnp.asarray(np.stack(masks).astype(np.float32))


def _head_indicator(width, seg):
    i = np.arange(width)
    return jnp.asarray((i[:, None] // seg == i[None, :] // seg).astype(np.float32), dtype=BF16)


def _ada_kernel(c_ref, w_ref, b_ref, o_ref):
    cond = _silu(c_ref[...])
    o_ref[0] = _dot_exact(cond, w_ref[0]) + b_ref[0]


def _ada(c, w_ada, b_ada):
    depth, d, n = w_ada.shape
    bsz = c.shape[0]
    tn = 1536
    return pl.pallas_call(
        _ada_kernel,
        grid=(depth, n // tn),
        in_specs=[pl.BlockSpec((bsz, d), lambda l, j: (0, 0)),
                  pl.BlockSpec((1, d, tn), lambda l, j: (l, 0, j)),
                  pl.BlockSpec((1, 1, tn), lambda l, j: (l, 0, j))],
        out_specs=pl.BlockSpec((1, bsz, tn), lambda l, j: (l, 0, j)),
        out_shape=jax.ShapeDtypeStruct((depth, bsz, n), F32),
        compiler_params=pltpu.CompilerParams(dimension_semantics=("parallel", "parallel"),
                                             vmem_limit_bytes=VMEM_LIMIT),
        name="ada",
    )(c, w_ada, b_ada.reshape(depth, 1, n))


def _modulated_norm(x, gain, scale, shift):
    ms = jnp.mean(x * x, axis=-1, keepdims=True)
    return (x * lax.rsqrt(ms + NORM_EPS)) * gain * (1.0 + scale) + shift


def _in_proj_kernel(x_ref, gain_ref, sc_ref, sh_ref, w_ref, o_ref, *, col_chunk):
    h = _modulated_norm(x_ref[...], gain_ref[...], sc_ref[0], sh_ref[0]).astype(BF16)
    n = o_ref.shape[1]
    for j in range(0, n, col_chunk):
        o_ref[:, j:j + col_chunk] = jnp.dot(h, w_ref[:, j:j + col_chunk],
                                            preferred_element_type=F32).astype(o_ref.dtype)


def _in_proj(x2, gain, scale, shift, w, seq, tm):
    t, d = x2.shape
    n = w.shape[1]
    col_chunk = next(n // parts for parts in (3, 2, 1) if n % (parts * LANE) == 0)
    per_b = seq // tm
    return pl.pallas_call(
        functools.partial(_in_proj_kernel, col_chunk=col_chunk),
        grid=(t // tm,),
        in_specs=[pl.BlockSpec((tm, d), lambda i: (i, 0)),
                  pl.BlockSpec((1, d), lambda i: (0, 0)),
                  pl.BlockSpec((1, 1, d), lambda i: (i // per_b, 0, 0)),
                  pl.BlockSpec((1, 1, d), lambda i: (i // per_b, 0, 0)),
                  pl.BlockSpec((d, n), lambda i: (0, 0))],
        out_specs=pl.BlockSpec((tm, n), lambda i: (i, 0)),
        out_shape=jax.ShapeDtypeStruct((t, n), BF16),
        compiler_params=pltpu.CompilerParams(dimension_semantics=("parallel",),
                                             vmem_limit_bytes=VMEM_LIMIT),
        name="in_proj",
    )(x2, gain, scale, shift, w)


def _cumsum_rows(x):
    rows = lax.broadcasted_iota(jnp.int32, x.shape, 0)
    step = 1
    while step < CHUNK:
        x = x + jnp.where(rows >= step, pltpu.roll(x, step, axis=0), 0.0)
        step *= 2
    return x


def _level_refs(b):
    cols = b.shape[1]
    rows = lax.broadcasted_iota(jnp.int32, b.shape, 0)

    def spread(offset, span):
        pieces = [jnp.broadcast_to(b[s + offset:s + offset + 1], (span, cols)) for s in range(0, CHUNK, span)]
        return pieces[0] if len(pieces) == 1 else jnp.concatenate(pieces, axis=0)

    refs = [spread(n // 2 - 1, n) for n in (64, 32, 16, 8)]
    refs.append(jnp.where((rows & 7) < 4, spread(1, 8), spread(5, 8)))
    refs.append(jnp.where((rows & 1) == 1, pltpu.roll(b, 1, axis=0), b))
    return refs


def _gla_chunk(q, k, v, g, st, pair_masks, heads, dk, dv):
    b = _cumsum_rows(g)
    refs = _level_refs(b)
    b_last = b[CHUNK - 1:CHUNK]
    q_in = q * jnp.exp(b)
    k_st = k * jnp.exp(b_last - b)
    decay = jnp.exp(b_last)
    yield
    scores = [None] * heads
    for lvl in range(7):
        if lvl < 6:
            e = jnp.exp(-jnp.abs(b - refs[lvl]))
            qe, ke = q * e, k * e
        else:
            qe, ke = q, k
        keep = pair_masks[lvl] > 0.5
        for h in range(heads):
            p = _dot_nt(qe[:, h * dk:(h + 1) * dk], ke[:, h * dk:(h + 1) * dk])
            p = jnp.where(keep, p, 0.0)
            scores[h] = p if scores[h] is None else scores[h] + p
        yield
    hs = range(heads)
    ks = [slice(h * dk, (h + 1) * dk) for h in hs]
    vs = [slice(h * dv, (h + 1) * dv) for h in hs]
    o_inter = [_dot_nt(q_in[:, ks[h]], st[h]) for h in hs]
    o_intra = [_dot(scores[h], v[:, vs[h]]) for h in hs]
    yield
    st_new = [st[h] * decay[:, ks[h]] + _dot_tn(v[:, vs[h]], k_st[:, ks[h]]) for h in hs]
    return jnp.concatenate([o_inter[h] + o_intra[h] for h in hs], axis=-1), st_new


def _interleave(*stages):
    results = [None] * len(stages)
    live = list(range(len(stages)))
    while live:
        for i in list(live):
            try:
                next(stages[i])
            except StopIteration as stop:
                results[i] = stop.value
                live.remove(i)
    return results


def _head_rms(o, gain, heads, dv):
    outs = []
    for h in range(heads):
        oh = o[:, h * dv:(h + 1) * dv]
        ms = jnp.mean(oh * oh, axis=-1, keepdims=True)
        outs.append(oh * lax.rsqrt(ms + NORM_EPS) * gain)
    return jnp.concatenate(outs, axis=-1)


def _rwkv_chunk(r, kk, a, kt, v, logw, s0):
    b = _cumsum_rows(logw)
    b_last = b[CHUNK - 1:CHUNK]
    e_neg = jnp.exp(-b)
    e_end = jnp.exp(b_last - b)
    decay = jnp.exp(b_last)
    beta = a * kk
    k_bar = kk * jnp.exp(b - logw)
    r_bar = r * jnp.exp(b)
    beta_t, k_t = beta * e_neg, kt * e_neg
    beta_hat, k_hat = beta * e_end, kt * e_end
    row = lax.broadcasted_iota(jnp.int32, (CHUNK, CHUNK), 0)
    col = lax.broadcasted_iota(jnp.int32, (CHUNK, CHUNK), 1)
    strict = col < row
    incl = col <= row
    same_blk = (row >> 4) == (col >> 4)
    eye = (row == col).astype(F32)
    hs = range(RWKV_HEADS)
    sls = [slice(h * HEAD_DIM, (h + 1) * HEAD_DIM) for h in hs]
    kr = [jnp.concatenate([k_bar[:, sl], r_bar[:, sl]], axis=0) for sl in sls]
    bk = [jnp.concatenate([beta_t[:, sl], k_t[:, sl]], axis=0) for sl in sls]
    vh = [v[:, sl] for sl in sls]
    yield
    m1 = [_dot_nt(kr[h], bk[h]) for h in hs]
    yield
    m2 = [_dot_nt(kr[h], s0[h]) for h in hs]
    yield
    a_m = [jnp.where(strict, m[0:CHUNK, 0:CHUNK], 0.0) for m in m1]
    b_m = [jnp.where(strict, m[0:CHUNK, CHUNK:], 0.0) for m in m1]
    cb_m = [jnp.where(incl, m[CHUNK:, 0:CHUNK], 0.0) for m in m1]
    ck_m = [jnp.where(incl, m[CHUNK:, CHUNK:], 0.0) for m in m1]
    x1 = [jnp.where(same_blk, -am, 0.0) for am in a_m]
    a_off = [jnp.where(same_blk, 0.0, am) for am in a_m]
    rhs = [m2[h][0:CHUNK] + _dot(b_m[h], vh[h]) for h in hs]
    lo, hi = slice(0, CHUNK), slice(CHUNK, 2 * CHUNK)
    side = lambda left, right: jnp.concatenate([left, right], axis=1)
    x2 = [_dot(x, x) for x in x1]
    yield
    p = [eye + x for x in x1]
    w = [_dot(x2[h], side(p[h], x2[h])) for h in hs]
    yield
    p = [p[h] + w[h][:, lo] for h in hs]
    x4 = [w[h][:, hi] for h in hs]
    w = [_dot(x4[h], side(p[h], x4[h])) for h in hs]
    yield
    p = [p[h] + w[h][:, lo] for h in hs]
    t_d = [p[h] + _dot(w[h][:, hi], p[h]) for h in hs]
    yield
    nz = [_dot(t_d[h], side(a_off[h], rhs[h])) for h in hs]
    yield
    w = [_dot(nz[h][:, lo], nz[h]) for h in hs]
    yield
    y1 = [nz[h][:, hi] - w[h][:, hi] for h in hs]
    u = [y1[h] + _dot(w[h][:, lo], y1[h]) for h in hs]
    yield
    outs = [m2[h][CHUNK:] + _dot(side(ck_m[h], -cb_m[h]), jnp.concatenate([vh[h], u[h]], axis=0)) for h in hs]
    yield
    s_new = [s0[h] * decay[:, sls[h]]
             + _dot_tn(jnp.concatenate([vh[h], -u[h]], axis=0),
                       jnp.concatenate([k_hat[:, sls[h]], beta_hat[:, sls[h]]], axis=0)) for h in hs]
    return jnp.concatenate(outs, axis=-1), s_new


def _head_sums(x, ind):
    return jnp.concatenate([_dot(x[:, i:i + LANE], ind) for i in range(0, x.shape[1], LANE)], axis=1)


def _mix_even_kernel(u_ref, mu_ref, w0_ref, wup_ref, a0_ref, aup_ref, gup_ref, kk_ref, ka_ref,
                     rk_ref, lng_ref, lnb_ref, alup_ref, albias_ref, bng_ref, ind_ref,
                     pmask_ref, y_ref, s_ref, g_ref, prev_ref):
    @pl.when(pl.program_id(1) == 0)
    def _():
        s_ref[...] = jnp.zeros_like(s_ref)
        g_ref[...] = jnp.zeros_like(g_ref)
        prev_ref[...] = jnp.zeros_like(prev_ref)

    pair_masks = pmask_ref[...]
    ind = ind_ref[...]
    s_state = [s_ref[h] for h in range(RWKV_HEADS)]
    g_state = [g_ref[h] for h in range(GLA_HEADS)]
    prev = prev_ref[...]
    def rwkv_stages(rs, prev, s_state):
        ua = u_ref[0, rs, 0:RWKV_IN].astype(F32)
        rows = lax.broadcasted_iota(jnp.int32, ua.shape, 0)
        shifted = jnp.where(rows == 0, prev, pltpu.roll(ua, 1, axis=0))
        xa = ua + mu_ref[...] * (shifted - ua)
        r, k, v = xa[:, 0:512], xa[:, 512:1024], xa[:, 1024:1536]
        wd, ad, gd = xa[:, 1536:1600], xa[:, 1600:1664], xa[:, 1664:1792]
        logw = -RWKV_DECAY_SCALE * _sigmoid(w0_ref[...] + _dot(jnp.tanh(wd), wup_ref[...]))
        a = _sigmoid(a0_ref[...] + _dot(ad, aup_ref[...]))
        gate = _dot(_sigmoid(gd), gup_ref[...])
        yield
        kk = k * kk_ref[...]
        kk = kk * lax.rsqrt(_head_sums(kk * kk, ind) + 1e-12)
        kt = k * (1.0 + (a - 1.0) * ka_ref[...])
        yield
        y, s_state = yield from _rwkv_chunk(r, kk, a, kt, v, logw, s_state)
        yield
        mean = _head_sums(y, ind) * (1.0 / HEAD_DIM)
        yc = y - mean
        yield
        var = _head_sums(yc * yc, ind) * (1.0 / HEAD_DIM)
        y = yc * lax.rsqrt(var + RWKV_GN_EPS) * lng_ref[...] + lnb_ref[...]
        y = y + _head_sums(r * kt * rk_ref[...], ind) * v
        y_ref[0, rs, 0:MIX_HALF] = (y * gate).astype(y_ref.dtype)
        return ua[CHUNK - 1:CHUNK], s_state

    def gla_stages(rs, g_state):
        ub = u_ref[0, rs, RWKV_IN:].astype(F32)
        q, kg, vg = ub[:, 0:256] * (HEAD_DIM ** -0.5), ub[:, 256:512], ub[:, 512:1024]
        alpha, gg = ub[:, 1024:1152], ub[:, 1152:1664]
        log_a = _log_sigmoid(_dot(alpha, alup_ref[...]) + albias_ref[...]) * (1.0 / GLA_GATE_NORM)
        yield
        o, g_state = yield from _gla_chunk(q, kg, vg, log_a, g_state, pair_masks, GLA_HEADS, 64, 128)
        o = _head_rms(o, bng_ref[...], GLA_HEADS, 128)
        y_ref[0, rs, MIX_HALF:] = (o * _silu(gg)).astype(y_ref.dtype)
        return g_state

    for c in range(u_ref.shape[1] // CHUNK):
        rs = slice(c * CHUNK, (c + 1) * CHUNK)
        (prev, s_state), g_state = _interleave(rwkv_stages(rs, prev, s_state), gla_stages(rs, g_state))
    for h in range(RWKV_HEADS):
        s_ref[h] = s_state[h]
    for h in range(GLA_HEADS):
        g_ref[h] = g_state[h]
    prev_ref[...] = prev


def _row(p):
    return p.reshape(1, -1).astype(F32)


def _mix_even(u3, p, consts):
    bsz, seq, n = u3.shape
    pair_masks, ind = consts
    small = [p["mu"], p["w0"], p["w_up"], p["a0"], p["a_up"], p["g_up"], p["k_k"], p["k_a"], p["r_k"],
             p["ln_gain"], p["ln_bias"], p["alpha_up"], p["alpha_bias"], p["b_norm_gain"],
             ind, pair_masks]

    def full(arr):
        nd = arr.ndim
        return pl.BlockSpec(arr.shape, lambda b, s, _nd=nd: (0,) * _nd)

    return pl.pallas_call(
        _mix_even_kernel,
        grid=(bsz, seq // MIX_TILE),
        in_specs=[pl.BlockSpec((1, MIX_TILE, n), lambda b, s: (b, s, 0))] + [full(a) for a in small],
        out_specs=pl.BlockSpec((1, MIX_TILE, D_MODEL), lambda b, s: (b, s, 0)),
        out_shape=jax.ShapeDtypeStruct((bsz, seq, D_MODEL), BF16),
        scratch_shapes=[pltpu.VMEM((RWKV_HEADS, HEAD_DIM, HEAD_DIM), F32),
                        pltpu.VMEM((GLA_HEADS, 128, 64), F32),
                        pltpu.VMEM((1, RWKV_IN), F32)],
        compiler_params=pltpu.CompilerParams(dimension_semantics=("parallel", "arbitrary"),
                                             vmem_limit_bytes=VMEM_LIMIT),
        name="mix_even",
    )(u3, *small)


def _mix_odd_kernel(u_ref, clb_ref, cng_ref, convw_ref, convb_ref, dtb_ref, alog_ref, skip_ref,
                    dng_ref, pmask_ref, y_ref, h_ref, d_ref, tail_ref, *, layer):
    @pl.when(pl.program_id(1) == 0)
    def _():
        h_ref[...] = jnp.zeros_like(h_ref)
        d_ref[...] = jnp.zeros_like(d_ref)
        tail_ref[...] = jnp.zeros_like(tail_ref)

    pair_masks = pmask_ref[...]
    c_lb = clb_ref[...]
    c_exp = jnp.exp(c_lb - jnp.max(c_lb, axis=0, keepdims=True))
    lb = jnp.sum(c_exp[1:layer + 1], axis=0, keepdims=True) / jnp.sum(c_exp, axis=0, keepdims=True)
    h_state = [h_ref[h] for h in range(4)]
    d_state = [d_ref[grp] for grp in range(SSD_GROUPS)]
    tail = tail_ref[...]
    def hgrn_stages(rs, h_state):
        u = u_ref[0, rs, 0:1536].astype(F32)
        q, fr, iv, g = u[:, 0:256], u[:, 256:512], u[:, 512:1024], u[:, 1024:1536]
        f = lb + (1.0 - lb) * _sigmoid(fr)
        yield
        o, h_state = yield from _gla_chunk(q, 1.0 - f, iv, jnp.log(f), h_state, pair_masks, 4, 64, 128)
        y_ref[0, rs, 0:MIX_HALF] = (_head_rms(o, cng_ref[...], 4, 128) * _silu(g)).astype(y_ref.dtype)
        return h_state

    def ssd_stages(rs, tail, d_state):
        u = u_ref[0, rs, 1536:].astype(F32)
        y, d_state, tail = yield from _ssd_chunk(u, tail, d_state, convw_ref, convb_ref[...], dtb_ref[...],
                                                 alog_ref[...], skip_ref[...], dng_ref[...])
        y_ref[0, rs, MIX_HALF:] = y.astype(y_ref.dtype)
        return tail, d_state

    for c in range(u_ref.shape[1] // CHUNK):
        rs = slice(c * CHUNK, (c + 1) * CHUNK)
        (h_state,) = _interleave(hgrn_stages(rs, h_state))
        ((tail, d_state),) = _interleave(ssd_stages(rs, tail, d_state))
    for h in range(4):
        h_ref[h] = h_state[h]
    for grp in range(SSD_GROUPS):
        d_ref[grp] = d_state[grp]
    tail_ref[...] = tail


def _ssd_chunk(u, tail, st, convw_ref, conv_b, dt_bias, a_log, skip, norm_gain):
    z, xbc, dt_raw = u[:, 0:512], u[:, 512:1280], u[:, 1280:1792]
    new_tail = xbc[CHUNK - 8:CHUNK]
    rows8 = lax.broadcasted_iota(jnp.int32, (8, SSD_XBC), 0)
    conv = xbc * convw_ref[SSD_CONV - 1:SSD_CONV] + conv_b
    for back in range(1, SSD_CONV):
        rolled = pltpu.roll(xbc, back, axis=0)
        head8 = jnp.where(rows8 < back, pltpu.roll(tail, back, axis=0), rolled[0:8])
        shifted = jnp.concatenate([head8, rolled[8:]], axis=0)
        conv = conv + shifted * convw_ref[SSD_CONV - 1 - back:SSD_CONV - back]
    xbc = _silu(conv)
    yield
    xs, bmat, cmat = xbc[:, 0:512], xbc[:, 512:640], xbc[:, 640:768]
    dt = _softplus(dt_raw + dt_bias)
    da = dt * (-jnp.exp(a_log))
    cum = _cumsum_rows(da)
    yield
    cum_last = cum[CHUNK - 1:CHUNK]
    e_cum = jnp.exp(cum)
    xdt = xs * dt
    x_end = xdt * jnp.exp(cum_last - cum)
    decay = jnp.exp(cum_last)
    yield
    row = lax.broadcasted_iota(jnp.int32, (CHUNK, CHUNK), 0)
    col = lax.broadcasted_iota(jnp.int32, (CHUNK, CHUNK), 1)
    causal = col <= row
    groups = range(SSD_GROUPS)
    heads = range(SSD_HEADS)
    per_group = SSD_HEADS // SSD_GROUPS
    gs = [slice(grp * 256, (grp + 1) * 256) for grp in groups]
    ns = [slice(grp * SSD_STATE, (grp + 1) * SSD_STATE) for grp in groups]
    hs = [slice(h * HEAD_DIM, (h + 1) * HEAD_DIM) for h in heads]
    cb = [_dot_nt(cmat[:, ns[grp]], bmat[:, ns[grp]]) for grp in groups]
    y_off = [_dot(cmat[:, ns[grp]], st[grp]) for grp in groups]
    st_new = [st[grp] * decay[:, gs[grp]] + _dot_tn(bmat[:, ns[grp]], x_end[:, gs[grp]]) for grp in groups]
    yield
    cum_h = [cum[:, sl] for sl in hs]
    seg = [jnp.where(causal, jnp.exp(ch - ch.T), 0.0) for ch in cum_h]
    yield
    y_diag = [_dot(cb[h // per_group] * seg[h], xdt[:, hs[h]]) for h in heads]
    yield
    y = (jnp.concatenate(y_diag, axis=-1) + jnp.concatenate(y_off, axis=-1) * e_cum + skip * xs)
    y = y * _silu(z)
    parts = []
    for grp in range(SSD_GROUPS):
        yg = y[:, grp * 256:(grp + 1) * 256]
        ms = jnp.mean(yg * yg, axis=-1, keepdims=True)
        parts.append(yg * lax.rsqrt(ms + NORM_EPS))
    return jnp.concatenate(parts, axis=-1) * norm_gain, st_new, new_tail


def _mix_odd(u3, p, layer, consts):
    bsz, seq, n = u3.shape
    pair_masks = consts
    small = [p["c_lb"], p["c_norm_gain"], p["conv_w"], p["conv_b"], p["dt_bias"], p["a_log"], p["skip"],
             p["d_norm_gain"], pair_masks]

    def full(arr):
        nd = arr.ndim
        return pl.BlockSpec(arr.shape, lambda b, s, _nd=nd: (0,) * _nd)

    return pl.pallas_call(
        functools.partial(_mix_odd_kernel, layer=layer),
        grid=(bsz, seq // MIX_TILE),
        in_specs=[pl.BlockSpec((1, MIX_TILE, n), lambda b, s: (b, s, 0))] + [full(a) for a in small],
        out_specs=pl.BlockSpec((1, MIX_TILE, D_MODEL), lambda b, s: (b, s, 0)),
        out_shape=jax.ShapeDtypeStruct((bsz, seq, D_MODEL), BF16),
        scratch_shapes=[pltpu.VMEM((4, 128, 64), F32),
                        pltpu.VMEM((SSD_GROUPS, SSD_STATE, 256), F32),
                        pltpu.VMEM((8, SSD_XBC), F32)],
        compiler_params=pltpu.CompilerParams(dimension_semantics=("parallel", "arbitrary"),
                                             vmem_limit_bytes=VMEM_LIMIT),
        name="mix_odd",
    )(u3, *small)


def _route_kernel(y_ref, wo_ref, x_ref, gm_ref, gain_ref, sc_ref, sh_ref, wr_ref, bias_ref, tri_ref,
                  xo_ref, h_ref, e_ref, gt_ref, r_ref, cnt_ref):
    x = x_ref[...] + gm_ref[0] * jnp.dot(y_ref[...], wo_ref[...], preferred_element_type=F32)
    xo_ref[...] = x
    h = _modulated_norm(x, gain_ref[...], sc_ref[0], sh_ref[0])
    wr = wr_ref[...]
    wr_hi = wr.astype(BF16)
    wr_lo = (wr - wr_hi.astype(F32)).astype(BF16)
    h_hi = h.astype(BF16)
    h_ref[...] = h_hi
    h_lo = (h - h_hi.astype(F32)).astype(BF16)
    both = _dot_nt(jnp.concatenate([wr_hi, wr_lo], axis=0), h_hi)
    logits = both[0:N_EXPERTS] + both[N_EXPERTS:] + _dot_nt(wr_hi, h_lo)
    score = _sigmoid(logits)
    sel = score + bias_ref[...]
    gscore = []
    for grp in range(N_EXPERT_GROUPS):
        a, b, c, d = [sel[grp * 4 + j:grp * 4 + j + 1] for j in range(4)]
        hi1, lo1, hi2, lo2 = jnp.maximum(a, b), jnp.minimum(a, b), jnp.maximum(c, d), jnp.minimum(c, d)
        gscore.append(jnp.maximum(hi1, hi2) + jnp.maximum(jnp.minimum(hi1, hi2), jnp.maximum(lo1, lo2)))
    best, gidx = gscore[0], jnp.zeros_like(gscore[0], dtype=jnp.int32)
    for grp in range(1, N_EXPERT_GROUPS):
        better = gscore[grp] > best
        gidx = jnp.where(better, grp, gidx)
        best = jnp.where(better, gscore[grp], best)
    vals, raw = [], []
    for j in range(EXPERTS_PER_GROUP):
        vj, rj = sel[j:j + 1], score[j:j + 1]
        for grp in range(1, N_EXPERT_GROUPS):
            vj = jnp.where(gidx == grp, sel[grp * 4 + j:grp * 4 + j + 1], vj)
            rj = jnp.where(gidx == grp, score[grp * 4 + j:grp * 4 + j + 1], rj)
        vals.append(vj)
        raw.append(rj)
    i1, m1, g1 = jnp.zeros_like(gidx), vals[0], raw[0]
    for j in range(1, EXPERTS_PER_GROUP):
        better = vals[j] > m1
        i1 = jnp.where(better, j, i1)
        m1 = jnp.where(better, vals[j], m1)
        g1 = jnp.where(better, raw[j], g1)
    i2, m2, g2 = jnp.zeros_like(gidx), jnp.full_like(m1, -jnp.inf), jnp.zeros_like(m1)
    for j in range(EXPERTS_PER_GROUP):
        better = jnp.logical_and(i1 != j, vals[j] > m2)
        i2 = jnp.where(better, j, i2)
        m2 = jnp.where(better, vals[j], m2)
        g2 = jnp.where(better, raw[j], g2)
    total = g1 + g2
    e1 = gidx * EXPERTS_PER_GROUP + i1
    e2 = gidx * EXPERTS_PER_GROUP + i2
    e_ref[0:1, :] = e1
    e_ref[1:2, :] = e2
    gt_ref[0:1, :] = g1 / total
    gt_ref[1:2, :] = g2 / total
    @pl.when(pl.program_id(0) == 0)
    def _():
        cnt_ref[...] = jnp.zeros_like(cnt_ref)

    eid = lax.broadcasted_iota(jnp.int32, logits.shape, 0)
    is1, is2 = eid == e1, eid == e2
    member = jnp.where(jnp.logical_or(is1, is2), 1.0, 0.0)
    before = jnp.dot(member.astype(BF16), tri_ref[...], preferred_element_type=F32) - member
    base = cnt_ref[:, 0:1] + before
    r_ref[0:1, :] = jnp.sum(jnp.where(is1, base, 0.0), axis=0, keepdims=True).astype(jnp.int32)
    r_ref[1:2, :] = jnp.sum(jnp.where(is2, base, 0.0), axis=0, keepdims=True).astype(jnp.int32)
    cnt_ref[...] = cnt_ref[...] + jnp.sum(member, axis=1, keepdims=True)


def _out_route(y2, w_out, x2, gate_m, gain, scale, shift, wr_t, bias_col, seq, tm):
    t, d = x2.shape
    per_b = seq // tm
    tri = jnp.asarray(np.triu(np.ones((tm, tm), np.float32)), dtype=BF16)
    per_batch = pl.BlockSpec((1, 1, d), lambda i: (i // per_b, 0, 0))
    rows = pl.BlockSpec((tm, d), lambda i: (i, 0))
    pair = pl.BlockSpec((2, tm), lambda i: (0, i))
    return pl.pallas_call(
        _route_kernel,
        grid=(t // tm,),
        in_specs=[rows,
                  pl.BlockSpec((d, d), lambda i: (0, 0)),
                  rows,
                  per_batch,
                  pl.BlockSpec((1, d), lambda i: (0, 0)),
                  per_batch,
                  per_batch,
                  pl.BlockSpec((N_EXPERTS, d), lambda i: (0, 0)),
                  pl.BlockSpec((N_EXPERTS, 1), lambda i: (0, 0)),
                  pl.BlockSpec((tm, tm), lambda i: (0, 0))],
        out_specs=[rows, rows, pair, pair, pair, pl.BlockSpec((N_EXPERTS, LANE), lambda i: (0, 0))],
        out_shape=[jax.ShapeDtypeStruct((t, d), F32),
                   jax.ShapeDtypeStruct((t, d), BF16),
                   jax.ShapeDtypeStruct((2, t), jnp.int32),
                   jax.ShapeDtypeStruct((2, t), F32),
                   jax.ShapeDtypeStruct((2, t), jnp.int32),
                   jax.ShapeDtypeStruct((N_EXPERTS, LANE), F32)],
        compiler_params=pltpu.CompilerParams(dimension_semantics=("arbitrary",),
                                             vmem_limit_bytes=VMEM_LIMIT),
        name="out_route",
    )(y2, w_out, x2, gate_m, gain, scale, shift, wr_t, bias_col, tri)


def _expert_kernel(ib_ref, ie_ref, lo_ref, hi_ref, x_ref, wg_ref, wu_ref, wd_ref, o_ref, wg_s, wu_s, wd_s):
    i = pl.program_id(0)
    blk, lo, hi = ib_ref[i], lo_ref[i], hi_ref[i]

    @pl.when(jnp.logical_or(i == 0, ie_ref[i] != ie_ref[jnp.maximum(i - 1, 0)]))
    def _():
        wg_s[...] = wg_ref[0, 0].astype(BF16)
        wu_s[...] = wu_ref[0, 0].astype(BF16)
        wd_s[...] = wd_ref[0, 0].astype(BF16)

    @pl.when(hi > lo)
    def _():
        x = x_ref[...]
        gate = jnp.dot(x, wg_s[...], preferred_element_type=F32)
        up = jnp.dot(x, wu_s[...], preferred_element_type=F32)
        act = (_silu(gate) * up).astype(BF16)
        y = jnp.dot(act, wd_s[...], preferred_element_type=F32).astype(o_ref.dtype)
        rows = blk * MOE_ROWS + lax.broadcasted_iota(jnp.int32, y.shape, 0)
        mine = jnp.logical_and(rows >= lo, rows < hi)

        @pl.when(lo == blk * MOE_ROWS)
        def _():
            o_ref[...] = jnp.where(mine, y, jnp.zeros_like(y))

        @pl.when(lo != blk * MOE_ROWS)
        def _():
            o_ref[...] = jnp.where(mine, y, o_ref[...])


def _experts(xb, w_gate, w_up, w_down, layer, items):
    n_rows, d = xb.shape
    de = w_gate.shape[-1]
    n_items = items[0].shape[0]
    grid_spec = pltpu.PrefetchScalarGridSpec(
        num_scalar_prefetch=4,
        grid=(n_items,),
        in_specs=[pl.BlockSpec((MOE_ROWS, d), lambda i, ib, ie, lo, hi: (ib[i], 0)),
                  pl.BlockSpec((1, 1, d, de), lambda i, ib, ie, lo, hi: (layer, ie[i], 0, 0)),
                  pl.BlockSpec((1, 1, d, de), lambda i, ib, ie, lo, hi: (layer, ie[i], 0, 0)),
                  pl.BlockSpec((1, 1, de, d), lambda i, ib, ie, lo, hi: (layer, ie[i], 0, 0))],
        out_specs=pl.BlockSpec((MOE_ROWS, d), lambda i, ib, ie, lo, hi: (ib[i], 0)),
        scratch_shapes=[pltpu.VMEM((d, de), BF16), pltpu.VMEM((d, de), BF16), pltpu.VMEM((de, d), BF16)],
    )
    return pl.pallas_call(
        _expert_kernel,
        grid_spec=grid_spec,
        out_shape=jax.ShapeDtypeStruct((n_rows, d), BF16),
        compiler_params=pltpu.CompilerParams(dimension_semantics=("arbitrary",),
                                             vmem_limit_bytes=VMEM_LIMIT),
        name="experts",
    )(*items, xb, w_gate, w_up, w_down)


def _combine_kernel(x_ref, y0_ref, y1_ref, w_ref, g_ref, fg_ref, o_ref, *, final):
    w = w_ref[...]
    moe = y0_ref[...].astype(F32) * w[:, 0:1] + y1_ref[...].astype(F32) * w[:, 1:2]
    x = x_ref[...] + g_ref[0] * moe
    if final:
        x = x * lax.rsqrt(jnp.mean(x * x, axis=-1, keepdims=True) + NORM_EPS) * fg_ref[...]
    o_ref[...] = x


def _combine(x2, y0, y1, weights, gate, final_gain, final, seq, tm):
    t, d = x2.shape
    per_b = seq // tm
    return pl.pallas_call(
        functools.partial(_combine_kernel, final=final),
        grid=(t // tm,),
        in_specs=[pl.BlockSpec((tm, d), lambda i: (i, 0)),
                  pl.BlockSpec((tm, d), lambda i: (i, 0)),
                  pl.BlockSpec((tm, d), lambda i: (i, 0)),
                  pl.BlockSpec((tm, 2), lambda i: (i, 0)),
                  pl.BlockSpec((1, 1, d), lambda i: (i // per_b, 0, 0)),
                  pl.BlockSpec((1, d), lambda i: (0, 0))],
        out_specs=pl.BlockSpec((tm, d), lambda i: (i, 0)),
        out_shape=jax.ShapeDtypeStruct((t, d), F32),
        compiler_params=pltpu.CompilerParams(dimension_semantics=("parallel",),
                                             vmem_limit_bytes=VMEM_LIMIT),
        name="combine",
    )(x2, y0, y1, weights, gate, final_gain)


def _out_moe(y2, w_out, x2, gate_m, gain, scale, shift, gate, wr_t, bias_col, w_gate, w_up, w_down, layer,
             final_gain, final, seq, tm):
    t, d = x2.shape
    x2, h, experts, weights, ranks, cnt = _out_route(y2, w_out, x2, gate_m, gain, scale, shift, wr_t, bias_col,
                                                     seq, tm)
    counts = cnt[:, 0].astype(jnp.int32)
    end = jnp.cumsum(counts)
    start = end - counts
    dest = ranks
    for e in range(N_EXPERTS):
        dest = dest + jnp.where(experts == e, start[e], 0)
    tok = jnp.arange(t, dtype=jnp.uint32)
    keys = dest.astype(jnp.uint32) * jnp.uint32(t) + tok[None, :]
    slot_tok = (jnp.sort(keys.reshape(-1)) % jnp.uint32(t)).astype(jnp.int32)
    xb = jnp.take(h, slot_tok, axis=0, mode="clip")
    n_blocks = 2 * t // MOE_ROWS
    n_items = n_blocks + N_EXPERTS - 1
    first_blk = start // MOE_ROWS
    per_expert = jnp.where(counts > 0, (end - 1) // MOE_ROWS - first_blk + 1, 0)
    item_end = jnp.cumsum(per_expert)
    item_start = item_end - per_expert
    idx = jnp.arange(n_items, dtype=jnp.int32)
    item_e = jnp.zeros((n_items,), jnp.int32)
    for e in range(N_EXPERTS - 1):
        item_e = item_e + (idx >= item_end[e]).astype(jnp.int32)
    pick = lambda table: sum(jnp.where(item_e == e, table[e], 0) for e in range(N_EXPERTS))
    valid = idx < item_end[-1]
    item_blk = jnp.where(valid, pick(first_blk) + idx - pick(item_start), n_blocks - 1)
    item_lo = jnp.where(valid, jnp.maximum(pick(start), item_blk * MOE_ROWS), 0)
    item_hi = jnp.where(valid, jnp.minimum(pick(end), (item_blk + 1) * MOE_ROWS), 0)
    yb = _experts(xb, w_gate, w_up, w_down, layer, (item_blk, item_e, item_lo, item_hi))
    y0 = jnp.take(yb, dest[0], axis=0, mode="clip", unique_indices=True)
    y1 = jnp.take(yb, dest[1], axis=0, mode="clip", unique_indices=True)
    return _combine(x2, y0, y1, weights.T, gate, final_gain, final, seq, tm)


def kernel(x, c, norm_gain, w_ada, b_ada, w_in_even, w_in_odd, w_out, a_mu, a_w0, a_w_up, a_a0, a_a_up,
           a_g_up, a_k_k, a_k_a, a_r_k, a_ln_gain, a_ln_bias, b_alpha_up, b_alpha_bias, b_norm_gain, c_lb,
           c_norm_gain, d_conv_w, d_conv_b, d_dt_bias, d_a_log, d_skip, d_norm_gain, w_router, router_bias,
           w_gate, w_up, w_down, final_gain):
    bsz, seq, d = x.shape
    depth = w_ada.shape[0]
    t = bsz * seq
    tm = min(512, seq)
    pair_masks = _chunk_constants()
    ind = _head_indicator(LANE, HEAD_DIM)

    mods = _ada(c, w_ada, b_ada).reshape(depth, bsz, 6, 1, d)
    wr_t = w_router.T
    bias_col = router_bias.reshape(N_EXPERTS, 1)

    x2 = x.reshape(t, d)
    for l in range(depth):
        j = l // 2
        sh_m, sc_m, g_m, sh_f, sc_f, g_f = [mods[l, :, i] for i in range(6)]
        gain_m, gain_f = norm_gain[l, 0].reshape(1, d), norm_gain[l, 1].reshape(1, d)
        if l % 2 == 0:
            w = w_in_even[j]
            zpad = jnp.zeros((d, LANE - GLA_GATE_RANK), w.dtype)
            w = jnp.concatenate([w[:, :RWKV_IN + 1024 + GLA_GATE_RANK], zpad,
                                 w[:, RWKV_IN + 1024 + GLA_GATE_RANK:]], axis=1).astype(BF16)
            u = _in_proj(x2, gain_m, sc_m, sh_m, w, seq, tm)
            alpha_up = jnp.concatenate([b_alpha_up[j], jnp.zeros((LANE - GLA_GATE_RANK, 256), F32)], axis=0)
            p = dict(mu=_row(a_mu[j]), w0=_row(a_w0[j]), w_up=a_w_up[j], a0=_row(a_a0[j]), a_up=a_a_up[j],
                     g_up=a_g_up[j], k_k=_row(a_k_k[j]), k_a=_row(a_k_a[j]), r_k=_row(a_r_k[j]),
                     ln_gain=_row(a_ln_gain[j]), ln_bias=_row(a_ln_bias[j]), alpha_up=alpha_up,
                     alpha_bias=_row(b_alpha_bias[j]), b_norm_gain=_row(b_norm_gain[j]))
            y = _mix_even(u.reshape(bsz, seq, EVEN_COLS), p, (pair_masks, ind))
        else:
            w = w_in_odd[j]
            n_main = w.shape[1] - SSD_HEADS
            w = jnp.concatenate([w[:, :n_main], jnp.repeat(w[:, n_main:], HEAD_DIM, axis=1)], axis=1).astype(BF16)
            u = _in_proj(x2, gain_m, sc_m, sh_m, w, seq, tm)
            p = dict(c_lb=c_lb.astype(F32), c_norm_gain=_row(c_norm_gain[j]), conv_w=d_conv_w[j],
                     conv_b=_row(d_conv_b[j]), dt_bias=_row(jnp.repeat(d_dt_bias[j], HEAD_DIM)),
                     a_log=_row(jnp.repeat(d_a_log[j], HEAD_DIM)), skip=_row(jnp.repeat(d_skip[j], HEAD_DIM)),
                     d_norm_gain=_row(d_norm_gain[j]))
            y = _mix_odd(u.reshape(bsz, seq, ODD_COLS), p, l, pair_masks)
        x2 = _out_moe(y.reshape(t, d), w_out[l].astype(BF16), x2, g_m, gain_f, sc_f, sh_f, g_f, wr_t, bias_col,
                      w_gate, w_up, w_down, l, final_gain.reshape(1, d), l == depth - 1, seq, tm)
    return x2.reshape(bsz, seq, d)
```

```python
import functools

import numpy as np
import jax
import jax.numpy as jnp
from jax import lax
from jax.experimental import pallas as pl
from jax.experimental.pallas import tpu as pltpu

F32 = jnp.float32
BF16 = jnp.bfloat16
HIGHEST = lax.Precision.HIGHEST

D_MODEL = 1024
MIX_HALF = 512
HEAD_DIM = 64
CHUNK = 64
NORM_EPS = 1e-6
RWKV_HEADS = 8
RWKV_IN = 1792
RWKV_GN_EPS = 64e-5
RWKV_DECAY_SCALE = float(np.exp(-0.5))
LOG2_E = float(np.log2(np.e))
GLA_HEADS = 4
GLA_GATE_RANK = 16
GLA_GATE_NORM = 16.0
SSD_HEADS = 8
SSD_GROUPS = 2
SSD_STATE = 64
SSD_CONV = 4
SSD_XBC = 768
N_EXPERTS = 16
N_EXPERT_GROUPS = 4
EXPERTS_PER_GROUP = 4
D_EXPERT = 512
LANE = 128
EVEN_COLS = 3456
ODD_COLS = 3328
MOE_ROWS = 512
EVEN_TILE, EVEN_ROWS = 64, 2
ODD_TILE, ODD_ROWS = 128, 1
VMEM_LIMIT = 48 * 1024 * 1024


def _dot(a, b):
    return jnp.dot(a.astype(BF16), b.astype(BF16), preferred_element_type=F32)


def _dot_nt(a, b):
    return lax.dot_general(a.astype(BF16), b.astype(BF16), (((1,), (1,)), ((), ())),
                           preferred_element_type=F32)


def _dot_tn(a, b):
    return lax.dot_general(a.astype(BF16), b.astype(BF16), (((0,), (0,)), ((), ())),
                           preferred_element_type=F32)


def _dot_exact(a, b):
    return jnp.dot(a, b, precision=HIGHEST, preferred_element_type=F32)


def _sigmoid(x):
    return 1.0 / (1.0 + jnp.exp(-x))


def _silu(x):
    return x * _sigmoid(x)


def _softplus(x):
    return jnp.maximum(x, 0.0) + jnp.log(1.0 + jnp.exp(-jnp.abs(x)))


def _log_sigmoid(x):
    return jnp.minimum(x, 0.0) - jnp.log(1.0 + jnp.exp(-jnp.abs(x)))


def _chunk_constants():
    t = np.arange(CHUNK)
    masks = []
    for shift in range(5, -1, -1):
        masks.append(((t[:, None] > t[None, :]) & (((t[:, None] ^ t[None, :]) >> shift) == 1)))
    masks.append(t[:, None] == t[None, :])
    return jnp.asarray(np.stack(masks).astype(np.float32))


def _head_indicator(width, seg):
    i = np.arange(width)
    return jnp.asarray((i[:, None] // seg == i[None, :] // seg).astype(np.float32), dtype=BF16)


def _ada_kernel(c_ref, w_ref, b_ref, o_ref):
    cond = _silu(c_ref[...])
    o_ref[0] = _dot_exact(cond, w_ref[0]) + b_ref[0]


def _ada(c, w_ada, b_ada):
    depth, d, n = w_ada.shape
    bsz = c.shape[0]
    tn = 1536
    return pl.pallas_call(
        _ada_kernel,
        grid=(depth, n // tn),
        in_specs=[pl.BlockSpec((bsz, d), lambda l, j: (0, 0)),
                  pl.BlockSpec((1, d, tn), lambda l, j: (l, 0, j)),
                  pl.BlockSpec((1, 1, tn), lambda l, j: (l, 0, j))],
        out_specs=pl.BlockSpec((1, bsz, tn), lambda l, j: (l, 0, j)),
        out_shape=jax.ShapeDtypeStruct((depth, bsz, n), F32),
        compiler_params=pltpu.CompilerParams(dimension_semantics=("parallel", "parallel"),
                                             vmem_limit_bytes=VMEM_LIMIT),
        name="ada",
    )(c, w_ada, b_ada.reshape(depth, 1, n))


def _modulated_norm(x, gain, scale, shift):
    ms = jnp.mean(x * x, axis=-1, keepdims=True)
    return (x * lax.rsqrt(ms + NORM_EPS)) * gain * (1.0 + scale) + shift


def _in_proj_kernel(x_ref, gain_ref, sc_ref, sh_ref, w_ref, o_ref, *, col_chunk):
    h = _modulated_norm(x_ref[...], gain_ref[...], sc_ref[0], sh_ref[0]).astype(BF16)
    n = o_ref.shape[1]
    for j in range(0, n, col_chunk):
        o_ref[:, j:j + col_chunk] = jnp.dot(h, w_ref[:, j:j + col_chunk],
                                            preferred_element_type=F32).astype(o_ref.dtype)


def _in_proj(x2, gain, scale, shift, w, seq, tm):
    t, d = x2.shape
    n = w.shape[1]
    col_chunk = next(n // parts for parts in (3, 2, 1) if n % (parts * LANE) == 0)
    per_b = seq // tm
    return pl.pallas_call(
        functools.partial(_in_proj_kernel, col_chunk=col_chunk),
        grid=(t // tm,),
        in_specs=[pl.BlockSpec((tm, d), lambda i: (i, 0)),
                  pl.BlockSpec((1, d), lambda i: (0, 0)),
                  pl.BlockSpec((1, 1, d), lambda i: (i // per_b, 0, 0)),
                  pl.BlockSpec((1, 1, d), lambda i: (i // per_b, 0, 0)),
                  pl.BlockSpec((d, n), lambda i: (0, 0))],
        out_specs=pl.BlockSpec((tm, n), lambda i: (i, 0)),
        out_shape=jax.ShapeDtypeStruct((t, n), BF16),
        compiler_params=pltpu.CompilerParams(dimension_semantics=("parallel",),
                                             vmem_limit_bytes=VMEM_LIMIT),
        name="in_proj",
    )(x2, gain, scale, shift, w)


def _cumsum_rows(x, on_mxu):
    if not on_mxu:
        rows = lax.broadcasted_iota(jnp.int32, x.shape, 0)
        step = 1
        while step < CHUNK:
            x = x + jnp.where(rows >= step, pltpu.roll(x, step, axis=0), 0.0)
            step *= 2
        return x
    n = x.shape[1]
    hi = x.astype(BF16)
    lo = (x - hi.astype(F32)).astype(BF16)
    row = lax.broadcasted_iota(jnp.int32, (CHUNK, CHUNK), 0)
    col = lax.broadcasted_iota(jnp.int32, (CHUNK, CHUNK), 1)
    tril = jnp.where(col <= row, 1.0, 0.0).astype(BF16)
    both = jnp.dot(tril, jnp.concatenate([hi, lo], axis=1), preferred_element_type=F32)
    return both[:, 0:n] + both[:, n:]


def _level_refs(b):
    cols = b.shape[1]
    rows = lax.broadcasted_iota(jnp.int32, b.shape, 0)

    def spread(offset, span):
        pieces = [jnp.broadcast_to(b[s + offset:s + offset + 1], (span, cols)) for s in range(0, CHUNK, span)]
        return pieces[0] if len(pieces) == 1 else jnp.concatenate(pieces, axis=0)

    refs = [spread(n // 2 - 1, n) for n in (64, 32, 16, 8)]
    refs.append(jnp.where((rows & 7) < 4, spread(1, 8), spread(5, 8)))
    refs.append(jnp.where((rows & 1) == 1, pltpu.roll(b, 1, axis=0), b))
    return refs


def _gla_chunk(q, k, v, g, st, pair_masks, heads, dk, dv, cumsum_on_mxu):
    hs = range(heads)
    ks = [slice(h * dk, (h + 1) * dk) for h in hs]
    vs = [slice(h * dv, (h + 1) * dv) for h in hs]
    b = _cumsum_rows(g, cumsum_on_mxu)
    b_last = b[CHUNK - 1:CHUNK]
    q_in = q * jnp.exp2(b)
    k_st = k * jnp.exp2(b_last - b)
    decay = jnp.exp2(b_last)
    yield

    refs = _level_refs(b)
    scores = [None] * heads
    for lvl in range(7):
        if lvl < 6:
            e = jnp.exp2(-jnp.abs(b - refs[lvl]))
            qe, ke = q * e, k * e
        else:
            qe, ke = q, k
        keep = pair_masks[lvl] > 0.5
        for h in hs:
            p = _dot_nt(qe[:, ks[h]], ke[:, ks[h]])
            scores[h] = jnp.where(keep, p, 0.0) if scores[h] is None else jnp.where(keep, p, scores[h])
        yield
    o_inter = [_dot_nt(q_in[:, ks[h]], st[h]) for h in hs]
    o_intra = [_dot(scores[h], v[:, vs[h]]) for h in hs]
    yield
    st_new = [st[h] * decay[:, ks[h]] + _dot_tn(v[:, vs[h]], k_st[:, ks[h]]) for h in hs]
    return jnp.concatenate([o_inter[h] + o_intra[h] for h in hs], axis=-1), st_new


def _interleave(*stages):
    results = [None] * len(stages)
    live = list(range(len(stages)))
    while live:
        for i in list(live):
            try:
                next(stages[i])
            except StopIteration as stop:
                results[i] = stop.value
                live.remove(i)
    return results


def _head_rms(o, gain, heads, dv):
    outs = []
    for h in range(heads):
        oh = o[:, h * dv:(h + 1) * dv]
        ms = jnp.mean(oh * oh, axis=-1, keepdims=True)
        outs.append(oh * lax.rsqrt(ms + NORM_EPS) * gain)
    return jnp.concatenate(outs, axis=-1)


def _rwkv_chunk(r, kk, a, kt, v, logw, s0):
    b = _cumsum_rows(logw, True)
    b_last = b[CHUNK - 1:CHUNK]
    e_neg = jnp.exp2(-b)
    e_end = jnp.exp2(b_last - b)
    decay = jnp.exp2(b_last)
    beta = a * kk
    k_bar = kk * jnp.exp2(b - logw)
    r_bar = r * jnp.exp2(b)
    beta_t, k_t = beta * e_neg, kt * e_neg
    beta_hat, k_hat = beta * e_end, kt * e_end
    row = lax.broadcasted_iota(jnp.int32, (CHUNK, CHUNK), 0)
    col = lax.broadcasted_iota(jnp.int32, (CHUNK, CHUNK), 1)
    strict = col < row
    incl = col <= row
    same_blk = (row >> 4) == (col >> 4)
    eye = (row == col).astype(F32)
    hs = range(RWKV_HEADS)
    sls = [slice(h * HEAD_DIM, (h + 1) * HEAD_DIM) for h in hs]
    kr = [jnp.concatenate([k_bar[:, sl], r_bar[:, sl]], axis=0) for sl in sls]
    bk = [jnp.concatenate([beta_t[:, sl], k_t[:, sl]], axis=0) for sl in sls]
    vh = [v[:, sl] for sl in sls]
    yield
    m1 = [_dot_nt(kr[h], bk[h]) for h in hs]
    yield
    m2 = [_dot_nt(kr[h], s0[h]) for h in hs]
    yield
    a_m = [jnp.where(strict, m[0:CHUNK, 0:CHUNK], 0.0) for m in m1]
    b_m = [jnp.where(strict, m[0:CHUNK, CHUNK:], 0.0) for m in m1]
    cb_m = [jnp.where(incl, m[CHUNK:, 0:CHUNK], 0.0) for m in m1]
    ck_m = [jnp.where(incl, m[CHUNK:, CHUNK:], 0.0) for m in m1]
    x1 = [jnp.where(same_blk, -am, 0.0) for am in a_m]
    a_off = [jnp.where(same_blk, 0.0, am) for am in a_m]
    rhs = [m2[h][0:CHUNK] + _dot(b_m[h], vh[h]) for h in hs]
    lo, hi = slice(0, CHUNK), slice(CHUNK, 2 * CHUNK)
    side = lambda left, right: jnp.concatenate([left, right], axis=1)
    x2 = [_dot(x, x) for x in x1]
    yield
    p = [eye + x for x in x1]
    w = [_dot(x2[h], side(p[h], x2[h])) for h in hs]
    yield
    p = [p[h] + w[h][:, lo] for h in hs]
    x4 = [w[h][:, hi] for h in hs]
    w = [_dot(x4[h], side(p[h], x4[h])) for h in hs]
    yield
    p = [p[h] + w[h][:, lo] for h in hs]
    t_d = [p[h] + _dot(w[h][:, hi], p[h]) for h in hs]
    yield
    nz = [_dot(t_d[h], side(a_off[h], rhs[h])) for h in hs]
    yield
    w = [_dot(nz[h][:, lo], nz[h]) for h in hs]
    yield
    y1 = [nz[h][:, hi] - w[h][:, hi] for h in hs]
    u = [y1[h] + _dot(w[h][:, lo], y1[h]) for h in hs]
    yield
    outs = [m2[h][CHUNK:] + _dot(side(ck_m[h], -cb_m[h]), jnp.concatenate([vh[h], u[h]], axis=0)) for h in hs]
    yield
    s_new = [s0[h] * decay[:, sls[h]]
             + _dot_tn(jnp.concatenate([vh[h], -u[h]], axis=0),
                       jnp.concatenate([k_hat[:, sls[h]], beta_hat[:, sls[h]]], axis=0)) for h in hs]
    return jnp.concatenate(outs, axis=-1), s_new


def _head_sums(x, ind):
    return jnp.concatenate([_dot(x[:, i:i + LANE], ind) for i in range(0, x.shape[1], LANE)], axis=1)


def _mix_even_kernel(u_ref, mu_ref, w0_ref, wup_ref, a0_ref, aup_ref, gup_ref, kk_ref, ka_ref,
                     rk_ref, lng_ref, lnb_ref, alup_ref, albias_ref, bng_ref, ind_ref,
                     pmask_ref, y_ref, s_ref, g_ref, prev_ref):
    @pl.when(pl.program_id(1) == 0)
    def _():
        s_ref[...] = jnp.zeros_like(s_ref)
        g_ref[...] = jnp.zeros_like(g_ref)
        prev_ref[...] = jnp.zeros_like(prev_ref)

    pair_masks = pmask_ref[...]
    ind = ind_ref[...]
    n_rows = u_ref.shape[0]
    s_state = [[s_ref[b * RWKV_HEADS + h] for h in range(RWKV_HEADS)] for b in range(n_rows)]
    g_state = [[g_ref[b * GLA_HEADS + h] for h in range(GLA_HEADS)] for b in range(n_rows)]
    prev = [prev_ref[b] for b in range(n_rows)]

    def rwkv_stages(b, rs, prev, s_state):
        ua = u_ref[b, rs, 0:RWKV_IN].astype(F32)
        rows = lax.broadcasted_iota(jnp.int32, ua.shape, 0)
        shifted = jnp.where(rows == 0, prev, pltpu.roll(ua, 1, axis=0))
        xa = ua + mu_ref[...] * (shifted - ua)
        r, k, v = xa[:, 0:512], xa[:, 512:1024], xa[:, 1024:1536]
        wd, ad, gd = xa[:, 1536:1600], xa[:, 1600:1664], xa[:, 1664:1792]
        logw = (-RWKV_DECAY_SCALE * LOG2_E) * _sigmoid(w0_ref[...] + _dot(jnp.tanh(wd), wup_ref[...]))
        a = _sigmoid(a0_ref[...] + _dot(ad, aup_ref[...]))
        gate = _dot(_sigmoid(gd), gup_ref[...])
        yield
        kk = k * kk_ref[...]
        kk = kk * lax.rsqrt(_head_sums(kk * kk, ind) + 1e-12)
        kt = k * (1.0 + (a - 1.0) * ka_ref[...])
        yield
        y, s_state = yield from _rwkv_chunk(r, kk, a, kt, v, logw, s_state)
        yield
        mean = _head_sums(y, ind) * (1.0 / HEAD_DIM)
        yc = y - mean
        yield
        var = _head_sums(yc * yc, ind) * (1.0 / HEAD_DIM)
        y = yc * lax.rsqrt(var + RWKV_GN_EPS) * lng_ref[...] + lnb_ref[...]
        y = y + _head_sums(r * kt * rk_ref[...], ind) * v
        y_ref[b, rs, 0:MIX_HALF] = (y * gate).astype(y_ref.dtype)
        return ua[CHUNK - 1:CHUNK], s_state

    def gla_stages(b, rs, g_state):
        ub = u_ref[b, rs, RWKV_IN:].astype(F32)
        q, kg, vg = ub[:, 0:256] * (HEAD_DIM ** -0.5), ub[:, 256:512], ub[:, 512:1024]
        alpha, gg = ub[:, 1024:1152], ub[:, 1152:1664]
        log_a = _log_sigmoid(_dot(alpha, alup_ref[...]) + albias_ref[...]) * (LOG2_E / GLA_GATE_NORM)
        yield
        o, g_state = yield from _gla_chunk(q, kg, vg, log_a, g_state, pair_masks, GLA_HEADS, 64, 128, True)
        o = _head_rms(o, bng_ref[...], GLA_HEADS, 128)
        y_ref[b, rs, MIX_HALF:] = (o * _silu(gg)).astype(y_ref.dtype)
        return g_state

    for c in range(u_ref.shape[1] // CHUNK):
        rs = slice(c * CHUNK, (c + 1) * CHUNK)
        done = _interleave(*[stages for b in range(n_rows)
                             for stages in (rwkv_stages(b, rs, prev[b], s_state[b]), gla_stages(b, rs, g_state[b]))])
        for b in range(n_rows):
            (prev[b], s_state[b]), g_state[b] = done[2 * b], done[2 * b + 1]
    for b in range(n_rows):
        for h in range(RWKV_HEADS):
            s_ref[b * RWKV_HEADS + h] = s_state[b][h]
        for h in range(GLA_HEADS):
            g_ref[b * GLA_HEADS + h] = g_state[b][h]
        prev_ref[b] = prev[b]


def _row(p):
    return p.reshape(1, -1).astype(F32)


def _mix_even(u3, p, consts):
    bsz, seq, n = u3.shape
    pair_masks, ind = consts
    small = [p["mu"], p["w0"], p["w_up"], p["a0"], p["a_up"], p["g_up"], p["k_k"], p["k_a"], p["r_k"],
             p["ln_gain"], p["ln_bias"], p["alpha_up"], p["alpha_bias"], p["b_norm_gain"],
             ind, pair_masks]

    def full(arr):
        nd = arr.ndim
        return pl.BlockSpec(arr.shape, lambda b, s, _nd=nd: (0,) * _nd)

    rows = EVEN_ROWS if bsz % EVEN_ROWS == 0 else 1
    return pl.pallas_call(
        _mix_even_kernel,
        grid=(bsz // rows, seq // EVEN_TILE),
        in_specs=[pl.BlockSpec((rows, EVEN_TILE, n), lambda b, s: (b, s, 0))] + [full(a) for a in small],
        out_specs=pl.BlockSpec((rows, EVEN_TILE, D_MODEL), lambda b, s: (b, s, 0)),
        out_shape=jax.ShapeDtypeStruct((bsz, seq, D_MODEL), BF16),
        scratch_shapes=[pltpu.VMEM((rows * RWKV_HEADS, HEAD_DIM, HEAD_DIM), F32),
                        pltpu.VMEM((rows * GLA_HEADS, 128, 64), F32),
                        pltpu.VMEM((rows, 1, RWKV_IN), F32)],
        compiler_params=pltpu.CompilerParams(dimension_semantics=("parallel", "arbitrary"),
                                             vmem_limit_bytes=VMEM_LIMIT),
        name="mix_even",
    )(u3, *small)


def _mix_odd_kernel(u_ref, clb_ref, cng_ref, convw_ref, convb_ref, dtb_ref, alog_ref, skip_ref,
                    dng_ref, pmask_ref, y_ref, h_ref, d_ref, tail_ref, *, layer):
    @pl.when(pl.program_id(1) == 0)
    def _():
        h_ref[...] = jnp.zeros_like(h_ref)
        d_ref[...] = jnp.zeros_like(d_ref)
        tail_ref[...] = jnp.zeros_like(tail_ref)

    pair_masks = pmask_ref[...]
    c_lb = clb_ref[...]
    c_exp = jnp.exp(c_lb - jnp.max(c_lb, axis=0, keepdims=True))
    lb = jnp.sum(c_exp[1:layer + 1], axis=0, keepdims=True) / jnp.sum(c_exp, axis=0, keepdims=True)
    n_rows = u_ref.shape[0]
    h_state = [[h_ref[b * 4 + h] for h in range(4)] for b in range(n_rows)]
    d_state = [[d_ref[b * SSD_GROUPS + grp] for grp in range(SSD_GROUPS)] for b in range(n_rows)]
    tail = [tail_ref[b] for b in range(n_rows)]

    def hgrn_stages(b, rs, h_state):
        u = u_ref[b, rs, 0:1536].astype(F32)
        q, fr, iv, g = u[:, 0:256], u[:, 256:512], u[:, 512:1024], u[:, 1024:1536]
        f = lb + (1.0 - lb) * _sigmoid(fr)
        yield
        o, h_state = yield from _gla_chunk(q, 1.0 - f, iv, jnp.log2(f), h_state, pair_masks, 4, 64, 128, False)
        y_ref[b, rs, 0:MIX_HALF] = (_head_rms(o, cng_ref[...], 4, 128) * _silu(g)).astype(y_ref.dtype)
        return h_state

    def ssd_stages(b, rs, tail, d_state):
        u = u_ref[b, rs, 1536:].astype(F32)
        y, d_state, tail = yield from _ssd_chunk(u, tail, d_state, convw_ref, convb_ref[...], dtb_ref[...],
                                                 alog_ref[...], skip_ref[...], dng_ref[...])
        y_ref[b, rs, MIX_HALF:] = y.astype(y_ref.dtype)
        return tail, d_state

    for c in range(u_ref.shape[1] // CHUNK):
        rs = slice(c * CHUNK, (c + 1) * CHUNK)
        h_state = _interleave(*[hgrn_stages(b, rs, h_state[b]) for b in range(n_rows)])
        done = _interleave(*[ssd_stages(b, rs, tail[b], d_state[b]) for b in range(n_rows)])
        for b in range(n_rows):
            tail[b], d_state[b] = done[b]
    for b in range(n_rows):
        for h in range(4):
            h_ref[b * 4 + h] = h_state[b][h]
        for grp in range(SSD_GROUPS):
            d_ref[b * SSD_GROUPS + grp] = d_state[b][grp]
        tail_ref[b] = tail[b]


def _ssd_chunk(u, tail, st, convw_ref, conv_b, dt_bias, a_log, skip, norm_gain):
    z, xbc, dt_raw = u[:, 0:512], u[:, 512:1280], u[:, 1280:1792]
    new_tail = xbc[CHUNK - 8:CHUNK]
    rows8 = lax.broadcasted_iota(jnp.int32, (8, SSD_XBC), 0)
    conv = xbc * convw_ref[SSD_CONV - 1:SSD_CONV] + conv_b
    for back in range(1, SSD_CONV):
        rolled = pltpu.roll(xbc, back, axis=0)
        head8 = jnp.where(rows8 < back, pltpu.roll(tail, back, axis=0), rolled[0:8])
        shifted = jnp.concatenate([head8, rolled[8:]], axis=0)
        conv = conv + shifted * convw_ref[SSD_CONV - 1 - back:SSD_CONV - back]
    xbc = _silu(conv)
    yield
    xs, bmat, cmat = xbc[:, 0:512], xbc[:, 512:640], xbc[:, 640:768]
    dt = _softplus(dt_raw + dt_bias)
    da = dt * (-LOG2_E * jnp.exp(a_log))
    cum = _cumsum_rows(da, False)
    yield
    cum_last = cum[CHUNK - 1:CHUNK]
    e_cum = jnp.exp2(cum)
    xdt = xs * dt
    x_end = xdt * jnp.exp2(cum_last - cum)
    decay = jnp.exp2(cum_last)
    yield
    row = lax.broadcasted_iota(jnp.int32, (CHUNK, CHUNK), 0)
    col = lax.broadcasted_iota(jnp.int32, (CHUNK, CHUNK), 1)
    causal = col <= row
    groups = range(SSD_GROUPS)
    heads = range(SSD_HEADS)
    per_group = SSD_HEADS // SSD_GROUPS
    gs = [slice(grp * 256, (grp + 1) * 256) for grp in groups]
    ns = [slice(grp * SSD_STATE, (grp + 1) * SSD_STATE) for grp in groups]
    hs = [slice(h * HEAD_DIM, (h + 1) * HEAD_DIM) for h in heads]
    cb = [_dot_nt(cmat[:, ns[grp]], bmat[:, ns[grp]]) for grp in groups]
    y_off = [_dot(cmat[:, ns[grp]], st[grp]) for grp in groups]
    st_new = [st[grp] * decay[:, gs[grp]] + _dot_tn(bmat[:, ns[grp]], x_end[:, gs[grp]]) for grp in groups]
    yield
    cum_h = [cum[:, sl] for sl in hs]
    seg = [jnp.where(causal, jnp.exp2(ch - ch.T), 0.0) for ch in cum_h]
    yield
    y_diag = [_dot(cb[h // per_group] * seg[h], xdt[:, hs[h]]) for h in heads]
    yield
    y = (jnp.concatenate(y_diag, axis=-1) + jnp.concatenate(y_off, axis=-1) * e_cum + skip * xs)
    y = y * _silu(z)
    parts = []
    for grp in range(SSD_GROUPS):
        yg = y[:, grp * 256:(grp + 1) * 256]
        ms = jnp.mean(yg * yg, axis=-1, keepdims=True)
        parts.append(yg * lax.rsqrt(ms + NORM_EPS))
    return jnp.concatenate(parts, axis=-1) * norm_gain, st_new, new_tail


def _mix_odd(u3, p, layer, consts):
    bsz, seq, n = u3.shape
    pair_masks = consts
    small = [p["c_lb"], p["c_norm_gain"], p["conv_w"], p["conv_b"], p["dt_bias"], p["a_log"], p["skip"],
             p["d_norm_gain"], pair_masks]

    def full(arr):
        nd = arr.ndim
        return pl.BlockSpec(arr.shape, lambda b, s, _nd=nd: (0,) * _nd)

    rows = ODD_ROWS if bsz % ODD_ROWS == 0 else 1
    return pl.pallas_call(
        functools.partial(_mix_odd_kernel, layer=layer),
        grid=(bsz // rows, seq // ODD_TILE),
        in_specs=[pl.BlockSpec((rows, ODD_TILE, n), lambda b, s: (b, s, 0))] + [full(a) for a in small],
        out_specs=pl.BlockSpec((rows, ODD_TILE, D_MODEL), lambda b, s: (b, s, 0)),
        out_shape=jax.ShapeDtypeStruct((bsz, seq, D_MODEL), BF16),
        scratch_shapes=[pltpu.VMEM((rows * 4, 128, 64), F32),
                        pltpu.VMEM((rows * SSD_GROUPS, SSD_STATE, 256), F32),
                        pltpu.VMEM((rows, 8, SSD_XBC), F32)],
        compiler_params=pltpu.CompilerParams(dimension_semantics=("parallel", "arbitrary"),
                                             vmem_limit_bytes=VMEM_LIMIT),
        name="mix_odd",
    )(u3, *small)


def _route_kernel(y_ref, wo_ref, x_ref, gm_ref, gain_ref, sc_ref, sh_ref, wr_ref, bias_ref, tri_ref,
                  xo_ref, h_ref, e_ref, gt_ref, r_ref, cnt_ref):
    x = x_ref[...] + gm_ref[0] * jnp.dot(y_ref[...], wo_ref[...], preferred_element_type=F32)
    xo_ref[...] = x
    h = _modulated_norm(x, gain_ref[...], sc_ref[0], sh_ref[0])
    wr = wr_ref[...]
    wr_hi = wr.astype(BF16)
    wr_lo = (wr - wr_hi.astype(F32)).astype(BF16)
    h_hi = h.astype(BF16)
    h_ref[...] = h_hi
    h_lo = (h - h_hi.astype(F32)).astype(BF16)
    both = _dot_nt(jnp.concatenate([wr_hi, wr_lo], axis=0), h_hi)
    logits = both[0:N_EXPERTS] + both[N_EXPERTS:] + _dot_nt(wr_hi, h_lo)
    score = _sigmoid(logits)
    sel = score + bias_ref[...]
    gscore = []
    for grp in range(N_EXPERT_GROUPS):
        a, b, c, d = [sel[grp * 4 + j:grp * 4 + j + 1] for j in range(4)]
        hi1, lo1, hi2, lo2 = jnp.maximum(a, b), jnp.minimum(a, b), jnp.maximum(c, d), jnp.minimum(c, d)
        gscore.append(jnp.maximum(hi1, hi2) + jnp.maximum(jnp.minimum(hi1, hi2), jnp.maximum(lo1, lo2)))
    best, gidx = gscore[0], jnp.zeros_like(gscore[0], dtype=jnp.int32)
    for grp in range(1, N_EXPERT_GROUPS):
        better = gscore[grp] > best
        gidx = jnp.where(better, grp, gidx)
        best = jnp.where(better, gscore[grp], best)
    vals, raw = [], []
    for j in range(EXPERTS_PER_GROUP):
        vj, rj = sel[j:j + 1], score[j:j + 1]
        for grp in range(1, N_EXPERT_GROUPS):
            vj = jnp.where(gidx == grp, sel[grp * 4 + j:grp * 4 + j + 1], vj)
            rj = jnp.where(gidx == grp, score[grp * 4 + j:grp * 4 + j + 1], rj)
        vals.append(vj)
        raw.append(rj)
    i1, m1, g1 = jnp.zeros_like(gidx), vals[0], raw[0]
    for j in range(1, EXPERTS_PER_GROUP):
        better = vals[j] > m1
        i1 = jnp.where(better, j, i1)
        m1 = jnp.where(better, vals[j], m1)
        g1 = jnp.where(better, raw[j], g1)
    i2, m2, g2 = jnp.zeros_like(gidx), jnp.full_like(m1, -jnp.inf), jnp.zeros_like(m1)
    for j in range(EXPERTS_PER_GROUP):
        better = jnp.logical_and(i1 != j, vals[j] > m2)
        i2 = jnp.where(better, j, i2)
        m2 = jnp.where(better, vals[j], m2)
        g2 = jnp.where(better, raw[j], g2)
    total = g1 + g2
    e1 = gidx * EXPERTS_PER_GROUP + i1
    e2 = gidx * EXPERTS_PER_GROUP + i2
    e_ref[0:1, :] = e1
    e_ref[1:2, :] = e2
    gt_ref[0:1, :] = g1 / total
    gt_ref[1:2, :] = g2 / total
    @pl.when(pl.program_id(0) == 0)
    def _():
        cnt_ref[...] = jnp.zeros_like(cnt_ref)

    eid = lax.broadcasted_iota(jnp.int32, logits.shape, 0)
    is1, is2 = eid == e1, eid == e2
    member = jnp.where(jnp.logical_or(is1, is2), 1.0, 0.0)
    before = jnp.dot(member.astype(BF16), tri_ref[...], preferred_element_type=F32) - member
    base = cnt_ref[:, 0:1] + before
    r_ref[0:1, :] = jnp.sum(jnp.where(is1, base, 0.0), axis=0, keepdims=True).astype(jnp.int32)
    r_ref[1:2, :] = jnp.sum(jnp.where(is2, base, 0.0), axis=0, keepdims=True).astype(jnp.int32)
    cnt_ref[...] = cnt_ref[...] + jnp.sum(member, axis=1, keepdims=True)


def _out_route(y2, w_out, x2, gate_m, gain, scale, shift, wr_t, bias_col, seq, tm):
    t, d = x2.shape
    per_b = seq // tm
    tri = jnp.asarray(np.triu(np.ones((tm, tm), np.float32)), dtype=BF16)
    per_batch = pl.BlockSpec((1, 1, d), lambda i: (i // per_b, 0, 0))
    rows = pl.BlockSpec((tm, d), lambda i: (i, 0))
    pair = pl.BlockSpec((2, tm), lambda i: (0, i))
    return pl.pallas_call(
        _route_kernel,
        grid=(t // tm,),
        in_specs=[rows,
                  pl.BlockSpec((d, d), lambda i: (0, 0)),
                  rows,
                  per_batch,
                  pl.BlockSpec((1, d), lambda i: (0, 0)),
                  per_batch,
                  per_batch,
                  pl.BlockSpec((N_EXPERTS, d), lambda i: (0, 0)),
                  pl.BlockSpec((N_EXPERTS, 1), lambda i: (0, 0)),
                  pl.BlockSpec((tm, tm), lambda i: (0, 0))],
        out_specs=[rows, rows, pair, pair, pair, pl.BlockSpec((N_EXPERTS, LANE), lambda i: (0, 0))],
        out_shape=[jax.ShapeDtypeStruct((t, d), F32),
                   jax.ShapeDtypeStruct((t, d), BF16),
                   jax.ShapeDtypeStruct((2, t), jnp.int32),
                   jax.ShapeDtypeStruct((2, t), F32),
                   jax.ShapeDtypeStruct((2, t), jnp.int32),
                   jax.ShapeDtypeStruct((N_EXPERTS, LANE), F32)],
        compiler_params=pltpu.CompilerParams(dimension_semantics=("arbitrary",),
                                             vmem_limit_bytes=VMEM_LIMIT),
        name="out_route",
    )(y2, w_out, x2, gate_m, gain, scale, shift, wr_t, bias_col, tri)


def _expert_kernel(ib_ref, ie_ref, lo_ref, hi_ref, x_ref, wg_ref, wu_ref, wd_ref, o_ref, wg_s, wu_s, wd_s):
    i = pl.program_id(0)
    blk, lo, hi = ib_ref[i], lo_ref[i], hi_ref[i]

    @pl.when(jnp.logical_or(i == 0, ie_ref[i] != ie_ref[jnp.maximum(i - 1, 0)]))
    def _():
        wg_s[...] = wg_ref[0, 0].astype(BF16)
        wu_s[...] = wu_ref[0, 0].astype(BF16)
        wd_s[...] = wd_ref[0, 0].astype(BF16)

    @pl.when(hi > lo)
    def _():
        x = x_ref[...]
        gate = jnp.dot(x, wg_s[...], preferred_element_type=F32)
        up = jnp.dot(x, wu_s[...], preferred_element_type=F32)
        act = (_silu(gate) * up).astype(BF16)
        y = jnp.dot(act, wd_s[...], preferred_element_type=F32).astype(o_ref.dtype)
        rows = blk * MOE_ROWS + lax.broadcasted_iota(jnp.int32, y.shape, 0)
        mine = jnp.logical_and(rows >= lo, rows < hi)

        @pl.when(lo == blk * MOE_ROWS)
        def _():
            o_ref[...] = jnp.where(mine, y, jnp.zeros_like(y))

        @pl.when(lo != blk * MOE_ROWS)
        def _():
            o_ref[...] = jnp.where(mine, y, o_ref[...])


def _experts(xb, w_gate, w_up, w_down, layer, items):
    n_rows, d = xb.shape
    de = w_gate.shape[-1]
    n_items = items[0].shape[0]
    grid_spec = pltpu.PrefetchScalarGridSpec(
        num_scalar_prefetch=4,
        grid=(n_items,),
        in_specs=[pl.BlockSpec((MOE_ROWS, d), lambda i, ib, ie, lo, hi: (ib[i], 0)),
                  pl.BlockSpec((1, 1, d, de), lambda i, ib, ie, lo, hi: (layer, ie[i], 0, 0)),
                  pl.BlockSpec((1, 1, d, de), lambda i, ib, ie, lo, hi: (layer, ie[i], 0, 0)),
                  pl.BlockSpec((1, 1, de, d), lambda i, ib, ie, lo, hi: (layer, ie[i], 0, 0))],
        out_specs=pl.BlockSpec((MOE_ROWS, d), lambda i, ib, ie, lo, hi: (ib[i], 0)),
        scratch_shapes=[pltpu.VMEM((d, de), BF16), pltpu.VMEM((d, de), BF16), pltpu.VMEM((de, d), BF16)],
    )
    return pl.pallas_call(
        _expert_kernel,
        grid_spec=grid_spec,
        out_shape=jax.ShapeDtypeStruct((n_rows, d), BF16),
        compiler_params=pltpu.CompilerParams(dimension_semantics=("arbitrary",),
                                             vmem_limit_bytes=VMEM_LIMIT),
        name="experts",
    )(*items, xb, w_gate, w_up, w_down)


def _combine_kernel(x_ref, y0_ref, y1_ref, w_ref, g_ref, fg_ref, o_ref, *, final):
    w = w_ref[...]
    moe = y0_ref[...].astype(F32) * w[:, 0:1] + y1_ref[...].astype(F32) * w[:, 1:2]
    x = x_ref[...] + g_ref[0] * moe
    if final:
        x = x * lax.rsqrt(jnp.mean(x * x, axis=-1, keepdims=True) + NORM_EPS) * fg_ref[...]
    o_ref[...] = x


def _combine(x2, y0, y1, weights, gate, final_gain, final, seq, tm):
    t, d = x2.shape
    per_b = seq // tm
    return pl.pallas_call(
        functools.partial(_combine_kernel, final=final),
        grid=(t // tm,),
        in_specs=[pl.BlockSpec((tm, d), lambda i: (i, 0)),
                  pl.BlockSpec((tm, d), lambda i: (i, 0)),
                  pl.BlockSpec((tm, d), lambda i: (i, 0)),
                  pl.BlockSpec((tm, 2), lambda i: (i, 0)),
                  pl.BlockSpec((1, 1, d), lambda i: (i // per_b, 0, 0)),
                  pl.BlockSpec((1, d), lambda i: (0, 0))],
        out_specs=pl.BlockSpec((tm, d), lambda i: (i, 0)),
        out_shape=jax.ShapeDtypeStruct((t, d), F32),
        compiler_params=pltpu.CompilerParams(dimension_semantics=("parallel",),
                                             vmem_limit_bytes=VMEM_LIMIT),
        name="combine",
    )(x2, y0, y1, weights, gate, final_gain)


def _out_moe(y2, w_out, x2, gate_m, gain, scale, shift, gate, wr_t, bias_col, w_gate, w_up, w_down, layer,
             final_gain, final, seq, tm):
    t, d = x2.shape
    x2, h, experts, weights, ranks, cnt = _out_route(y2, w_out, x2, gate_m, gain, scale, shift, wr_t, bias_col,
                                                     seq, tm)
    counts = cnt[:, 0].astype(jnp.int32)
    end = jnp.cumsum(counts)
    start = end - counts
    dest = ranks
    for e in range(N_EXPERTS):
        dest = dest + jnp.where(experts == e, start[e], 0)
    tok = jnp.arange(t, dtype=jnp.uint32)
    keys = dest.astype(jnp.uint32) * jnp.uint32(t) + tok[None, :]
    slot_tok = (jnp.sort(keys.reshape(-1)) % jnp.uint32(t)).astype(jnp.int32)
    xb = jnp.take(h, slot_tok, axis=0, mode="clip")
    n_blocks = 2 * t // MOE_ROWS
    n_items = n_blocks + N_EXPERTS - 1
    first_blk = start // MOE_ROWS
    per_expert = jnp.where(counts > 0, (end - 1) // MOE_ROWS - first_blk + 1, 0)
    item_end = jnp.cumsum(per_expert)
    item_start = item_end - per_expert
    idx = jnp.arange(n_items, dtype=jnp.int32)
    item_e = jnp.zeros((n_items,), jnp.int32)
    for e in range(N_EXPERTS - 1):
        item_e = item_e + (idx >= item_end[e]).astype(jnp.int32)
    pick = lambda table: sum(jnp.where(item_e == e, table[e], 0) for e in range(N_EXPERTS))
    valid = idx < item_end[-1]
    item_blk = jnp.where(valid, pick(first_blk) + idx - pick(item_start), n_blocks - 1)
    item_lo = jnp.where(valid, jnp.maximum(pick(start), item_blk * MOE_ROWS), 0)
    item_hi = jnp.where(valid, jnp.minimum(pick(end), (item_blk + 1) * MOE_ROWS), 0)
    yb = _experts(xb, w_gate, w_up, w_down, layer, (item_blk, item_e, item_lo, item_hi))
    y0 = jnp.take(yb, dest[0], axis=0, mode="clip", unique_indices=True)
    y1 = jnp.take(yb, dest[1], axis=0, mode="clip", unique_indices=True)
    return _combine(x2, y0, y1, weights.T, gate, final_gain, final, seq, tm)


def kernel(x, c, norm_gain, w_ada, b_ada, w_in_even, w_in_odd, w_out, a_mu, a_w0, a_w_up, a_a0, a_a_up,
           a_g_up, a_k_k, a_k_a, a_r_k, a_ln_gain, a_ln_bias, b_alpha_up, b_alpha_bias, b_norm_gain, c_lb,
           c_norm_gain, d_conv_w, d_conv_b, d_dt_bias, d_a_log, d_skip, d_norm_gain, w_router, router_bias,
           w_gate, w_up, w_down, final_gain):
    bsz, seq, d = x.shape
    depth = w_ada.shape[0]
    t = bsz * seq
    tm = min(512, seq)
    pair_masks = _chunk_constants()
    ind = _head_indicator(LANE, HEAD_DIM)

    mods = _ada(c, w_ada, b_ada).reshape(depth, bsz, 6, 1, d)
    wr_t = w_router.T
    bias_col = router_bias.reshape(N_EXPERTS, 1)

    x2 = x.reshape(t, d)
    for l in range(depth):
        j = l // 2
        sh_m, sc_m, g_m, sh_f, sc_f, g_f = [mods[l, :, i] for i in range(6)]
        gain_m, gain_f = norm_gain[l, 0].reshape(1, d), norm_gain[l, 1].reshape(1, d)
        if l % 2 == 0:
            w = w_in_even[j]
            zpad = jnp.zeros((d, LANE - GLA_GATE_RANK), w.dtype)
            w = jnp.concatenate([w[:, :RWKV_IN + 1024 + GLA_GATE_RANK], zpad,
                                 w[:, RWKV_IN + 1024 + GLA_GATE_RANK:]], axis=1).astype(BF16)
            u = _in_proj(x2, gain_m, sc_m, sh_m, w, seq, tm)
            alpha_up = jnp.concatenate([b_alpha_up[j], jnp.zeros((LANE - GLA_GATE_RANK, 256), F32)], axis=0)
            p = dict(mu=_row(a_mu[j]), w0=_row(a_w0[j]), w_up=a_w_up[j], a0=_row(a_a0[j]), a_up=a_a_up[j],
                     g_up=a_g_up[j], k_k=_row(a_k_k[j]), k_a=_row(a_k_a[j]), r_k=_row(a_r_k[j]),
                     ln_gain=_row(a_ln_gain[j]), ln_bias=_row(a_ln_bias[j]), alpha_up=alpha_up,
                     alpha_bias=_row(b_alpha_bias[j]), b_norm_gain=_row(b_norm_gain[j]))
            y = _mix_even(u.reshape(bsz, seq, EVEN_COLS), p, (pair_masks, ind))
        else:
            w = w_in_odd[j]
            n_main = w.shape[1] - SSD_HEADS
            w = jnp.concatenate([w[:, :n_main], jnp.repeat(w[:, n_main:], HEAD_DIM, axis=1)], axis=1).astype(BF16)
            u = _in_proj(x2, gain_m, sc_m, sh_m, w, seq, tm)
            p = dict(c_lb=c_lb.astype(F32), c_norm_gain=_row(c_norm_gain[j]), conv_w=d_conv_w[j],
                     conv_b=_row(d_conv_b[j]), dt_bias=_row(jnp.repeat(d_dt_bias[j], HEAD_DIM)),
                     a_log=_row(jnp.repeat(d_a_log[j], HEAD_DIM)), skip=_row(jnp.repeat(d_skip[j], HEAD_DIM)),
                     d_norm_gain=_row(d_norm_gain[j]))
            y = _mix_odd(u.reshape(bsz, seq, ODD_COLS), p, l, pair_masks)
        x2 = _out_moe(y.reshape(t, d), w_out[l].astype(BF16), x2, g_m, gain_f, sc_f, sh_f, g_f, wr_t, bias_col,
                      w_gate, w_up, w_down, l, final_gain.reshape(1, d), l == depth - 1, seq, tm)
    return x2.reshape(bsz, seq, d)
```

```python
import functools

import numpy as np
import jax
import jax.numpy as jnp
from jax import lax
from jax.experimental import pallas as pl
from jax.experimental.pallas import tpu as pltpu

F32 = jnp.float32
BF16 = jnp.bfloat16
HIGHEST = lax.Precision.HIGHEST

D_MODEL = 1024
MIX_HALF = 512
HEAD_DIM = 64
CHUNK = 64
NORM_EPS = 1e-6
RWKV_HEADS = 8
RWKV_IN = 1792
RWKV_GN_EPS = 64e-5
RWKV_DECAY_SCALE = float(np.exp(-0.5))
LOG2_E = float(np.log2(np.e))
GLA_HEADS = 4
GLA_GATE_RANK = 16
GLA_GATE_NORM = 16.0
SSD_HEADS = 8
SSD_GROUPS = 2
SSD_STATE = 64
SSD_CONV = 4
SSD_XBC = 768
N_EXPERTS = 16
N_EXPERT_GROUPS = 4
EXPERTS_PER_GROUP = 4
D_EXPERT = 512
LANE = 128
EVEN_COLS = 3456
ODD_COLS = 3328
MOE_ROWS = 512
EVEN_TILE, EVEN_ROWS = 64, 4
ODD_TILE, ODD_ROWS = 128, 1
VMEM_LIMIT = 48 * 1024 * 1024


def _dot(a, b):
    return jnp.dot(a.astype(BF16), b.astype(BF16), preferred_element_type=F32)


def _dot_nt(a, b):
    return lax.dot_general(a.astype(BF16), b.astype(BF16), (((1,), (1,)), ((), ())),
                           preferred_element_type=F32)


def _dot_tn(a, b):
    return lax.dot_general(a.astype(BF16), b.astype(BF16), (((0,), (0,)), ((), ())),
                           preferred_element_type=F32)


def _dot_exact(a, b):
    return jnp.dot(a, b, precision=HIGHEST, preferred_element_type=F32)


def _sigmoid(x):
    return 1.0 / (1.0 + jnp.exp(-x))


def _silu(x):
    return x * _sigmoid(x)


def _softplus(x):
    return jnp.maximum(x, 0.0) + jnp.log(1.0 + jnp.exp(-jnp.abs(x)))


def _log_sigmoid(x):
    return jnp.minimum(x, 0.0) - jnp.log(1.0 + jnp.exp(-jnp.abs(x)))


def _chunk_constants():
    t = np.arange(CHUNK)
    masks = []
    for shift in range(5, -1, -1):
        masks.append(((t[:, None] > t[None, :]) & (((t[:, None] ^ t[None, :]) >> shift) == 1)))
    masks.append(t[:, None] == t[None, :])
    return jnp.asarray(np.stack(masks).astype(np.float32))


def _head_indicator(width, seg):
    i = np.arange(width)
    return jnp.asarray((i[:, None] // seg == i[None, :] // seg).astype(np.float32), dtype=BF16)


def _ada_kernel(c_ref, w_ref, b_ref, o_ref):
    cond = _silu(c_ref[...])
    o_ref[0] = _dot_exact(cond, w_ref[0]) + b_ref[0]


def _ada(c, w_ada, b_ada):
    depth, d, n = w_ada.shape
    bsz = c.shape[0]
    tn = 1536
    return pl.pallas_call(
        _ada_kernel,
        grid=(depth, n // tn),
        in_specs=[pl.BlockSpec((bsz, d), lambda l, j: (0, 0)),
                  pl.BlockSpec((1, d, tn), lambda l, j: (l, 0, j)),
                  pl.BlockSpec((1, 1, tn), lambda l, j: (l, 0, j))],
        out_specs=pl.BlockSpec((1, bsz, tn), lambda l, j: (l, 0, j)),
        out_shape=jax.ShapeDtypeStruct((depth, bsz, n), F32),
        compiler_params=pltpu.CompilerParams(dimension_semantics=("parallel", "parallel"),
                                             vmem_limit_bytes=VMEM_LIMIT),
        name="ada",
    )(c, w_ada, b_ada.reshape(depth, 1, n))


def _modulated_norm(x, gain, scale, shift):
    ms = jnp.mean(x * x, axis=-1, keepdims=True)
    return (x * lax.rsqrt(ms + NORM_EPS)) * gain * (1.0 + scale) + shift


def _in_proj_kernel(x_ref, gain_ref, sc_ref, sh_ref, w_ref, o_ref, *, col_chunk):
    h = _modulated_norm(x_ref[...], gain_ref[...], sc_ref[0], sh_ref[0]).astype(BF16)
    n = o_ref.shape[1]
    for j in range(0, n, col_chunk):
        o_ref[:, j:j + col_chunk] = jnp.dot(h, w_ref[:, j:j + col_chunk],
                                            preferred_element_type=F32).astype(o_ref.dtype)


def _in_proj(x2, gain, scale, shift, w, seq, tm):
    t, d = x2.shape
    n = w.shape[1]
    col_chunk = next(n // parts for parts in (3, 2, 1) if n % (parts * LANE) == 0)
    per_b = seq // tm
    return pl.pallas_call(
        functools.partial(_in_proj_kernel, col_chunk=col_chunk),
        grid=(t // tm,),
        in_specs=[pl.BlockSpec((tm, d), lambda i: (i, 0)),
                  pl.BlockSpec((1, d), lambda i: (0, 0)),
                  pl.BlockSpec((1, 1, d), lambda i: (i // per_b, 0, 0)),
                  pl.BlockSpec((1, 1, d), lambda i: (i // per_b, 0, 0)),
                  pl.BlockSpec((d, n), lambda i: (0, 0))],
        out_specs=pl.BlockSpec((tm, n), lambda i: (i, 0)),
        out_shape=jax.ShapeDtypeStruct((t, n), BF16),
        compiler_params=pltpu.CompilerParams(dimension_semantics=("parallel",),
                                             vmem_limit_bytes=VMEM_LIMIT),
        name="in_proj",
    )(x2, gain, scale, shift, w)


def _cumsum_rows(x, on_mxu):
    if not on_mxu:
        rows = lax.broadcasted_iota(jnp.int32, x.shape, 0)
        step = 1
        while step < CHUNK:
            x = x + jnp.where(rows >= step, pltpu.roll(x, step, axis=0), 0.0)
            step *= 2
        return x
    n = x.shape[1]
    hi = x.astype(BF16)
    lo = (x - hi.astype(F32)).astype(BF16)
    row = lax.broadcasted_iota(jnp.int32, (CHUNK, CHUNK), 0)
    col = lax.broadcasted_iota(jnp.int32, (CHUNK, CHUNK), 1)
    tril = jnp.where(col <= row, 1.0, 0.0).astype(BF16)
    both = jnp.dot(tril, jnp.concatenate([hi, lo], axis=1), preferred_element_type=F32)
    return both[:, 0:n] + both[:, n:]


def _level_refs(b):
    cols = b.shape[1]
    rows = lax.broadcasted_iota(jnp.int32, b.shape, 0)

    def spread(offset, span):
        pieces = [jnp.broadcast_to(b[s + offset:s + offset + 1], (span, cols)) for s in range(0, CHUNK, span)]
        return pieces[0] if len(pieces) == 1 else jnp.concatenate(pieces, axis=0)

    refs = [spread(n // 2 - 1, n) for n in (64, 32, 16, 8)]
    refs.append(jnp.where((rows & 7) < 4, spread(1, 8), spread(5, 8)))
    refs.append(jnp.where((rows & 1) == 1, pltpu.roll(b, 1, axis=0), b))
    return refs


def _gla_chunk(q, k, v, g, st, pair_masks, heads, dk, dv, cumsum_on_mxu):
    hs = range(heads)
    ks = [slice(h * dk, (h + 1) * dk) for h in hs]
    vs = [slice(h * dv, (h + 1) * dv) for h in hs]
    b = _cumsum_rows(g, cumsum_on_mxu)
    b_last = b[CHUNK - 1:CHUNK]
    q_in = q * jnp.exp2(b)
    k_st = k * jnp.exp2(b_last - b)
    decay = jnp.exp2(b_last)
    yield

    refs = _level_refs(b)
    scores = [None] * heads
    for lvl in range(7):
        if lvl < 6:
            e = jnp.exp2(-jnp.abs(b - refs[lvl]))
            qe, ke = q * e, k * e
        else:
            qe, ke = q, k
        keep = pair_masks[lvl] > 0.5
        for h in hs:
            p = _dot_nt(qe[:, ks[h]], ke[:, ks[h]])
            scores[h] = jnp.where(keep, p, 0.0) if scores[h] is None else jnp.where(keep, p, scores[h])
        yield
    o_inter = [_dot_nt(q_in[:, ks[h]], st[h]) for h in hs]
    o_intra = [_dot(scores[h], v[:, vs[h]]) for h in hs]
    yield
    st_new = [st[h] * decay[:, ks[h]] + _dot_tn(v[:, vs[h]], k_st[:, ks[h]]) for h in hs]
    return jnp.concatenate([o_inter[h] + o_intra[h] for h in hs], axis=-1), st_new


def _interleave(*stages):
    results = [None] * len(stages)
    live = list(range(len(stages)))
    while live:
        for i in list(live):
            try:
                next(stages[i])
            except StopIteration as stop:
                results[i] = stop.value
                live.remove(i)
    return results


def _head_rms(o, gain, heads, dv):
    outs = []
    for h in range(heads):
        oh = o[:, h * dv:(h + 1) * dv]
        ms = jnp.mean(oh * oh, axis=-1, keepdims=True)
        outs.append(oh * lax.rsqrt(ms + NORM_EPS) * gain)
    return jnp.concatenate(outs, axis=-1)


def _rwkv_chunk(r, kk, a, kt, v, logw, s0):
    b = _cumsum_rows(logw, True)
    b_last = b[CHUNK - 1:CHUNK]
    e_neg = jnp.exp2(-b)
    e_end = jnp.exp2(b_last - b)
    decay = jnp.exp2(b_last)
    beta = a * kk
    k_bar = kk * jnp.exp2(b - logw)
    r_bar = r * jnp.exp2(b)
    beta_t, k_t = beta * e_neg, kt * e_neg
    beta_hat, k_hat = beta * e_end, kt * e_end
    row = lax.broadcasted_iota(jnp.int32, (CHUNK, CHUNK), 0)
    col = lax.broadcasted_iota(jnp.int32, (CHUNK, CHUNK), 1)
    strict = col < row
    incl = col <= row
    same_blk = (row >> 4) == (col >> 4)
    in_blk = jnp.logical_and(strict, same_blk)
    off_blk = jnp.logical_and(strict, jnp.logical_not(same_blk))
    eye = (row == col).astype(F32)
    hs = range(RWKV_HEADS)
    sls = [slice(h * HEAD_DIM, (h + 1) * HEAD_DIM) for h in hs]
    kr = [jnp.concatenate([k_bar[:, sl], r_bar[:, sl]], axis=0).astype(BF16) for sl in sls]
    bk = [jnp.concatenate([beta_t[:, sl], k_t[:, sl]], axis=0) for sl in sls]
    v_b = v.astype(BF16)
    vh = [v_b[:, sl] for sl in sls]
    yield
    m1 = [_dot_nt(kr[h], bk[h]) for h in hs]
    yield
    m2 = [_dot_nt(kr[h], s0[h]) for h in hs]
    yield
    x1 = [jnp.where(in_blk, m[0:CHUNK, 0:CHUNK], 0.0) for m in m1]
    a_off = [jnp.where(off_blk, m[0:CHUNK, 0:CHUNK], 0.0) for m in m1]
    b_m = [jnp.where(strict, m[0:CHUNK, CHUNK:], 0.0) for m in m1]
    cb_m = [jnp.where(incl, m[CHUNK:, 0:CHUNK], 0.0) for m in m1]
    ck_m = [jnp.where(incl, m[CHUNK:, CHUNK:], 0.0) for m in m1]
    rhs = [m2[h][0:CHUNK] + _dot(b_m[h], vh[h]) for h in hs]
    lo, hi = slice(0, CHUNK), slice(CHUNK, 2 * CHUNK)
    side = lambda left, right: jnp.concatenate([left, right], axis=1)
    x2 = [_dot(x, x) for x in x1]
    yield
    p = [eye - x for x in x1]
    w = [_dot(x2[h], side(p[h], x2[h])) for h in hs]
    yield
    p = [p[h] + w[h][:, lo] for h in hs]
    x4 = [w[h][:, hi] for h in hs]
    w = [_dot(x4[h], side(p[h], x4[h])) for h in hs]
    yield
    p = [p[h] + w[h][:, lo] for h in hs]
    t_d = [p[h] + _dot(w[h][:, hi], p[h]) for h in hs]
    yield
    nz = [_dot(t_d[h], side(a_off[h], rhs[h])) for h in hs]
    yield
    w = [_dot(nz[h][:, lo], nz[h]) for h in hs]
    yield
    y1 = [w[h][:, hi] - nz[h][:, hi] for h in hs]
    neg_u = [y1[h] + _dot(w[h][:, lo], y1[h]) for h in hs]
    vu = [jnp.concatenate([vh[h], neg_u[h].astype(BF16)], axis=0) for h in hs]
    yield
    outs = [m2[h][CHUNK:] + _dot(side(ck_m[h], cb_m[h]), vu[h]) for h in hs]
    yield
    s_new = [s0[h] * decay[:, sls[h]]
             + _dot_tn(vu[h], jnp.concatenate([k_hat[:, sls[h]], beta_hat[:, sls[h]]], axis=0)) for h in hs]
    return jnp.concatenate(outs, axis=-1), s_new


def _head_sums(x, ind):
    return jnp.concatenate([_dot(x[:, i:i + LANE], ind) for i in range(0, x.shape[1], LANE)], axis=1)


def _mix_even_kernel(u_ref, mu_ref, w0_ref, wup_ref, a0_ref, aup_ref, gup_ref, kk_ref, ka_ref,
                     rk_ref, lng_ref, lnb_ref, alup_ref, albias_ref, bng_ref, ind_ref,
                     pmask_ref, y_ref, s_ref, g_ref, prev_ref):
    @pl.when(pl.program_id(1) == 0)
    def _():
        s_ref[...] = jnp.zeros_like(s_ref)
        g_ref[...] = jnp.zeros_like(g_ref)
        prev_ref[...] = jnp.zeros_like(prev_ref)

    pair_masks = pmask_ref[...]
    ind = ind_ref[...]
    n_rows = u_ref.shape[0]
    s_state = [[s_ref[b * RWKV_HEADS + h] for h in range(RWKV_HEADS)] for b in range(n_rows)]
    g_state = [[g_ref[b * GLA_HEADS + h] for h in range(GLA_HEADS)] for b in range(n_rows)]
    prev = [prev_ref[b] for b in range(n_rows)]

    def rwkv_stages(b, rs, prev, s_state):
        ua = u_ref[b, rs, 0:RWKV_IN].astype(F32)
        rows = lax.broadcasted_iota(jnp.int32, ua.shape, 0)
        shifted = jnp.where(rows == 0, prev, pltpu.roll(ua, 1, axis=0))
        xa = ua + mu_ref[...] * (shifted - ua)
        r, k, v = xa[:, 0:512], xa[:, 512:1024], xa[:, 1024:1536]
        wd, ad, gd = xa[:, 1536:1600], xa[:, 1600:1664], xa[:, 1664:1792]
        logw = (-RWKV_DECAY_SCALE * LOG2_E) * _sigmoid(w0_ref[...] + _dot(jnp.tanh(wd), wup_ref[...]))
        a = _sigmoid(a0_ref[...] + _dot(ad, aup_ref[...]))
        gate = _dot(_sigmoid(gd), gup_ref[...])
        yield
        kk = k * kk_ref[...]
        kk = kk * lax.rsqrt(_head_sums(kk * kk, ind) + 1e-12)
        kt = k * (1.0 + (a - 1.0) * ka_ref[...])
        yield
        y, s_state = yield from _rwkv_chunk(r, kk, a, kt, v, logw, s_state)
        yield
        mean = _head_sums(y, ind) * (1.0 / HEAD_DIM)
        yc = y - mean
        yield
        var = _head_sums(yc * yc, ind) * (1.0 / HEAD_DIM)
        y = yc * lax.rsqrt(var + RWKV_GN_EPS) * lng_ref[...] + lnb_ref[...]
        y = y + _head_sums(r * kt * rk_ref[...], ind) * v
        y_ref[b, rs, 0:MIX_HALF] = (y * gate).astype(y_ref.dtype)
        return ua[CHUNK - 1:CHUNK], s_state

    def gla_stages(b, rs, g_state):
        ub = u_ref[b, rs, RWKV_IN:].astype(F32)
        q, kg, vg = ub[:, 0:256] * (HEAD_DIM ** -0.5), ub[:, 256:512], ub[:, 512:1024]
        alpha, gg = ub[:, 1024:1152], ub[:, 1152:1664]
        log_a = _log_sigmoid(_dot(alpha, alup_ref[...]) + albias_ref[...]) * (LOG2_E / GLA_GATE_NORM)
        yield
        o, g_state = yield from _gla_chunk(q, kg, vg, log_a, g_state, pair_masks, GLA_HEADS, 64, 128, True)
        o = _head_rms(o, bng_ref[...], GLA_HEADS, 128)
        y_ref[b, rs, MIX_HALF:] = (o * _silu(gg)).astype(y_ref.dtype)
        return g_state

    for c in range(u_ref.shape[1] // CHUNK):
        rs = slice(c * CHUNK, (c + 1) * CHUNK)
        done = _interleave(*[stages for b in range(n_rows)
                             for stages in (rwkv_stages(b, rs, prev[b], s_state[b]), gla_stages(b, rs, g_state[b]))])
        for b in range(n_rows):
            (prev[b], s_state[b]), g_state[b] = done[2 * b], done[2 * b + 1]
    for b in range(n_rows):
        for h in range(RWKV_HEADS):
            s_ref[b * RWKV_HEADS + h] = s_state[b][h]
        for h in range(GLA_HEADS):
            g_ref[b * GLA_HEADS + h] = g_state[b][h]
        prev_ref[b] = prev[b]


def _row(p):
    return p.reshape(1, -1).astype(F32)


def _mix_even(u3, p, consts):
    bsz, seq, n = u3.shape
    pair_masks, ind = consts
    small = [p["mu"], p["w0"], p["w_up"], p["a0"], p["a_up"], p["g_up"], p["k_k"], p["k_a"], p["r_k"],
             p["ln_gain"], p["ln_bias"], p["alpha_up"], p["alpha_bias"], p["b_norm_gain"],
             ind, pair_masks]

    def full(arr):
        nd = arr.ndim
        return pl.BlockSpec(arr.shape, lambda b, s, _nd=nd: (0,) * _nd)

    rows = EVEN_ROWS if bsz % EVEN_ROWS == 0 else 1
    return pl.pallas_call(
        _mix_even_kernel,
        grid=(bsz // rows, seq // EVEN_TILE),
        in_specs=[pl.BlockSpec((rows, EVEN_TILE, n), lambda b, s: (b, s, 0))] + [full(a) for a in small],
        out_specs=pl.BlockSpec((rows, EVEN_TILE, D_MODEL), lambda b, s: (b, s, 0)),
        out_shape=jax.ShapeDtypeStruct((bsz, seq, D_MODEL), BF16),
        scratch_shapes=[pltpu.VMEM((rows * RWKV_HEADS, HEAD_DIM, HEAD_DIM), F32),
                        pltpu.VMEM((rows * GLA_HEADS, 128, 64), F32),
                        pltpu.VMEM((rows, 1, RWKV_IN), F32)],
        compiler_params=pltpu.CompilerParams(dimension_semantics=("parallel", "arbitrary"),
                                             vmem_limit_bytes=VMEM_LIMIT),
        name="mix_even",
    )(u3, *small)


def _mix_odd_kernel(u_ref, clb_ref, cng_ref, convw_ref, convb_ref, dtb_ref, alog_ref, skip_ref,
                    dng_ref, pmask_ref, y_ref, h_ref, d_ref, tail_ref, *, layer):
    @pl.when(pl.program_id(1) == 0)
    def _():
        h_ref[...] = jnp.zeros_like(h_ref)
        d_ref[...] = jnp.zeros_like(d_ref)
        tail_ref[...] = jnp.zeros_like(tail_ref)

    pair_masks = pmask_ref[...]
    c_lb = clb_ref[...]
    c_exp = jnp.exp(c_lb - jnp.max(c_lb, axis=0, keepdims=True))
    lb = jnp.sum(c_exp[1:layer + 1], axis=0, keepdims=True) / jnp.sum(c_exp, axis=0, keepdims=True)
    n_rows = u_ref.shape[0]
    h_state = [[h_ref[b * 4 + h] for h in range(4)] for b in range(n_rows)]
    d_state = [[d_ref[b * SSD_GROUPS + grp] for grp in range(SSD_GROUPS)] for b in range(n_rows)]
    tail = [tail_ref[b] for b in range(n_rows)]

    def hgrn_stages(b, rs, h_state):
        u = u_ref[b, rs, 0:1536].astype(F32)
        q, fr, iv, g = u[:, 0:256], u[:, 256:512], u[:, 512:1024], u[:, 1024:1536]
        f = lb + (1.0 - lb) * _sigmoid(fr)
        yield
        o, h_state = yield from _gla_chunk(q, 1.0 - f, iv, jnp.log2(f), h_state, pair_masks, 4, 64, 128, False)
        y_ref[b, rs, 0:MIX_HALF] = (_head_rms(o, cng_ref[...], 4, 128) * _silu(g)).astype(y_ref.dtype)
        return h_state

    def ssd_stages(b, rs, tail, d_state):
        u = u_ref[b, rs, 1536:].astype(F32)
        y, d_state, tail = yield from _ssd_chunk(u, tail, d_state, convw_ref, convb_ref[...], dtb_ref[...],
                                                 alog_ref[...], skip_ref[...], dng_ref[...])
        y_ref[b, rs, MIX_HALF:] = y.astype(y_ref.dtype)
        return tail, d_state

    for c in range(u_ref.shape[1] // CHUNK):
        rs = slice(c * CHUNK, (c + 1) * CHUNK)
        h_state = _interleave(*[hgrn_stages(b, rs, h_state[b]) for b in range(n_rows)])
        done = _interleave(*[ssd_stages(b, rs, tail[b], d_state[b]) for b in range(n_rows)])
        for b in range(n_rows):
            tail[b], d_state[b] = done[b]
    for b in range(n_rows):
        for h in range(4):
            h_ref[b * 4 + h] = h_state[b][h]
        for grp in range(SSD_GROUPS):
            d_ref[b * SSD_GROUPS + grp] = d_state[b][grp]
        tail_ref[b] = tail[b]


def _ssd_chunk(u, tail, st, convw_ref, conv_b, dt_bias, a_log, skip, norm_gain):
    z, xbc, dt_raw = u[:, 0:512], u[:, 512:1280], u[:, 1280:1792]
    new_tail = xbc[CHUNK - 8:CHUNK]
    rows8 = lax.broadcasted_iota(jnp.int32, (8, SSD_XBC), 0)
    conv = xbc * convw_ref[SSD_CONV - 1:SSD_CONV] + conv_b
    for back in range(1, SSD_CONV):
        rolled = pltpu.roll(xbc, back, axis=0)
        head8 = jnp.where(rows8 < back, pltpu.roll(tail, back, axis=0), rolled[0:8])
        shifted = jnp.concatenate([head8, rolled[8:]], axis=0)
        conv = conv + shifted * convw_ref[SSD_CONV - 1 - back:SSD_CONV - back]
    xbc = _silu(conv)
    yield
    xs, bmat, cmat = xbc[:, 0:512], xbc[:, 512:640], xbc[:, 640:768]
    dt = _softplus(dt_raw + dt_bias)
    da = dt * (-LOG2_E * jnp.exp(a_log))
    cum = _cumsum_rows(da, False)
    yield
    cum_last = cum[CHUNK - 1:CHUNK]
    e_cum = jnp.exp2(cum)
    xdt = xs * dt
    x_end = xdt * jnp.exp2(cum_last - cum)
    decay = jnp.exp2(cum_last)
    yield
    row = lax.broadcasted_iota(jnp.int32, (CHUNK, CHUNK), 0)
    col = lax.broadcasted_iota(jnp.int32, (CHUNK, CHUNK), 1)
    causal = col <= row
    groups = range(SSD_GROUPS)
    heads = range(SSD_HEADS)
    per_group = SSD_HEADS // SSD_GROUPS
    gs = [slice(grp * 256, (grp + 1) * 256) for grp in groups]
    ns = [slice(grp * SSD_STATE, (grp + 1) * SSD_STATE) for grp in groups]
    hs = [slice(h * HEAD_DIM, (h + 1) * HEAD_DIM) for h in heads]
    cb = [_dot_nt(cmat[:, ns[grp]], bmat[:, ns[grp]]) for grp in groups]
    y_off = [_dot(cmat[:, ns[grp]], st[grp]) for grp in groups]
    st_new = [st[grp] * decay[:, gs[grp]] + _dot_tn(bmat[:, ns[grp]], x_end[:, gs[grp]]) for grp in groups]
    yield
    cum_h = [cum[:, sl] for sl in hs]
    seg = [jnp.where(causal, jnp.exp2(ch - ch.T), 0.0) for ch in cum_h]
    yield
    y_diag = [_dot(cb[h // per_group] * seg[h], xdt[:, hs[h]]) for h in heads]
    yield
    y = (jnp.concatenate(y_diag, axis=-1) + jnp.concatenate(y_off, axis=-1) * e_cum + skip * xs)
    y = y * _silu(z)
    parts = []
    for grp in range(SSD_GROUPS):
        yg = y[:, grp * 256:(grp + 1) * 256]
        ms = jnp.mean(yg * yg, axis=-1, keepdims=True)
        parts.append(yg * lax.rsqrt(ms + NORM_EPS))
    return jnp.concatenate(parts, axis=-1) * norm_gain, st_new, new_tail


def _mix_odd(u3, p, layer, consts):
    bsz, seq, n = u3.shape
    pair_masks = consts
    small = [p["c_lb"], p["c_norm_gain"], p["conv_w"], p["conv_b"], p["dt_bias"], p["a_log"], p["skip"],
             p["d_norm_gain"], pair_masks]

    def full(arr):
        nd = arr.ndim
        return pl.BlockSpec(arr.shape, lambda b, s, _nd=nd: (0,) * _nd)

    rows = ODD_ROWS if bsz % ODD_ROWS == 0 else 1
    return pl.pallas_call(
        functools.partial(_mix_odd_kernel, layer=layer),
        grid=(bsz // rows, seq // ODD_TILE),
        in_specs=[pl.BlockSpec((rows, ODD_TILE, n), lambda b, s: (b, s, 0))] + [full(a) for a in small],
        out_specs=pl.BlockSpec((rows, ODD_TILE, D_MODEL), lambda b, s: (b, s, 0)),
        out_shape=jax.ShapeDtypeStruct((bsz, seq, D_MODEL), BF16),
        scratch_shapes=[pltpu.VMEM((rows * 4, 128, 64), F32),
                        pltpu.VMEM((rows * SSD_GROUPS, SSD_STATE, 256), F32),
                        pltpu.VMEM((rows, 8, SSD_XBC), F32)],
        compiler_params=pltpu.CompilerParams(dimension_semantics=("parallel", "arbitrary"),
                                             vmem_limit_bytes=VMEM_LIMIT),
        name="mix_odd",
    )(u3, *small)


def _route_kernel(y_ref, wo_ref, x_ref, gm_ref, gain_ref, sc_ref, sh_ref, wr_ref, bias_ref, tri_ref,
                  xo_ref, h_ref, e_ref, gt_ref, r_ref, cnt_ref):
    x = x_ref[...] + gm_ref[0] * jnp.dot(y_ref[...], wo_ref[...], preferred_element_type=F32)
    xo_ref[...] = x
    h = _modulated_norm(x, gain_ref[...], sc_ref[0], sh_ref[0])
    wr = wr_ref[...]
    wr_hi = wr.astype(BF16)
    wr_lo = (wr - wr_hi.astype(F32)).astype(BF16)
    h_hi = h.astype(BF16)
    h_ref[...] = h_hi
    h_lo = (h - h_hi.astype(F32)).astype(BF16)
    both = _dot_nt(jnp.concatenate([wr_hi, wr_lo], axis=0), h_hi)
    logits = both[0:N_EXPERTS] + both[N_EXPERTS:] + _dot_nt(wr_hi, h_lo)
    score = _sigmoid(logits)
    sel = score + bias_ref[...]
    gscore = []
    for grp in range(N_EXPERT_GROUPS):
        a, b, c, d = [sel[grp * 4 + j:grp * 4 + j + 1] for j in range(4)]
        hi1, lo1, hi2, lo2 = jnp.maximum(a, b), jnp.minimum(a, b), jnp.maximum(c, d), jnp.minimum(c, d)
        gscore.append(jnp.maximum(hi1, hi2) + jnp.maximum(jnp.minimum(hi1, hi2), jnp.maximum(lo1, lo2)))
    best, gidx = gscore[0], jnp.zeros_like(gscore[0], dtype=jnp.int32)
    for grp in range(1, N_EXPERT_GROUPS):
        better = gscore[grp] > best
        gidx = jnp.where(better, grp, gidx)
        best = jnp.where(better, gscore[grp], best)
    vals, raw = [], []
    for j in range(EXPERTS_PER_GROUP):
        vj, rj = sel[j:j + 1], score[j:j + 1]
        for grp in range(1, N_EXPERT_GROUPS):
            vj = jnp.where(gidx == grp, sel[grp * 4 + j:grp * 4 + j + 1], vj)
            rj = jnp.where(gidx == grp, score[grp * 4 + j:grp * 4 + j + 1], rj)
        vals.append(vj)
        raw.append(rj)
    i1, m1, g1 = jnp.zeros_like(gidx), vals[0], raw[0]
    for j in range(1, EXPERTS_PER_GROUP):
        better = vals[j] > m1
        i1 = jnp.where(better, j, i1)
        m1 = jnp.where(better, vals[j], m1)
        g1 = jnp.where(better, raw[j], g1)
    i2, m2, g2 = jnp.zeros_like(gidx), jnp.full_like(m1, -jnp.inf), jnp.zeros_like(m1)
    for j in range(EXPERTS_PER_GROUP):
        better = jnp.logical_and(i1 != j, vals[j] > m2)
        i2 = jnp.where(better, j, i2)
        m2 = jnp.where(better, vals[j], m2)
        g2 = jnp.where(better, raw[j], g2)
    total = g1 + g2
    e1 = gidx * EXPERTS_PER_GROUP + i1
    e2 = gidx * EXPERTS_PER_GROUP + i2
    e_ref[0:1, :] = e1
    e_ref[1:2, :] = e2
    gt_ref[0:1, :] = g1 / total
    gt_ref[1:2, :] = g2 / total
    @pl.when(pl.program_id(0) == 0)
    def _():
        cnt_ref[...] = jnp.zeros_like(cnt_ref)

    eid = lax.broadcasted_iota(jnp.int32, logits.shape, 0)
    is1, is2 = eid == e1, eid == e2
    member = jnp.where(jnp.logical_or(is1, is2), 1.0, 0.0)
    before = jnp.dot(member.astype(BF16), tri_ref[...], preferred_element_type=F32) - member
    base = cnt_ref[:, 0:1] + before
    r_ref[0:1, :] = jnp.sum(jnp.where(is1, base, 0.0), axis=0, keepdims=True).astype(jnp.int32)
    r_ref[1:2, :] = jnp.sum(jnp.where(is2, base, 0.0), axis=0, keepdims=True).astype(jnp.int32)
    cnt_ref[...] = cnt_ref[...] + jnp.sum(member, axis=1, keepdims=True)


def _out_route(y2, w_out, x2, gate_m, gain, scale, shift, wr_t, bias_col, seq, tm):
    t, d = x2.shape
    per_b = seq // tm
    tri = jnp.asarray(np.triu(np.ones((tm, tm), np.float32)), dtype=BF16)
    per_batch = pl.BlockSpec((1, 1, d), lambda i: (i // per_b, 0, 0))
    rows = pl.BlockSpec((tm, d), lambda i: (i, 0))
    pair = pl.BlockSpec((2, tm), lambda i: (0, i))
    return pl.pallas_call(
        _route_kernel,
        grid=(t // tm,),
        in_specs=[rows,
                  pl.BlockSpec((d, d), lambda i: (0, 0)),
                  rows,
                  per_batch,
                  pl.BlockSpec((1, d), lambda i: (0, 0)),
                  per_batch,
                  per_batch,
                  pl.BlockSpec((N_EXPERTS, d), lambda i: (0, 0)),
                  pl.BlockSpec((N_EXPERTS, 1), lambda i: (0, 0)),
                  pl.BlockSpec((tm, tm), lambda i: (0, 0))],
        out_specs=[rows, rows, pair, pair, pair, pl.BlockSpec((N_EXPERTS, LANE), lambda i: (0, 0))],
        out_shape=[jax.ShapeDtypeStruct((t, d), F32),
                   jax.ShapeDtypeStruct((t, d), BF16),
                   jax.ShapeDtypeStruct((2, t), jnp.int32),
                   jax.ShapeDtypeStruct((2, t), F32),
                   jax.ShapeDtypeStruct((2, t), jnp.int32),
                   jax.ShapeDtypeStruct((N_EXPERTS, LANE), F32)],
        compiler_params=pltpu.CompilerParams(dimension_semantics=("arbitrary",),
                                             vmem_limit_bytes=VMEM_LIMIT),
        name="out_route",
    )(y2, w_out, x2, gate_m, gain, scale, shift, wr_t, bias_col, tri)


def _expert_kernel(ib_ref, ie_ref, lo_ref, hi_ref, x_ref, wg_ref, wu_ref, wd_ref, o_ref, wg_s, wu_s, wd_s):
    i = pl.program_id(0)
    blk, lo, hi = ib_ref[i], lo_ref[i], hi_ref[i]

    @pl.when(jnp.logical_or(i == 0, ie_ref[i] != ie_ref[jnp.maximum(i - 1, 0)]))
    def _():
        wg_s[...] = wg_ref[0, 0].astype(BF16)
        wu_s[...] = wu_ref[0, 0].astype(BF16)
        wd_s[...] = wd_ref[0, 0].astype(BF16)

    @pl.when(hi > lo)
    def _():
        x = x_ref[...]
        gate = jnp.dot(x, wg_s[...], preferred_element_type=F32)
        up = jnp.dot(x, wu_s[...], preferred_element_type=F32)
        act = (_silu(gate) * up).astype(BF16)
        y = jnp.dot(act, wd_s[...], preferred_element_type=F32).astype(o_ref.dtype)
        rows = blk * MOE_ROWS + lax.broadcasted_iota(jnp.int32, y.shape, 0)
        mine = jnp.logical_and(rows >= lo, rows < hi)

        @pl.when(lo == blk * MOE_ROWS)
        def _():
            o_ref[...] = jnp.where(mine, y, jnp.zeros_like(y))

        @pl.when(lo != blk * MOE_ROWS)
        def _():
            o_ref[...] = jnp.where(mine, y, o_ref[...])


def _experts(xb, w_gate, w_up, w_down, layer, items):
    n_rows, d = xb.shape
    de = w_gate.shape[-1]
    n_items = items[0].shape[0]
    grid_spec = pltpu.PrefetchScalarGridSpec(
        num_scalar_prefetch=4,
        grid=(n_items,),
        in_specs=[pl.BlockSpec((MOE_ROWS, d), lambda i, ib, ie, lo, hi: (ib[i], 0)),
                  pl.BlockSpec((1, 1, d, de), lambda i, ib, ie, lo, hi: (layer, ie[i], 0, 0)),
                  pl.BlockSpec((1, 1, d, de), lambda i, ib, ie, lo, hi: (layer, ie[i], 0, 0)),
                  pl.BlockSpec((1, 1, de, d), lambda i, ib, ie, lo, hi: (layer, ie[i], 0, 0))],
        out_specs=pl.BlockSpec((MOE_ROWS, d), lambda i, ib, ie, lo, hi: (ib[i], 0)),
        scratch_shapes=[pltpu.VMEM((d, de), BF16), pltpu.VMEM((d, de), BF16), pltpu.VMEM((de, d), BF16)],
    )
    return pl.pallas_call(
        _expert_kernel,
        grid_spec=grid_spec,
        out_shape=jax.ShapeDtypeStruct((n_rows, d), BF16),
        compiler_params=pltpu.CompilerParams(dimension_semantics=("arbitrary",),
                                             vmem_limit_bytes=VMEM_LIMIT),
        name="experts",
    )(*items, xb, w_gate, w_up, w_down)


def _combine_kernel(x_ref, y0_ref, y1_ref, w_ref, g_ref, fg_ref, o_ref, *, final):
    w = w_ref[...]
    moe = y0_ref[...].astype(F32) * w[:, 0:1] + y1_ref[...].astype(F32) * w[:, 1:2]
    x = x_ref[...] + g_ref[0] * moe
    if final:
        x = x * lax.rsqrt(jnp.mean(x * x, axis=-1, keepdims=True) + NORM_EPS) * fg_ref[...]
    o_ref[...] = x


def _combine(x2, y0, y1, weights, gate, final_gain, final, seq, tm):
    t, d = x2.shape
    per_b = seq // tm
    return pl.pallas_call(
        functools.partial(_combine_kernel, final=final),
        grid=(t // tm,),
        in_specs=[pl.BlockSpec((tm, d), lambda i: (i, 0)),
                  pl.BlockSpec((tm, d), lambda i: (i, 0)),
                  pl.BlockSpec((tm, d), lambda i: (i, 0)),
                  pl.BlockSpec((tm, 2), lambda i: (i, 0)),
                  pl.BlockSpec((1, 1, d), lambda i: (i // per_b, 0, 0)),
                  pl.BlockSpec((1, d), lambda i: (0, 0))],
        out_specs=pl.BlockSpec((tm, d), lambda i: (i, 0)),
        out_shape=jax.ShapeDtypeStruct((t, d), F32),
        compiler_params=pltpu.CompilerParams(dimension_semantics=("parallel",),
                                             vmem_limit_bytes=VMEM_LIMIT),
        name="combine",
    )(x2, y0, y1, weights, gate, final_gain)


def _out_moe(y2, w_out, x2, gate_m, gain, scale, shift, gate, wr_t, bias_col, w_gate, w_up, w_down, layer,
             final_gain, final, seq, tm):
    t, d = x2.shape
    x2, h, experts, weights, ranks, cnt = _out_route(y2, w_out, x2, gate_m, gain, scale, shift, wr_t, bias_col,
                                                     seq, tm)
    counts = cnt[:, 0].astype(jnp.int32)
    end = jnp.cumsum(counts)
    start = end - counts
    dest = ranks
    for e in range(N_EXPERTS):
        dest = dest + jnp.where(experts == e, start[e], 0)
    tok = jnp.arange(t, dtype=jnp.uint32)
    keys = dest.astype(jnp.uint32) * jnp.uint32(t) + tok[None, :]
    slot_tok = (jnp.sort(keys.reshape(-1)) % jnp.uint32(t)).astype(jnp.int32)
    xb = jnp.take(h, slot_tok, axis=0, mode="clip")
    n_blocks = 2 * t // MOE_ROWS
    n_items = n_blocks + N_EXPERTS - 1
    first_blk = start // MOE_ROWS
    per_expert = jnp.where(counts > 0, (end - 1) // MOE_ROWS - first_blk + 1, 0)
    item_end = jnp.cumsum(per_expert)
    item_start = item_end - per_expert
    idx = jnp.arange(n_items, dtype=jnp.int32)
    item_e = jnp.zeros((n_items,), jnp.int32)
    for e in range(N_EXPERTS - 1):
        item_e = item_e + (idx >= item_end[e]).astype(jnp.int32)
    pick = lambda table: sum(jnp.where(item_e == e, table[e], 0) for e in range(N_EXPERTS))
    valid = idx < item_end[-1]
    item_blk = jnp.where(valid, pick(first_blk) + idx - pick(item_start), n_blocks - 1)
    item_lo = jnp.where(valid, jnp.maximum(pick(start), item_blk * MOE_ROWS), 0)
    item_hi = jnp.where(valid, jnp.minimum(pick(end), (item_blk + 1) * MOE_ROWS), 0)
    yb = _experts(xb, w_gate, w_up, w_down, layer, (item_blk, item_e, item_lo, item_hi))
    y0 = jnp.take(yb, dest[0], axis=0, mode="clip", unique_indices=True)
    y1 = jnp.take(yb, dest[1], axis=0, mode="clip", unique_indices=True)
    return _combine(x2, y0, y1, weights.T, gate, final_gain, final, seq, tm)


def kernel(x, c, norm_gain, w_ada, b_ada, w_in_even, w_in_odd, w_out, a_mu, a_w0, a_w_up, a_a0, a_a_up,
           a_g_up, a_k_k, a_k_a, a_r_k, a_ln_gain, a_ln_bias, b_alpha_up, b_alpha_bias, b_norm_gain, c_lb,
           c_norm_gain, d_conv_w, d_conv_b, d_dt_bias, d_a_log, d_skip, d_norm_gain, w_router, router_bias,
           w_gate, w_up, w_down, final_gain):
    bsz, seq, d = x.shape
    depth = w_ada.shape[0]
    t = bsz * seq
    tm = min(512, seq)
    pair_masks = _chunk_constants()
    ind = _head_indicator(LANE, HEAD_DIM)

    mods = _ada(c, w_ada, b_ada).reshape(depth, bsz, 6, 1, d)
    wr_t = w_router.T
    bias_col = router_bias.reshape(N_EXPERTS, 1)

    x2 = x.reshape(t, d)
    for l in range(depth):
        j = l // 2
        sh_m, sc_m, g_m, sh_f, sc_f, g_f = [mods[l, :, i] for i in range(6)]
        gain_m, gain_f = norm_gain[l, 0].reshape(1, d), norm_gain[l, 1].reshape(1, d)
        if l % 2 == 0:
            w = w_in_even[j]
            zpad = jnp.zeros((d, LANE - GLA_GATE_RANK), w.dtype)
            w = jnp.concatenate([w[:, :RWKV_IN + 1024 + GLA_GATE_RANK], zpad,
                                 w[:, RWKV_IN + 1024 + GLA_GATE_RANK:]], axis=1).astype(BF16)
            u = _in_proj(x2, gain_m, sc_m, sh_m, w, seq, tm)
            alpha_up = jnp.concatenate([b_alpha_up[j], jnp.zeros((LANE - GLA_GATE_RANK, 256), F32)], axis=0)
            p = dict(mu=_row(a_mu[j]), w0=_row(a_w0[j]), w_up=a_w_up[j], a0=_row(a_a0[j]), a_up=a_a_up[j],
                     g_up=a_g_up[j], k_k=_row(a_k_k[j]), k_a=_row(a_k_a[j]), r_k=_row(a_r_k[j]),
                     ln_gain=_row(a_ln_gain[j]), ln_bias=_row(a_ln_bias[j]), alpha_up=alpha_up,
                     alpha_bias=_row(b_alpha_bias[j]), b_norm_gain=_row(b_norm_gain[j]))
            y = _mix_even(u.reshape(bsz, seq, EVEN_COLS), p, (pair_masks, ind))
        else:
            w = w_in_odd[j]
            n_main = w.shape[1] - SSD_HEADS
            w = jnp.concatenate([w[:, :n_main], jnp.repeat(w[:, n_main:], HEAD_DIM, axis=1)], axis=1).astype(BF16)
            u = _in_proj(x2, gain_m, sc_m, sh_m, w, seq, tm)
            p = dict(c_lb=c_lb.astype(F32), c_norm_gain=_row(c_norm_gain[j]), conv_w=d_conv_w[j],
                     conv_b=_row(d_conv_b[j]), dt_bias=_row(jnp.repeat(d_dt_bias[j], HEAD_DIM)),
                     a_log=_row(jnp.repeat(d_a_log[j], HEAD_DIM)), skip=_row(jnp.repeat(d_skip[j], HEAD_DIM)),
                     d_norm_gain=_row(d_norm_gain[j]))
            y = _mix_odd(u.reshape(bsz, seq, ODD_COLS), p, l, pair_masks)
        x2 = _out_moe(y.reshape(t, d), w_out[l].astype(BF16), x2, g_m, gain_f, sc_f, sh_f, g_f, wr_t, bias_col,
                      w_gate, w_up, w_down, l, final_gain.reshape(1, d), l == depth - 1, seq, tm)
    return x2.reshape(bsz, seq, d)
```

```python
import functools

import numpy as np
import jax
import jax.numpy as jnp
from jax import lax
from jax.experimental import pallas as pl
from jax.experimental.pallas import tpu as pltpu

F32 = jnp.float32
BF16 = jnp.bfloat16
HIGHEST = lax.Precision.HIGHEST

D_MODEL = 1024
MIX_HALF = 512
HEAD_DIM = 64
CHUNK = 64
NORM_EPS = 1e-6
RWKV_HEADS = 8
RWKV_IN = 1792
RWKV_GN_EPS = 64e-5
RWKV_DECAY_SCALE = float(np.exp(-0.5))
LOG2_E = float(np.log2(np.e))
GLA_HEADS = 4
GLA_GATE_RANK = 16
GLA_GATE_NORM = 16.0
SSD_HEADS = 8
SSD_GROUPS = 2
SSD_STATE = 64
SSD_CONV = 4
SSD_XBC = 768
N_EXPERTS = 16
N_EXPERT_GROUPS = 4
EXPERTS_PER_GROUP = 4
D_EXPERT = 512
LANE = 128
EVEN_COLS = 3456
ODD_COLS = 3328
MOE_ROWS = 512
EVEN_TILE, EVEN_ROWS = 64, 4
ODD_TILE, ODD_ROWS = 128, 1
VMEM_LIMIT = 48 * 1024 * 1024


def _dot(a, b):
    return jnp.dot(a.astype(BF16), b.astype(BF16), preferred_element_type=F32)


def _dot_nt(a, b):
    return lax.dot_general(a.astype(BF16), b.astype(BF16), (((1,), (1,)), ((), ())),
                           preferred_element_type=F32)


def _dot_tn(a, b):
    return lax.dot_general(a.astype(BF16), b.astype(BF16), (((0,), (0,)), ((), ())),
                           preferred_element_type=F32)


def _dot_exact(a, b):
    return jnp.dot(a, b, precision=HIGHEST, preferred_element_type=F32)


def _sigmoid(x):
    return 1.0 / (1.0 + jnp.exp(-x))


def _silu(x):
    return x * _sigmoid(x)


def _softplus(x):
    return jnp.maximum(x, 0.0) + jnp.log(1.0 + jnp.exp(-jnp.abs(x)))


def _log_sigmoid(x):
    return jnp.minimum(x, 0.0) - jnp.log(1.0 + jnp.exp(-jnp.abs(x)))


def _chunk_constants():
    t = np.arange(CHUNK)
    masks = []
    for shift in range(5, -1, -1):
        masks.append(((t[:, None] > t[None, :]) & (((t[:, None] ^ t[None, :]) >> shift) == 1)))
    masks.append(t[:, None] == t[None, :])
    return jnp.asarray(np.stack(masks).astype(np.float32))


def _head_indicator(width, seg):
    i = np.arange(width)
    return jnp.asarray((i[:, None] // seg == i[None, :] // seg).astype(np.float32), dtype=BF16)


def _ada_kernel(c_ref, w_ref, b_ref, o_ref):
    cond = _silu(c_ref[...])
    o_ref[0] = _dot_exact(cond, w_ref[0]) + b_ref[0]


def _ada(c, w_ada, b_ada):
    depth, d, n = w_ada.shape
    bsz = c.shape[0]
    tn = 1536
    return pl.pallas_call(
        _ada_kernel,
        grid=(depth, n // tn),
        in_specs=[pl.BlockSpec((bsz, d), lambda l, j: (0, 0)),
                  pl.BlockSpec((1, d, tn), lambda l, j: (l, 0, j)),
                  pl.BlockSpec((1, 1, tn), lambda l, j: (l, 0, j))],
        out_specs=pl.BlockSpec((1, bsz, tn), lambda l, j: (l, 0, j)),
        out_shape=jax.ShapeDtypeStruct((depth, bsz, n), F32),
        compiler_params=pltpu.CompilerParams(dimension_semantics=("parallel", "parallel"),
                                             vmem_limit_bytes=VMEM_LIMIT),
        name="ada",
    )(c, w_ada, b_ada.reshape(depth, 1, n))


def _modulated_norm(x, gain, scale, shift):
    ms = jnp.mean(x * x, axis=-1, keepdims=True)
    return (x * lax.rsqrt(ms + NORM_EPS)) * gain * (1.0 + scale) + shift


def _in_proj_kernel(x_ref, gain_ref, sc_ref, sh_ref, w_ref, o_ref, *, col_chunk):
    h = _modulated_norm(x_ref[...], gain_ref[...], sc_ref[0], sh_ref[0]).astype(BF16)
    n = o_ref.shape[1]
    for j in range(0, n, col_chunk):
        o_ref[:, j:j + col_chunk] = jnp.dot(h, w_ref[:, j:j + col_chunk],
                                            preferred_element_type=F32).astype(o_ref.dtype)


def _in_proj(x2, gain, scale, shift, w, seq, tm):
    t, d = x2.shape
    n = w.shape[1]
    col_chunk = next(n // parts for parts in (3, 2, 1) if n % (parts * LANE) == 0)
    per_b = seq // tm
    return pl.pallas_call(
        functools.partial(_in_proj_kernel, col_chunk=col_chunk),
        grid=(t // tm,),
        in_specs=[pl.BlockSpec((tm, d), lambda i: (i, 0)),
                  pl.BlockSpec((1, d), lambda i: (0, 0)),
                  pl.BlockSpec((1, 1, d), lambda i: (i // per_b, 0, 0)),
                  pl.BlockSpec((1, 1, d), lambda i: (i // per_b, 0, 0)),
                  pl.BlockSpec((d, n), lambda i: (0, 0))],
        out_specs=pl.BlockSpec((tm, n), lambda i: (i, 0)),
        out_shape=jax.ShapeDtypeStruct((t, n), BF16),
        compiler_params=pltpu.CompilerParams(dimension_semantics=("parallel",),
                                             vmem_limit_bytes=VMEM_LIMIT),
        name="in_proj",
    )(x2, gain, scale, shift, w)


def _cumsum_rows(x, on_mxu):
    if not on_mxu:
        rows = lax.broadcasted_iota(jnp.int32, x.shape, 0)
        step = 1
        while step < CHUNK:
            x = x + jnp.where(rows >= step, pltpu.roll(x, step, axis=0), 0.0)
            step *= 2
        return x
    n = x.shape[1]
    hi = x.astype(BF16)
    lo = (x - hi.astype(F32)).astype(BF16)
    row = lax.broadcasted_iota(jnp.int32, (CHUNK, CHUNK), 0)
    col = lax.broadcasted_iota(jnp.int32, (CHUNK, CHUNK), 1)
    tril = jnp.where(col <= row, 1.0, 0.0).astype(BF16)
    both = jnp.dot(tril, jnp.concatenate([hi, lo], axis=1), preferred_element_type=F32)
    return both[:, 0:n] + both[:, n:]


def _level_refs(b):
    cols = b.shape[1]
    rows = lax.broadcasted_iota(jnp.int32, b.shape, 0)

    def spread(offset, span):
        pieces = [jnp.broadcast_to(b[s + offset:s + offset + 1], (span, cols)) for s in range(0, CHUNK, span)]
        return pieces[0] if len(pieces) == 1 else jnp.concatenate(pieces, axis=0)

    refs = [spread(n // 2 - 1, n) for n in (64, 32, 16, 8)]
    refs.append(jnp.where((rows & 7) < 4, spread(1, 8), spread(5, 8)))
    refs.append(jnp.where((rows & 1) == 1, pltpu.roll(b, 1, axis=0), b))
    return refs


def _gla_chunk(q, k, v, g, st, pair_masks, heads, dk, dv, cumsum_on_mxu):
    hs = range(heads)
    ks = [slice(h * dk, (h + 1) * dk) for h in hs]
    vs = [slice(h * dv, (h + 1) * dv) for h in hs]
    b = _cumsum_rows(g, cumsum_on_mxu)
    b_last = b[CHUNK - 1:CHUNK]
    q_in = q * jnp.exp2(b)
    k_st = k * jnp.exp2(b_last - b)
    decay = jnp.exp2(b_last)
    yield

    refs = _level_refs(b)
    scores = [None] * heads
    for lvl in range(7):
        if lvl < 6:
            e = jnp.exp2(-jnp.abs(b - refs[lvl]))
            qe, ke = q * e, k * e
        else:
            qe, ke = q, k
        keep = pair_masks[lvl] > 0.5
        for h in hs:
            p = _dot_nt(qe[:, ks[h]], ke[:, ks[h]])
            scores[h] = jnp.where(keep, p, 0.0) if scores[h] is None else jnp.where(keep, p, scores[h])
        yield
    o_inter = [_dot_nt(q_in[:, ks[h]], st[h]) for h in hs]
    o_intra = [_dot(scores[h], v[:, vs[h]]) for h in hs]
    yield
    st_new = [st[h] * decay[:, ks[h]] + _dot_tn(v[:, vs[h]], k_st[:, ks[h]]) for h in hs]
    return jnp.concatenate([o_inter[h] + o_intra[h] for h in hs], axis=-1), st_new


def _interleave(*stages):
    results = [None] * len(stages)
    live = list(range(len(stages)))
    while live:
        for i in list(live):
            try:
                next(stages[i])
            except StopIteration as stop:
                results[i] = stop.value
                live.remove(i)
    return results


def _head_rms(o, gain, heads, dv):
    outs = []
    for h in range(heads):
        oh = o[:, h * dv:(h + 1) * dv]
        ms = jnp.mean(oh * oh, axis=-1, keepdims=True)
        outs.append(oh * lax.rsqrt(ms + NORM_EPS) * gain)
    return jnp.concatenate(outs, axis=-1)


def _rwkv_chunk(r, kk, a, kt, v, logw, s0):
    b = _cumsum_rows(logw, True)
    b_last = b[CHUNK - 1:CHUNK]
    e_neg = jnp.exp2(-b)
    e_end = jnp.exp2(b_last - b)
    decay = jnp.exp2(b_last)
    beta = a * kk
    k_bar = kk * jnp.exp2(b - logw)
    r_bar = r * jnp.exp2(b)
    beta_t, k_t = beta * e_neg, kt * e_neg
    beta_hat, k_hat = beta * e_end, kt * e_end
    row = lax.broadcasted_iota(jnp.int32, (CHUNK, CHUNK), 0)
    col = lax.broadcasted_iota(jnp.int32, (CHUNK, CHUNK), 1)
    strict = col < row
    incl = col <= row
    same_blk = (row >> 4) == (col >> 4)
    in_blk = jnp.logical_and(strict, same_blk)
    off_blk = jnp.logical_and(strict, jnp.logical_not(same_blk))
    eye = (row == col).astype(F32)
    hs = range(RWKV_HEADS)
    sls = [slice(h * HEAD_DIM, (h + 1) * HEAD_DIM) for h in hs]
    kr = [jnp.concatenate([k_bar[:, sl], r_bar[:, sl]], axis=0).astype(BF16) for sl in sls]
    bk = [jnp.concatenate([beta_t[:, sl], k_t[:, sl]], axis=0) for sl in sls]
    v_b = v.astype(BF16)
    vh = [v_b[:, sl] for sl in sls]
    yield
    m1 = [_dot_nt(kr[h], bk[h]) for h in hs]
    yield
    m2 = [_dot_nt(kr[h], s0[h]) for h in hs]
    yield
    x1 = [jnp.where(in_blk, m[0:CHUNK, 0:CHUNK], 0.0) for m in m1]
    a_off = [jnp.where(off_blk, m[0:CHUNK, 0:CHUNK], 0.0) for m in m1]
    b_m = [jnp.where(strict, m[0:CHUNK, CHUNK:], 0.0) for m in m1]
    cb_m = [jnp.where(incl, m[CHUNK:, 0:CHUNK], 0.0) for m in m1]
    ck_m = [jnp.where(incl, m[CHUNK:, CHUNK:], 0.0) for m in m1]
    rhs = [m2[h][0:CHUNK] + _dot(b_m[h], vh[h]) for h in hs]
    lo, hi = slice(0, CHUNK), slice(CHUNK, 2 * CHUNK)
    side = lambda left, right: jnp.concatenate([left, right], axis=1)
    x2 = [_dot(x, x) for x in x1]
    yield
    p = [eye - x for x in x1]
    w = [_dot(x2[h], side(p[h], x2[h])) for h in hs]
    yield
    p = [p[h] + w[h][:, lo] for h in hs]
    x4 = [w[h][:, hi] for h in hs]
    w = [_dot(x4[h], side(p[h], x4[h])) for h in hs]
    yield
    p = [p[h] + w[h][:, lo] for h in hs]
    t_d = [p[h] + _dot(w[h][:, hi], p[h]) for h in hs]
    yield
    nz = [_dot(t_d[h], side(a_off[h], rhs[h])) for h in hs]
    yield
    w = [_dot(nz[h][:, lo], nz[h]) for h in hs]
    yield
    y1 = [w[h][:, hi] - nz[h][:, hi] for h in hs]
    neg_u = [y1[h] + _dot(w[h][:, lo], y1[h]) for h in hs]
    vu = [jnp.concatenate([vh[h], neg_u[h].astype(BF16)], axis=0) for h in hs]
    yield
    outs = [m2[h][CHUNK:] + _dot(side(ck_m[h], cb_m[h]), vu[h]) for h in hs]
    yield
    s_new = [s0[h] * decay[:, sls[h]]
             + _dot_tn(vu[h], jnp.concatenate([k_hat[:, sls[h]], beta_hat[:, sls[h]]], axis=0)) for h in hs]
    return jnp.concatenate(outs, axis=-1), s_new


def _head_sums(x, ind):
    return jnp.concatenate([_dot(x[:, i:i + LANE], ind) for i in range(0, x.shape[1], LANE)], axis=1)


def _mix_even_kernel(u_ref, mu_ref, w0_ref, wup_ref, a0_ref, aup_ref, gup_ref, kk_ref, ka_ref,
                     rk_ref, lng_ref, lnb_ref, alup_ref, albias_ref, bng_ref, ind_ref,
                     pmask_ref, y_ref, s_ref, g_ref, prev_ref):
    @pl.when(pl.program_id(1) == 0)
    def _():
        s_ref[...] = jnp.zeros_like(s_ref)
        g_ref[...] = jnp.zeros_like(g_ref)
        prev_ref[...] = jnp.zeros_like(prev_ref)

    pair_masks = pmask_ref[...]
    ind = ind_ref[...]
    n_rows = u_ref.shape[0]
    s_state = [[s_ref[b * RWKV_HEADS + h] for h in range(RWKV_HEADS)] for b in range(n_rows)]
    g_state = [[g_ref[b * GLA_HEADS + h] for h in range(GLA_HEADS)] for b in range(n_rows)]
    prev = [prev_ref[b] for b in range(n_rows)]

    def rwkv_stages(b, rs, prev, s_state):
        ua = u_ref[b, rs, 0:RWKV_IN].astype(F32)
        rows = lax.broadcasted_iota(jnp.int32, ua.shape, 0)
        shifted = jnp.where(rows == 0, prev, pltpu.roll(ua, 1, axis=0))
        xa = ua + mu_ref[...] * (shifted - ua)
        r, k, v = xa[:, 0:512], xa[:, 512:1024], xa[:, 1024:1536]
        wd, ad, gd = xa[:, 1536:1600], xa[:, 1600:1664], xa[:, 1664:1792]
        logw = (-RWKV_DECAY_SCALE * LOG2_E) * _sigmoid(w0_ref[...] + _dot(jnp.tanh(wd), wup_ref[...]))
        a = _sigmoid(a0_ref[...] + _dot(ad, aup_ref[...]))
        gate = _dot(_sigmoid(gd), gup_ref[...])
        yield
        kk = k * kk_ref[...]
        kk = kk * lax.rsqrt(_head_sums(kk * kk, ind) + 1e-12)
        kt = k * (1.0 + (a - 1.0) * ka_ref[...])
        yield
        y, s_state = yield from _rwkv_chunk(r, kk, a, kt, v, logw, s_state)
        yield
        mean = _head_sums(y, ind) * (1.0 / HEAD_DIM)
        yc = y - mean
        yield
        var = _head_sums(yc * yc, ind) * (1.0 / HEAD_DIM)
        y = yc * lax.rsqrt(var + RWKV_GN_EPS) * lng_ref[...] + lnb_ref[...]
        y = y + _head_sums(r * kt * rk_ref[...], ind) * v
        y_ref[b, rs, 0:MIX_HALF] = (y * gate).astype(y_ref.dtype)
        return ua[CHUNK - 1:CHUNK], s_state

    def gla_stages(b, rs, g_state):
        ub = u_ref[b, rs, RWKV_IN:].astype(F32)
        q, kg, vg = ub[:, 0:256] * (HEAD_DIM ** -0.5), ub[:, 256:512], ub[:, 512:1024]
        alpha, gg = ub[:, 1024:1152], ub[:, 1152:1664]
        log_a = _log_sigmoid(_dot(alpha, alup_ref[...]) + albias_ref[...]) * (LOG2_E / GLA_GATE_NORM)
        yield
        o, g_state = yield from _gla_chunk(q, kg, vg, log_a, g_state, pair_masks, GLA_HEADS, 64, 128, True)
        o = _head_rms(o, bng_ref[...], GLA_HEADS, 128)
        y_ref[b, rs, MIX_HALF:] = (o * _silu(gg)).astype(y_ref.dtype)
        return g_state

    for c in range(u_ref.shape[1] // CHUNK):
        rs = slice(c * CHUNK, (c + 1) * CHUNK)
        done = _interleave(*[stages for b in range(n_rows)
                             for stages in (rwkv_stages(b, rs, prev[b], s_state[b]), gla_stages(b, rs, g_state[b]))])
        for b in range(n_rows):
            (prev[b], s_state[b]), g_state[b] = done[2 * b], done[2 * b + 1]
    for b in range(n_rows):
        for h in range(RWKV_HEADS):
            s_ref[b * RWKV_HEADS + h] = s_state[b][h]
        for h in range(GLA_HEADS):
            g_ref[b * GLA_HEADS + h] = g_state[b][h]
        prev_ref[b] = prev[b]


def _row(p):
    return p.reshape(1, -1).astype(F32)


def _mix_even(u3, p, consts):
    bsz, seq, n = u3.shape
    pair_masks, ind = consts
    small = [p["mu"], p["w0"], p["w_up"], p["a0"], p["a_up"], p["g_up"], p["k_k"], p["k_a"], p["r_k"],
             p["ln_gain"], p["ln_bias"], p["alpha_up"], p["alpha_bias"], p["b_norm_gain"],
             ind, pair_masks]

    def full(arr):
        nd = arr.ndim
        return pl.BlockSpec(arr.shape, lambda b, s, _nd=nd: (0,) * _nd)

    rows = EVEN_ROWS if bsz % EVEN_ROWS == 0 else 1
    return pl.pallas_call(
        _mix_even_kernel,
        grid=(bsz // rows, seq // EVEN_TILE),
        in_specs=[pl.BlockSpec((rows, EVEN_TILE, n), lambda b, s: (b, s, 0))] + [full(a) for a in small],
        out_specs=pl.BlockSpec((rows, EVEN_TILE, D_MODEL), lambda b, s: (b, s, 0)),
        out_shape=jax.ShapeDtypeStruct((bsz, seq, D_MODEL), BF16),
        scratch_shapes=[pltpu.VMEM((rows * RWKV_HEADS, HEAD_DIM, HEAD_DIM), F32),
                        pltpu.VMEM((rows * GLA_HEADS, 128, 64), F32),
                        pltpu.VMEM((rows, 1, RWKV_IN), F32)],
        compiler_params=pltpu.CompilerParams(dimension_semantics=("parallel", "arbitrary"),
                                             vmem_limit_bytes=VMEM_LIMIT),
        name="mix_even",
    )(u3, *small)


def _mix_odd_kernel(u_ref, clb_ref, cng_ref, convw_ref, convb_ref, dtb_ref, alog_ref, skip_ref,
                    dng_ref, pmask_ref, y_ref, h_ref, d_ref, tail_ref, *, layer):
    @pl.when(pl.program_id(1) == 0)
    def _():
        h_ref[...] = jnp.zeros_like(h_ref)
        d_ref[...] = jnp.zeros_like(d_ref)
        tail_ref[...] = jnp.zeros_like(tail_ref)

    pair_masks = pmask_ref[...]
    c_lb = clb_ref[...]
    c_exp = jnp.exp(c_lb - jnp.max(c_lb, axis=0, keepdims=True))
    lb = jnp.sum(c_exp[1:layer + 1], axis=0, keepdims=True) / jnp.sum(c_exp, axis=0, keepdims=True)
    n_rows = u_ref.shape[0]
    h_state = [[h_ref[b * 4 + h] for h in range(4)] for b in range(n_rows)]
    d_state = [[d_ref[b * SSD_GROUPS + grp] for grp in range(SSD_GROUPS)] for b in range(n_rows)]
    tail = [tail_ref[b] for b in range(n_rows)]

    def hgrn_stages(b, rs, h_state):
        u = u_ref[b, rs, 0:1536].astype(F32)
        q, fr, iv, g = u[:, 0:256], u[:, 256:512], u[:, 512:1024], u[:, 1024:1536]
        f = lb + (1.0 - lb) * _sigmoid(fr)
        yield
        o, h_state = yield from _gla_chunk(q, 1.0 - f, iv, jnp.log2(f), h_state, pair_masks, 4, 64, 128, False)
        y_ref[b, rs, 0:MIX_HALF] = (_head_rms(o, cng_ref[...], 4, 128) * _silu(g)).astype(y_ref.dtype)
        return h_state

    def ssd_stages(b, rs, tail, d_state):
        u = u_ref[b, rs, 1536:].astype(F32)
        y, d_state, tail = yield from _ssd_chunk(u, tail, d_state, convw_ref, convb_ref[...], dtb_ref[...],
                                                 alog_ref[...], skip_ref[...], dng_ref[...])
        y_ref[b, rs, MIX_HALF:] = y.astype(y_ref.dtype)
        return tail, d_state

    for c in range(u_ref.shape[1] // CHUNK):
        rs = slice(c * CHUNK, (c + 1) * CHUNK)
        h_state = _interleave(*[hgrn_stages(b, rs, h_state[b]) for b in range(n_rows)])
        done = _interleave(*[ssd_stages(b, rs, tail[b], d_state[b]) for b in range(n_rows)])
        for b in range(n_rows):
            tail[b], d_state[b] = done[b]
    for b in range(n_rows):
        for h in range(4):
            h_ref[b * 4 + h] = h_state[b][h]
        for grp in range(SSD_GROUPS):
            d_ref[b * SSD_GROUPS + grp] = d_state[b][grp]
        tail_ref[b] = tail[b]


def _ssd_chunk(u, tail, st, convw_ref, conv_b, dt_bias, a_log, skip, norm_gain):
    z, xbc, dt_raw = u[:, 0:512], u[:, 512:1280], u[:, 1280:1792]
    new_tail = xbc[CHUNK - 8:CHUNK]
    rows8 = lax.broadcasted_iota(jnp.int32, (8, SSD_XBC), 0)
    conv = xbc * convw_ref[SSD_CONV - 1:SSD_CONV] + conv_b
    for back in range(1, SSD_CONV):
        rolled = pltpu.roll(xbc, back, axis=0)
        head8 = jnp.where(rows8 < back, pltpu.roll(tail, back, axis=0), rolled[0:8])
        shifted = jnp.concatenate([head8, rolled[8:]], axis=0)
        conv = conv + shifted * convw_ref[SSD_CONV - 1 - back:SSD_CONV - back]
    xbc = _silu(conv)
    yield
    xs, bmat, cmat = xbc[:, 0:512], xbc[:, 512:640], xbc[:, 640:768]
    dt = _softplus(dt_raw + dt_bias)
    da = dt * (-LOG2_E * jnp.exp(a_log))
    cum = _cumsum_rows(da, False)
    yield
    cum_last = cum[CHUNK - 1:CHUNK]
    e_cum = jnp.exp2(cum)
    xdt = xs * dt
    x_end = xdt * jnp.exp2(cum_last - cum)
    decay = jnp.exp2(cum_last)
    yield
    row = lax.broadcasted_iota(jnp.int32, (CHUNK, CHUNK), 0)
    col = lax.broadcasted_iota(jnp.int32, (CHUNK, CHUNK), 1)
    causal = col <= row
    groups = range(SSD_GROUPS)
    heads = range(SSD_HEADS)
    per_group = SSD_HEADS // SSD_GROUPS
    gs = [slice(grp * 256, (grp + 1) * 256) for grp in groups]
    ns = [slice(grp * SSD_STATE, (grp + 1) * SSD_STATE) for grp in groups]
    hs = [slice(h * HEAD_DIM, (h + 1) * HEAD_DIM) for h in heads]
    cb = [_dot_nt(cmat[:, ns[grp]], bmat[:, ns[grp]]) for grp in groups]
    y_off = [_dot(cmat[:, ns[grp]], st[grp]) for grp in groups]
    st_new = [st[grp] * decay[:, gs[grp]] + _dot_tn(bmat[:, ns[grp]], x_end[:, gs[grp]]) for grp in groups]
    yield
    cum_h = [cum[:, sl] for sl in hs]
    seg = [jnp.where(causal, jnp.exp2(ch - ch.T), 0.0) for ch in cum_h]
    yield
    y_diag = [_dot(cb[h // per_group] * seg[h], xdt[:, hs[h]]) for h in heads]
    yield
    y = (jnp.concatenate(y_diag, axis=-1) + jnp.concatenate(y_off, axis=-1) * e_cum + skip * xs)
    y = y * _silu(z)
    parts = []
    for grp in range(SSD_GROUPS):
        yg = y[:, grp * 256:(grp + 1) * 256]
        ms = jnp.mean(yg * yg, axis=-1, keepdims=True)
        parts.append(yg * lax.rsqrt(ms + NORM_EPS))
    return jnp.concatenate(parts, axis=-1) * norm_gain, st_new, new_tail


def _mix_odd(u3, p, layer, consts):
    bsz, seq, n = u3.shape
    pair_masks = consts
    small = [p["c_lb"], p["c_norm_gain"], p["conv_w"], p["conv_b"], p["dt_bias"], p["a_log"], p["skip"],
             p["d_norm_gain"], pair_masks]

    def full(arr):
        nd = arr.ndim
        return pl.BlockSpec(arr.shape, lambda b, s, _nd=nd: (0,) * _nd)

    rows = ODD_ROWS if bsz % ODD_ROWS == 0 else 1
    return pl.pallas_call(
        functools.partial(_mix_odd_kernel, layer=layer),
        grid=(bsz // rows, seq // ODD_TILE),
        in_specs=[pl.BlockSpec((rows, ODD_TILE, n), lambda b, s: (b, s, 0))] + [full(a) for a in small],
        out_specs=pl.BlockSpec((rows, ODD_TILE, D_MODEL), lambda b, s: (b, s, 0)),
        out_shape=jax.ShapeDtypeStruct((bsz, seq, D_MODEL), BF16),
        scratch_shapes=[pltpu.VMEM((rows * 4, 128, 64), F32),
                        pltpu.VMEM((rows * SSD_GROUPS, SSD_STATE, 256), F32),
                        pltpu.VMEM((rows, 8, SSD_XBC), F32)],
        compiler_params=pltpu.CompilerParams(dimension_semantics=("parallel", "arbitrary"),
                                             vmem_limit_bytes=VMEM_LIMIT),
        name="mix_odd",
    )(u3, *small)


def _route_kernel(y_ref, wo_ref, x_ref, gm_ref, gain_ref, sc_ref, sh_ref, wr_ref, bias_ref, tri_ref,
                  xo_ref, h_ref, e_ref, gt_ref, r_ref, cnt_ref):
    x = x_ref[...] + gm_ref[0] * jnp.dot(y_ref[...], wo_ref[...], preferred_element_type=F32)
    xo_ref[...] = x
    h = _modulated_norm(x, gain_ref[...], sc_ref[0], sh_ref[0])
    wr = wr_ref[...]
    wr_hi = wr.astype(BF16)
    wr_lo = (wr - wr_hi.astype(F32)).astype(BF16)
    h_hi = h.astype(BF16)
    h_ref[...] = h_hi
    h_lo = (h - h_hi.astype(F32)).astype(BF16)
    both = _dot_nt(jnp.concatenate([wr_hi, wr_lo], axis=0), h_hi)
    logits = both[0:N_EXPERTS] + both[N_EXPERTS:] + _dot_nt(wr_hi, h_lo)
    score = _sigmoid(logits)
    sel = score + bias_ref[...]
    gscore = []
    for grp in range(N_EXPERT_GROUPS):
        a, b, c, d = [sel[grp * 4 + j:grp * 4 + j + 1] for j in range(4)]
        hi1, lo1, hi2, lo2 = jnp.maximum(a, b), jnp.minimum(a, b), jnp.maximum(c, d), jnp.minimum(c, d)
        gscore.append(jnp.maximum(hi1, hi2) + jnp.maximum(jnp.minimum(hi1, hi2), jnp.maximum(lo1, lo2)))
    best, gidx = gscore[0], jnp.zeros_like(gscore[0], dtype=jnp.int32)
    for grp in range(1, N_EXPERT_GROUPS):
        better = gscore[grp] > best
        gidx = jnp.where(better, grp, gidx)
        best = jnp.where(better, gscore[grp], best)
    vals, raw = [], []
    for j in range(EXPERTS_PER_GROUP):
        vj, rj = sel[j:j + 1], score[j:j + 1]
        for grp in range(1, N_EXPERT_GROUPS):
            vj = jnp.where(gidx == grp, sel[grp * 4 + j:grp * 4 + j + 1], vj)
            rj = jnp.where(gidx == grp, score[grp * 4 + j:grp * 4 + j + 1], rj)
        vals.append(vj)
        raw.append(rj)
    i1, m1, g1 = jnp.zeros_like(gidx), vals[0], raw[0]
    for j in range(1, EXPERTS_PER_GROUP):
        better = vals[j] > m1
        i1 = jnp.where(better, j, i1)
        m1 = jnp.where(better, vals[j], m1)
        g1 = jnp.where(better, raw[j], g1)
    i2, m2, g2 = jnp.zeros_like(gidx), jnp.full_like(m1, -jnp.inf), jnp.zeros_like(m1)
    for j in range(EXPERTS_PER_GROUP):
        better = jnp.logical_and(i1 != j, vals[j] > m2)
        i2 = jnp.where(better, j, i2)
        m2 = jnp.where(better, vals[j], m2)
        g2 = jnp.where(better, raw[j], g2)
    total = g1 + g2
    e1 = gidx * EXPERTS_PER_GROUP + i1
    e2 = gidx * EXPERTS_PER_GROUP + i2
    e_ref[0:1, :] = e1
    e_ref[1:2, :] = e2
    gt_ref[0:1, :] = g1 / total
    gt_ref[1:2, :] = g2 / total
    @pl.when(pl.program_id(0) == 0)
    def _():
        cnt_ref[...] = jnp.zeros_like(cnt_ref)

    eid = lax.broadcasted_iota(jnp.int32, logits.shape, 0)
    is1, is2 = eid == e1, eid == e2
    member = jnp.where(jnp.logical_or(is1, is2), 1.0, 0.0)
    before = jnp.dot(member.astype(BF16), tri_ref[...], preferred_element_type=F32) - member
    base = cnt_ref[:, 0:1] + before
    r_ref[0:1, :] = jnp.sum(jnp.where(is1, base, 0.0), axis=0, keepdims=True).astype(jnp.int32)
    r_ref[1:2, :] = jnp.sum(jnp.where(is2, base, 0.0), axis=0, keepdims=True).astype(jnp.int32)
    cnt_ref[...] = cnt_ref[...] + jnp.sum(member, axis=1, keepdims=True)


def _out_route(y2, w_out, x2, gate_m, gain, scale, shift, wr_t, bias_col, seq, tm):
    t, d = x2.shape
    per_b = seq // tm
    tri = jnp.asarray(np.triu(np.ones((tm, tm), np.float32)), dtype=BF16)
    per_batch = pl.BlockSpec((1, 1, d), lambda i: (i // per_b, 0, 0))
    rows = pl.BlockSpec((tm, d), lambda i: (i, 0))
    pair = pl.BlockSpec((2, tm), lambda i: (0, i))
    return pl.pallas_call(
        _route_kernel,
        grid=(t // tm,),
        in_specs=[rows,
                  pl.BlockSpec((d, d), lambda i: (0, 0)),
                  rows,
                  per_batch,
                  pl.BlockSpec((1, d), lambda i: (0, 0)),
                  per_batch,
                  per_batch,
                  pl.BlockSpec((N_EXPERTS, d), lambda i: (0, 0)),
                  pl.BlockSpec((N_EXPERTS, 1), lambda i: (0, 0)),
                  pl.BlockSpec((tm, tm), lambda i: (0, 0))],
        out_specs=[rows, rows, pair, pair, pair, pl.BlockSpec((N_EXPERTS, LANE), lambda i: (0, 0))],
        out_shape=[jax.ShapeDtypeStruct((t, d), F32),
                   jax.ShapeDtypeStruct((t, d), BF16),
                   jax.ShapeDtypeStruct((2, t), jnp.int32),
                   jax.ShapeDtypeStruct((2, t), F32),
                   jax.ShapeDtypeStruct((2, t), jnp.int32),
                   jax.ShapeDtypeStruct((N_EXPERTS, LANE), F32)],
        compiler_params=pltpu.CompilerParams(dimension_semantics=("arbitrary",),
                                             vmem_limit_bytes=VMEM_LIMIT),
        name="out_route",
    )(y2, w_out, x2, gate_m, gain, scale, shift, wr_t, bias_col, tri)


def _expert_kernel(ib_ref, ie_ref, lo_ref, hi_ref, x_ref, wg_ref, wu_ref, wd_ref, o_ref, wg_s, wu_s, wd_s):
    i = pl.program_id(0)
    blk, lo, hi = ib_ref[i], lo_ref[i], hi_ref[i]

    @pl.when(jnp.logical_or(i == 0, ie_ref[i] != ie_ref[jnp.maximum(i - 1, 0)]))
    def _():
        wg_s[...] = wg_ref[0, 0].astype(BF16)
        wu_s[...] = wu_ref[0, 0].astype(BF16)
        wd_s[...] = wd_ref[0, 0].astype(BF16)

    @pl.when(hi > lo)
    def _():
        x = x_ref[...]
        gate = jnp.dot(x, wg_s[...], preferred_element_type=F32)
        up = jnp.dot(x, wu_s[...], preferred_element_type=F32)
        act = (_silu(gate) * up).astype(BF16)
        y = jnp.dot(act, wd_s[...], preferred_element_type=F32).astype(o_ref.dtype)
        rows = blk * MOE_ROWS + lax.broadcasted_iota(jnp.int32, y.shape, 0)
        mine = jnp.logical_and(rows >= lo, rows < hi)

        @pl.when(lo == blk * MOE_ROWS)
        def _():
            o_ref[...] = jnp.where(mine, y, jnp.zeros_like(y))

        @pl.when(lo != blk * MOE_ROWS)
        def _():
            o_ref[...] = jnp.where(mine, y, o_ref[...])


def _experts(xb, w_gate, w_up, w_down, layer, items):
    n_rows, d = xb.shape
    de = w_gate.shape[-1]
    n_items = items[0].shape[0]
    grid_spec = pltpu.PrefetchScalarGridSpec(
        num_scalar_prefetch=4,
        grid=(n_items,),
        in_specs=[pl.BlockSpec((MOE_ROWS, d), lambda i, ib, ie, lo, hi: (ib[i], 0)),
                  pl.BlockSpec((1, 1, d, de), lambda i, ib, ie, lo, hi: (layer, ie[i], 0, 0)),
                  pl.BlockSpec((1, 1, d, de), lambda i, ib, ie, lo, hi: (layer, ie[i], 0, 0)),
                  pl.BlockSpec((1, 1, de, d), lambda i, ib, ie, lo, hi: (layer, ie[i], 0, 0))],
        out_specs=pl.BlockSpec((MOE_ROWS, d), lambda i, ib, ie, lo, hi: (ib[i], 0)),
        scratch_shapes=[pltpu.VMEM((d, de), BF16), pltpu.VMEM((d, de), BF16), pltpu.VMEM((de, d), BF16)],
    )
    return pl.pallas_call(
        _expert_kernel,
        grid_spec=grid_spec,
        out_shape=jax.ShapeDtypeStruct((n_rows, d), BF16),
        compiler_params=pltpu.CompilerParams(dimension_semantics=("arbitrary",),
                                             vmem_limit_bytes=VMEM_LIMIT),
        name="experts",
    )(*items, xb, w_gate, w_up, w_down)


def _combine_kernel(x_ref, y0_ref, y1_ref, w_ref, g_ref, fg_ref, o_ref, *, final):
    w = w_ref[...]
    moe = y0_ref[...].astype(F32) * w[:, 0:1] + y1_ref[...].astype(F32) * w[:, 1:2]
    x = x_ref[...] + g_ref[0] * moe
    if final:
        x = x * lax.rsqrt(jnp.mean(x * x, axis=-1, keepdims=True) + NORM_EPS) * fg_ref[...]
    o_ref[...] = x


def _combine(x2, y0, y1, weights, gate, final_gain, final, seq, tm):
    t, d = x2.shape
    per_b = seq // tm
    return pl.pallas_call(
        functools.partial(_combine_kernel, final=final),
        grid=(t // tm,),
        in_specs=[pl.BlockSpec((tm, d), lambda i: (i, 0)),
                  pl.BlockSpec((tm, d), lambda i: (i, 0)),
                  pl.BlockSpec((tm, d), lambda i: (i, 0)),
                  pl.BlockSpec((tm, 2), lambda i: (i, 0)),
                  pl.BlockSpec((1, 1, d), lambda i: (i // per_b, 0, 0)),
                  pl.BlockSpec((1, d), lambda i: (0, 0))],
        out_specs=pl.BlockSpec((tm, d), lambda i: (i, 0)),
        out_shape=jax.ShapeDtypeStruct((t, d), F32),
        compiler_params=pltpu.CompilerParams(dimension_semantics=("parallel",),
                                             vmem_limit_bytes=VMEM_LIMIT),
        name="combine",
    )(x2, y0, y1, weights, gate, final_gain)


def _out_moe(y2, w_out, x2, gate_m, gain, scale, shift, gate, wr_t, bias_col, w_gate, w_up, w_down, layer,
             final_gain, final, seq, tm):
    t, d = x2.shape
    x2, h, experts, weights, ranks, cnt = _out_route(y2, w_out, x2, gate_m, gain, scale, shift, wr_t, bias_col,
                                                     seq, tm)
    counts = cnt[:, 0].astype(jnp.int32)
    end = jnp.cumsum(counts)
    start = end - counts
    dest = ranks
    for e in range(N_EXPERTS):
        dest = dest + jnp.where(experts == e, start[e], 0)
    tok = jnp.arange(t, dtype=jnp.uint32)
    keys = dest.astype(jnp.uint32) * jnp.uint32(t) + tok[None, :]
    slot_tok = (jnp.sort(keys.reshape(-1)) % jnp.uint32(t)).astype(jnp.int32)
    xb = jnp.take(h, slot_tok, axis=0, mode="clip")
    n_blocks = 2 * t // MOE_ROWS
    n_items = n_blocks + N_EXPERTS - 1
    first_blk = start // MOE_ROWS
    per_expert = jnp.where(counts > 0, (end - 1) // MOE_ROWS - first_blk + 1, 0)
    item_end = jnp.cumsum(per_expert)
    item_start = item_end - per_expert
    idx = jnp.arange(n_items, dtype=jnp.int32)
    item_e = jnp.zeros((n_items,), jnp.int32)
    for e in range(N_EXPERTS - 1):
        item_e = item_e + (idx >= item_end[e]).astype(jnp.int32)
    pick = lambda table: sum(jnp.where(item_e == e, table[e], 0) for e in range(N_EXPERTS))
    valid = idx < item_end[-1]
    item_blk = jnp.where(valid, pick(first_blk) + idx - pick(item_start), n_blocks - 1)
    item_lo = jnp.where(valid, jnp.maximum(pick(start), item_blk * MOE_ROWS), 0)
    item_hi = jnp.where(valid, jnp.minimum(pick(end), (item_blk + 1) * MOE_ROWS), 0)
    yb = _experts(xb, w_gate, w_up, w_down, layer, (item_blk, item_e, item_lo, item_hi))
    y0 = jnp.take(yb, dest[0], axis=0, mode="clip", unique_indices=True)
    y1 = jnp.take(yb, dest[1], axis=0, mode="clip", unique_indices=True)
    return _combine(x2, y0, y1, weights.T, gate, final_gain, final, seq, tm)


def kernel(x, c, norm_gain, w_ada, b_ada, w_in_even, w_in_odd, w_out, a_mu, a_w0, a_w_up, a_a0, a_a_up,
           a_g_up, a_k_k, a_k_a, a_r_k, a_ln_gain, a_ln_bias, b_alpha_up, b_alpha_bias, b_norm_gain, c_lb,
           c_norm_gain, d_conv_w, d_conv_b, d_dt_bias, d_a_log, d_skip, d_norm_gain, w_router, router_bias,
           w_gate, w_up, w_down, final_gain):
    bsz, seq, d = x.shape
    depth = w_ada.shape[0]
    t = bsz * seq
    tm = min(512, seq)
    pair_masks = _chunk_constants()
    ind = _head_indicator(LANE, HEAD_DIM)

    mods = _ada(c, w_ada, b_ada).reshape(depth, bsz, 6, 1, d)
    wr_t = w_router.T
    bias_col = router_bias.reshape(N_EXPERTS, 1)

    x2 = x.reshape(t, d)
    for l in range(depth):
        j = l // 2
        sh_m, sc_m, g_m, sh_f, sc_f, g_f = [mods[l, :, i] for i in range(6)]
        gain_m, gain_f = norm_gain[l, 0].reshape(1, d), norm_gain[l, 1].reshape(1, d)
        if l % 2 == 0:
            wl = w_in_even[j].astype(BF16)
            cut = RWKV_IN + 1024 + GLA_GATE_RANK
            w = jnp.zeros((d, EVEN_COLS), BF16).at[:, :cut].set(wl[:, :cut])
            w = w.at[:, cut + LANE - GLA_GATE_RANK:].set(wl[:, cut:])
            u = _in_proj(x2, gain_m, sc_m, sh_m, w, seq, tm)
            alpha_up = jnp.concatenate([b_alpha_up[j], jnp.zeros((LANE - GLA_GATE_RANK, 256), F32)], axis=0)
            p = dict(mu=_row(a_mu[j]), w0=_row(a_w0[j]), w_up=a_w_up[j], a0=_row(a_a0[j]), a_up=a_a_up[j],
                     g_up=a_g_up[j], k_k=_row(a_k_k[j]), k_a=_row(a_k_a[j]), r_k=_row(a_r_k[j]),
                     ln_gain=_row(a_ln_gain[j]), ln_bias=_row(a_ln_bias[j]), alpha_up=alpha_up,
                     alpha_bias=_row(b_alpha_bias[j]), b_norm_gain=_row(b_norm_gain[j]))
            y = _mix_even(u.reshape(bsz, seq, EVEN_COLS), p, (pair_masks, ind))
        else:
            wl = w_in_odd[j].astype(BF16)
            n_main = wl.shape[1] - SSD_HEADS
            w = jnp.zeros((d, ODD_COLS), BF16).at[:, :n_main].set(wl[:, :n_main])
            w = w.at[:, n_main:].set(jnp.repeat(wl[:, n_main:], HEAD_DIM, axis=1))
            u = _in_proj(x2, gain_m, sc_m, sh_m, w, seq, tm)
            p = dict(c_lb=c_lb.astype(F32), c_norm_gain=_row(c_norm_gain[j]), conv_w=d_conv_w[j],
                     conv_b=_row(d_conv_b[j]), dt_bias=_row(jnp.repeat(d_dt_bias[j], HEAD_DIM)),
                     a_log=_row(jnp.repeat(d_a_log[j], HEAD_DIM)), skip=_row(jnp.repeat(d_skip[j], HEAD_DIM)),
                     d_norm_gain=_row(d_norm_gain[j]))
            y = _mix_odd(u.reshape(bsz, seq, ODD_COLS), p, l, pair_masks)
        x2 = _out_moe(y.reshape(t, d), w_out[l].astype(BF16), x2, g_m, gain_f, sc_f, sh_f, g_f, wr_t, bias_col,
                      w_gate, w_up, w_down, l, final_gain.reshape(1, d), l == depth - 1, seq, tm)
    return x2.reshape(bsz, seq, d)
```

```python
import functools

import numpy as np
import jax
import jax.numpy as jnp
from jax import lax
from jax.experimental import pallas as pl
from jax.experimental.pallas import tpu as pltpu

F32 = jnp.float32
BF16 = jnp.bfloat16
HIGHEST = lax.Precision.HIGHEST

D_MODEL = 1024
MIX_HALF = 512
HEAD_DIM = 64
CHUNK = 64
NORM_EPS = 1e-6
RWKV_HEADS = 8
RWKV_IN = 1792
RWKV_GN_EPS = 64e-5
RWKV_DECAY_SCALE = float(np.exp(-0.5))
LOG2_E = float(np.log2(np.e))
GLA_HEADS = 4
GLA_GATE_RANK = 16
GLA_GATE_NORM = 16.0
SSD_HEADS = 8
SSD_GROUPS = 2
SSD_STATE = 64
SSD_CONV = 4
SSD_XBC = 768
N_EXPERTS = 16
N_EXPERT_GROUPS = 4
EXPERTS_PER_GROUP = 4
D_EXPERT = 512
LANE = 128
EVEN_COLS = 3456
ODD_COLS = 3328
MOE_ROWS = 512
EVEN_TILE, EVEN_ROWS = 64, 4
ODD_TILE, ODD_ROWS = 128, 1
VMEM_LIMIT = 48 * 1024 * 1024


def _dot(a, b):
    return jnp.dot(a.astype(BF16), b.astype(BF16), preferred_element_type=F32)


def _dot_nt(a, b):
    return lax.dot_general(a.astype(BF16), b.astype(BF16), (((1,), (1,)), ((), ())),
                           preferred_element_type=F32)


def _dot_tn(a, b):
    return lax.dot_general(a.astype(BF16), b.astype(BF16), (((0,), (0,)), ((), ())),
                           preferred_element_type=F32)


def _dot_exact(a, b):
    return jnp.dot(a, b, precision=HIGHEST, preferred_element_type=F32)


def _sigmoid(x):
    return 1.0 / (1.0 + jnp.exp(-x))


def _silu(x):
    return x * _sigmoid(x)


def _softplus(x):
    return jnp.maximum(x, 0.0) + jnp.log(1.0 + jnp.exp(-jnp.abs(x)))


def _log_sigmoid(x):
    return jnp.minimum(x, 0.0) - jnp.log(1.0 + jnp.exp(-jnp.abs(x)))


def _chunk_constants():
    t = np.arange(CHUNK)
    masks = []
    for shift in range(5, -1, -1):
        masks.append(((t[:, None] > t[None, :]) & (((t[:, None] ^ t[None, :]) >> shift) == 1)))
    masks.append(t[:, None] == t[None, :])
    return jnp.asarray(np.stack(masks).astype(np.float32))


def _head_indicator(width, seg):
    i = np.arange(width)
    return jnp.asarray((i[:, None] // seg == i[None, :] // seg).astype(np.float32), dtype=BF16)


def _ada_kernel(c_ref, w_ref, b_ref, o_ref):
    cond = _silu(c_ref[...])
    o_ref[0] = _dot_exact(cond, w_ref[0]) + b_ref[0]


def _ada(c, w_ada, b_ada):
    depth, d, n = w_ada.shape
    bsz = c.shape[0]
    tn = 1536
    return pl.pallas_call(
        _ada_kernel,
        grid=(depth, n // tn),
        in_specs=[pl.BlockSpec((bsz, d), lambda l, j: (0, 0)),
                  pl.BlockSpec((1, d, tn), lambda l, j: (l, 0, j)),
                  pl.BlockSpec((1, 1, tn), lambda l, j: (l, 0, j))],
        out_specs=pl.BlockSpec((1, bsz, tn), lambda l, j: (l, 0, j)),
        out_shape=jax.ShapeDtypeStruct((depth, bsz, n), F32),
        compiler_params=pltpu.CompilerParams(dimension_semantics=("parallel", "parallel"),
                                             vmem_limit_bytes=VMEM_LIMIT),
        name="ada",
    )(c, w_ada, b_ada.reshape(depth, 1, n))


def _modulated_norm(x, gain, scale, shift):
    ms = jnp.mean(x * x, axis=-1, keepdims=True)
    return (x * lax.rsqrt(ms + NORM_EPS)) * gain * (1.0 + scale) + shift


def _in_proj_kernel(x_ref, gain_ref, sc_ref, sh_ref, w_ref, o_ref, *, col_chunk):
    h = _modulated_norm(x_ref[...], gain_ref[...], sc_ref[0], sh_ref[0]).astype(BF16)
    n = o_ref.shape[1]
    for j in range(0, n, col_chunk):
        o_ref[:, j:j + col_chunk] = jnp.dot(h, w_ref[:, j:j + col_chunk],
                                            preferred_element_type=F32).astype(o_ref.dtype)


def _in_proj(x2, gain, scale, shift, w, seq, tm):
    t, d = x2.shape
    n = w.shape[1]
    col_chunk = next(n // parts for parts in (3, 2, 1) if n % (parts * LANE) == 0)
    per_b = seq // tm
    return pl.pallas_call(
        functools.partial(_in_proj_kernel, col_chunk=col_chunk),
        grid=(t // tm,),
        in_specs=[pl.BlockSpec((tm, d), lambda i: (i, 0)),
                  pl.BlockSpec((1, d), lambda i: (0, 0)),
                  pl.BlockSpec((1, 1, d), lambda i: (i // per_b, 0, 0)),
                  pl.BlockSpec((1, 1, d), lambda i: (i // per_b, 0, 0)),
                  pl.BlockSpec((d, n), lambda i: (0, 0))],
        out_specs=pl.BlockSpec((tm, n), lambda i: (i, 0)),
        out_shape=jax.ShapeDtypeStruct((t, n), BF16),
        compiler_params=pltpu.CompilerParams(dimension_semantics=("parallel",),
                                             vmem_limit_bytes=VMEM_LIMIT),
        name="in_proj",
    )(x2, gain, scale, shift, w)


def _cumsum_rows(x, on_mxu):
    if not on_mxu:
        rows = lax.broadcasted_iota(jnp.int32, x.shape, 0)
        step = 1
        while step < CHUNK:
            x = x + jnp.where(rows >= step, pltpu.roll(x, step, axis=0), 0.0)
            step *= 2
        return x
    n = x.shape[1]
    hi = x.astype(BF16)
    lo = (x - hi.astype(F32)).astype(BF16)
    row = lax.broadcasted_iota(jnp.int32, (CHUNK, CHUNK), 0)
    col = lax.broadcasted_iota(jnp.int32, (CHUNK, CHUNK), 1)
    tril = jnp.where(col <= row, 1.0, 0.0).astype(BF16)
    both = jnp.dot(tril, jnp.concatenate([hi, lo], axis=1), preferred_element_type=F32)
    return both[:, 0:n] + both[:, n:]


def _level_refs(b):
    cols = b.shape[1]
    rows = lax.broadcasted_iota(jnp.int32, b.shape, 0)

    def spread(offset, span):
        pieces = [jnp.broadcast_to(b[s + offset:s + offset + 1], (span, cols)) for s in range(0, CHUNK, span)]
        return pieces[0] if len(pieces) == 1 else jnp.concatenate(pieces, axis=0)

    refs = [spread(n // 2 - 1, n) for n in (64, 32, 16, 8)]
    refs.append(jnp.where((rows & 7) < 4, spread(1, 8), spread(5, 8)))
    refs.append(jnp.where((rows & 1) == 1, pltpu.roll(b, 1, axis=0), b))
    return refs


def _gla_chunk(q, k, v, g, st, pair_masks, heads, dk, dv, cumsum_on_mxu):
    hs = range(heads)
    ks = [slice(h * dk, (h + 1) * dk) for h in hs]
    vs = [slice(h * dv, (h + 1) * dv) for h in hs]
    b = _cumsum_rows(g, cumsum_on_mxu)
    b_last = b[CHUNK - 1:CHUNK]
    q_in = q * jnp.exp2(b)
    k_st = k * jnp.exp2(b_last - b)
    decay = jnp.exp2(b_last)
    yield

    refs = _level_refs(b)
    scores = [None] * heads
    for lvl in range(7):
        if lvl < 6:
            e = jnp.exp2(-jnp.abs(b - refs[lvl]))
            qe, ke = q * e, k * e
        else:
            qe, ke = q, k
        keep = pair_masks[lvl] > 0.5
        for h in hs:
            p = _dot_nt(qe[:, ks[h]], ke[:, ks[h]])
            scores[h] = jnp.where(keep, p, 0.0) if scores[h] is None else jnp.where(keep, p, scores[h])
        yield
    o_inter = [_dot_nt(q_in[:, ks[h]], st[h]) for h in hs]
    o_intra = [_dot(scores[h], v[:, vs[h]]) for h in hs]
    yield
    st_new = [st[h] * decay[:, ks[h]] + _dot_tn(v[:, vs[h]], k_st[:, ks[h]]) for h in hs]
    return jnp.concatenate([o_inter[h] + o_intra[h] for h in hs], axis=-1), st_new


def _interleave(*stages):
    results = [None] * len(stages)
    live = list(range(len(stages)))
    while live:
        for i in list(live):
            try:
                next(stages[i])
            except StopIteration as stop:
                results[i] = stop.value
                live.remove(i)
    return results


def _head_rms(o, gain, heads, dv):
    outs = []
    for h in range(heads):
        oh = o[:, h * dv:(h + 1) * dv]
        ms = jnp.mean(oh * oh, axis=-1, keepdims=True)
        outs.append(oh * lax.rsqrt(ms + NORM_EPS) * gain)
    return jnp.concatenate(outs, axis=-1)


def _rwkv_chunk(r, kk, a, kt, v, logw, s0):
    b = _cumsum_rows(logw, True)
    b_last = b[CHUNK - 1:CHUNK]
    e_neg = jnp.exp2(-b)
    e_end = jnp.exp2(b_last - b)
    decay = jnp.exp2(b_last)
    beta = a * kk
    k_bar = kk * jnp.exp2(b - logw)
    r_bar = r * jnp.exp2(b)
    beta_t, k_t = beta * e_neg, kt * e_neg
    beta_hat, k_hat = beta * e_end, kt * e_end
    row = lax.broadcasted_iota(jnp.int32, (CHUNK, CHUNK), 0)
    col = lax.broadcasted_iota(jnp.int32, (CHUNK, CHUNK), 1)
    strict = col < row
    incl = col <= row
    same_blk = (row >> 4) == (col >> 4)
    in_blk = jnp.logical_and(strict, same_blk)
    off_blk = jnp.logical_and(strict, jnp.logical_not(same_blk))
    eye = (row == col).astype(F32)
    hs = range(RWKV_HEADS)
    sls = [slice(h * HEAD_DIM, (h + 1) * HEAD_DIM) for h in hs]
    kr = [jnp.concatenate([k_bar[:, sl], r_bar[:, sl]], axis=0).astype(BF16) for sl in sls]
    bk = [jnp.concatenate([beta_t[:, sl], k_t[:, sl]], axis=0) for sl in sls]
    v_b = v.astype(BF16)
    vh = [v_b[:, sl] for sl in sls]
    yield
    m1 = [_dot_nt(kr[h], bk[h]) for h in hs]
    yield
    m2 = [_dot_nt(kr[h], s0[h]) for h in hs]
    yield
    x1 = [jnp.where(in_blk, m[0:CHUNK, 0:CHUNK], 0.0) for m in m1]
    a_off = [jnp.where(off_blk, m[0:CHUNK, 0:CHUNK], 0.0) for m in m1]
    b_m = [jnp.where(strict, m[0:CHUNK, CHUNK:], 0.0) for m in m1]
    cb_m = [jnp.where(incl, m[CHUNK:, 0:CHUNK], 0.0) for m in m1]
    ck_m = [jnp.where(incl, m[CHUNK:, CHUNK:], 0.0) for m in m1]
    rhs = [m2[h][0:CHUNK] + _dot(b_m[h], vh[h]) for h in hs]
    lo, hi = slice(0, CHUNK), slice(CHUNK, 2 * CHUNK)
    side = lambda left, right: jnp.concatenate([left, right], axis=1)
    x2 = [_dot(x, x) for x in x1]
    yield
    p = [eye - x for x in x1]
    w = [_dot(x2[h], side(p[h], x2[h])) for h in hs]
    yield
    p = [p[h] + w[h][:, lo] for h in hs]
    x4 = [w[h][:, hi] for h in hs]
    w = [_dot(x4[h], side(p[h], x4[h])) for h in hs]
    yield
    p = [p[h] + w[h][:, lo] for h in hs]
    t_d = [p[h] + _dot(w[h][:, hi], p[h]) for h in hs]
    yield
    nz = [_dot(t_d[h], side(a_off[h], rhs[h])) for h in hs]
    yield
    w = [_dot(nz[h][:, lo], nz[h]) for h in hs]
    yield
    y1 = [w[h][:, hi] - nz[h][:, hi] for h in hs]
    neg_u = [y1[h] + _dot(w[h][:, lo], y1[h]) for h in hs]
    vu = [jnp.concatenate([vh[h], neg_u[h].astype(BF16)], axis=0) for h in hs]
    yield
    outs = [m2[h][CHUNK:] + _dot(side(ck_m[h], cb_m[h]), vu[h]) for h in hs]
    yield
    s_new = [s0[h] * decay[:, sls[h]]
             + _dot_tn(vu[h], jnp.concatenate([k_hat[:, sls[h]], beta_hat[:, sls[h]]], axis=0)) for h in hs]
    return jnp.concatenate(outs, axis=-1), s_new


def _head_sums(x, ind):
    return jnp.concatenate([_dot(x[:, i:i + LANE], ind) for i in range(0, x.shape[1], LANE)], axis=1)


def _mix_even_kernel(u_ref, mu_ref, w0_ref, wup_ref, a0_ref, aup_ref, gup_ref, kk_ref, ka_ref,
                     rk_ref, lng_ref, lnb_ref, alup_ref, albias_ref, bng_ref, ind_ref,
                     pmask_ref, y_ref, s_ref, g_ref, prev_ref):
    @pl.when(pl.program_id(1) == 0)
    def _():
        s_ref[...] = jnp.zeros_like(s_ref)
        g_ref[...] = jnp.zeros_like(g_ref)
        prev_ref[...] = jnp.zeros_like(prev_ref)

    pair_masks = pmask_ref[...]
    ind = ind_ref[...]
    n_rows = u_ref.shape[0]
    s_state = [[s_ref[b * RWKV_HEADS + h] for h in range(RWKV_HEADS)] for b in range(n_rows)]
    g_state = [[g_ref[b * GLA_HEADS + h] for h in range(GLA_HEADS)] for b in range(n_rows)]
    prev = [prev_ref[b] for b in range(n_rows)]

    def rwkv_stages(b, rs, prev, s_state):
        ua = u_ref[b, rs, 0:RWKV_IN].astype(F32)
        rows = lax.broadcasted_iota(jnp.int32, ua.shape, 0)
        shifted = jnp.where(rows == 0, prev, pltpu.roll(ua, 1, axis=0))
        xa = ua + mu_ref[...] * (shifted - ua)
        r, k, v = xa[:, 0:512], xa[:, 512:1024], xa[:, 1024:1536]
        wd, ad, gd = xa[:, 1536:1600], xa[:, 1600:1664], xa[:, 1664:1792]
        logw = (-RWKV_DECAY_SCALE * LOG2_E) * _sigmoid(w0_ref[...] + _dot(jnp.tanh(wd), wup_ref[...]))
        a = _sigmoid(a0_ref[...] + _dot(ad, aup_ref[...]))
        gate = _dot(_sigmoid(gd), gup_ref[...])
        yield
        kk = k * kk_ref[...]
        kk = kk * lax.rsqrt(_head_sums(kk * kk, ind) + 1e-12)
        kt = k * (1.0 + (a - 1.0) * ka_ref[...])
        yield
        y, s_state = yield from _rwkv_chunk(r, kk, a, kt, v, logw, s_state)
        yield
        mean = _head_sums(y, ind) * (1.0 / HEAD_DIM)
        yc = y - mean
        yield
        var = _head_sums(yc * yc, ind) * (1.0 / HEAD_DIM)
        y = yc * lax.rsqrt(var + RWKV_GN_EPS) * lng_ref[...] + lnb_ref[...]
        y = y + _head_sums(r * kt * rk_ref[...], ind) * v
        y_ref[b, rs, 0:MIX_HALF] = (y * gate).astype(y_ref.dtype)
        return ua[CHUNK - 1:CHUNK], s_state

    def gla_stages(b, rs, g_state):
        ub = u_ref[b, rs, RWKV_IN:].astype(F32)
        q, kg, vg = ub[:, 0:256] * (HEAD_DIM ** -0.5), ub[:, 256:512], ub[:, 512:1024]
        alpha, gg = ub[:, 1024:1152], ub[:, 1152:1664]
        log_a = _log_sigmoid(_dot(alpha, alup_ref[...]) + albias_ref[...]) * (LOG2_E / GLA_GATE_NORM)
        yield
        o, g_state = yield from _gla_chunk(q, kg, vg, log_a, g_state, pair_masks, GLA_HEADS, 64, 128, True)
        o = _head_rms(o, bng_ref[...], GLA_HEADS, 128)
        y_ref[b, rs, MIX_HALF:] = (o * _silu(gg)).astype(y_ref.dtype)
        return g_state

    for c in range(u_ref.shape[1] // CHUNK):
        rs = slice(c * CHUNK, (c + 1) * CHUNK)
        done = _interleave(*[stages for b in range(n_rows)
                             for stages in (rwkv_stages(b, rs, prev[b], s_state[b]), gla_stages(b, rs, g_state[b]))])
        for b in range(n_rows):
            (prev[b], s_state[b]), g_state[b] = done[2 * b], done[2 * b + 1]
    for b in range(n_rows):
        for h in range(RWKV_HEADS):
            s_ref[b * RWKV_HEADS + h] = s_state[b][h]
        for h in range(GLA_HEADS):
            g_ref[b * GLA_HEADS + h] = g_state[b][h]
        prev_ref[b] = prev[b]


def _row(p):
    return p.reshape(1, -1).astype(F32)


def _mix_even(u3, p, consts):
    bsz, seq, n = u3.shape
    pair_masks, ind = consts
    small = [p["mu"], p["w0"], p["w_up"], p["a0"], p["a_up"], p["g_up"], p["k_k"], p["k_a"], p["r_k"],
             p["ln_gain"], p["ln_bias"], p["alpha_up"], p["alpha_bias"], p["b_norm_gain"],
             ind, pair_masks]

    def full(arr):
        nd = arr.ndim
        return pl.BlockSpec(arr.shape, lambda b, s, _nd=nd: (0,) * _nd)

    rows = EVEN_ROWS if bsz % EVEN_ROWS == 0 else 1
    return pl.pallas_call(
        _mix_even_kernel,
        grid=(bsz // rows, seq // EVEN_TILE),
        in_specs=[pl.BlockSpec((rows, EVEN_TILE, n), lambda b, s: (b, s, 0))] + [full(a) for a in small],
        out_specs=pl.BlockSpec((rows, EVEN_TILE, D_MODEL), lambda b, s: (b, s, 0)),
        out_shape=jax.ShapeDtypeStruct((bsz, seq, D_MODEL), BF16),
        scratch_shapes=[pltpu.VMEM((rows * RWKV_HEADS, HEAD_DIM, HEAD_DIM), F32),
                        pltpu.VMEM((rows * GLA_HEADS, 128, 64), F32),
                        pltpu.VMEM((rows, 1, RWKV_IN), F32)],
        compiler_params=pltpu.CompilerParams(dimension_semantics=("parallel", "arbitrary"),
                                             vmem_limit_bytes=VMEM_LIMIT),
        name="mix_even",
    )(u3, *small)


def _mix_odd_kernel(u_ref, clb_ref, cng_ref, convw_ref, convb_ref, dtb_ref, alog_ref, skip_ref,
                    dng_ref, pmask_ref, y_ref, h_ref, d_ref, tail_ref, *, layer):
    @pl.when(pl.program_id(1) == 0)
    def _():
        h_ref[...] = jnp.zeros_like(h_ref)
        d_ref[...] = jnp.zeros_like(d_ref)
        tail_ref[...] = jnp.zeros_like(tail_ref)

    pair_masks = pmask_ref[...]
    c_lb = clb_ref[...]
    c_exp = jnp.exp(c_lb - jnp.max(c_lb, axis=0, keepdims=True))
    lb = jnp.sum(c_exp[1:layer + 1], axis=0, keepdims=True) / jnp.sum(c_exp, axis=0, keepdims=True)
    n_rows = u_ref.shape[0]
    h_state = [[h_ref[b * 4 + h] for h in range(4)] for b in range(n_rows)]
    d_state = [[d_ref[b * SSD_GROUPS + grp] for grp in range(SSD_GROUPS)] for b in range(n_rows)]
    tail = [tail_ref[b] for b in range(n_rows)]

    def hgrn_stages(b, rs, h_state):
        u = u_ref[b, rs, 0:1536].astype(F32)
        q, fr, iv, g = u[:, 0:256], u[:, 256:512], u[:, 512:1024], u[:, 1024:1536]
        f = lb + (1.0 - lb) * _sigmoid(fr)
        yield
        o, h_state = yield from _gla_chunk(q, 1.0 - f, iv, jnp.log2(f), h_state, pair_masks, 4, 64, 128, False)
        y_ref[b, rs, 0:MIX_HALF] = (_head_rms(o, cng_ref[...], 4, 128) * _silu(g)).astype(y_ref.dtype)
        return h_state

    def ssd_stages(b, rs, tail, d_state):
        u = u_ref[b, rs, 1536:].astype(F32)
        y, d_state, tail = yield from _ssd_chunk(u, tail, d_state, convw_ref, convb_ref[...], dtb_ref[...],
                                                 alog_ref[...], skip_ref[...], dng_ref[...])
        y_ref[b, rs, MIX_HALF:] = y.astype(y_ref.dtype)
        return tail, d_state

    for c in range(u_ref.shape[1] // CHUNK):
        rs = slice(c * CHUNK, (c + 1) * CHUNK)
        h_state = _interleave(*[hgrn_stages(b, rs, h_state[b]) for b in range(n_rows)])
        done = _interleave(*[ssd_stages(b, rs, tail[b], d_state[b]) for b in range(n_rows)])
        for b in range(n_rows):
            tail[b], d_state[b] = done[b]
    for b in range(n_rows):
        for h in range(4):
            h_ref[b * 4 + h] = h_state[b][h]
        for grp in range(SSD_GROUPS):
            d_ref[b * SSD_GROUPS + grp] = d_state[b][grp]
        tail_ref[b] = tail[b]


def _ssd_chunk(u, tail, st, convw_ref, conv_b, dt_bias, a_log, skip, norm_gain):
    z, xbc, dt_raw = u[:, 0:512], u[:, 512:1280], u[:, 1280:1792]
    new_tail = xbc[CHUNK - 8:CHUNK]
    rows8 = lax.broadcasted_iota(jnp.int32, (8, SSD_XBC), 0)
    conv = xbc * convw_ref[SSD_CONV - 1:SSD_CONV] + conv_b
    for back in range(1, SSD_CONV):
        rolled = pltpu.roll(xbc, back, axis=0)
        head8 = jnp.where(rows8 < back, pltpu.roll(tail, back, axis=0), rolled[0:8])
        shifted = jnp.concatenate([head8, rolled[8:]], axis=0)
        conv = conv + shifted * convw_ref[SSD_CONV - 1 - back:SSD_CONV - back]
    xbc = _silu(conv)
    yield
    xs, bmat, cmat = xbc[:, 0:512], xbc[:, 512:640], xbc[:, 640:768]
    dt = _softplus(dt_raw + dt_bias)
    da = dt * (-LOG2_E * jnp.exp(a_log))
    cum = _cumsum_rows(da, False)
    yield
    cum_last = cum[CHUNK - 1:CHUNK]
    e_cum = jnp.exp2(cum)
    xdt = xs * dt
    x_end = xdt * jnp.exp2(cum_last - cum)
    decay = jnp.exp2(cum_last)
    yield
    row = lax.broadcasted_iota(jnp.int32, (CHUNK, CHUNK), 0)
    col = lax.broadcasted_iota(jnp.int32, (CHUNK, CHUNK), 1)
    causal = col <= row
    groups = range(SSD_GROUPS)
    heads = range(SSD_HEADS)
    per_group = SSD_HEADS // SSD_GROUPS
    gs = [slice(grp * 256, (grp + 1) * 256) for grp in groups]
    ns = [slice(grp * SSD_STATE, (grp + 1) * SSD_STATE) for grp in groups]
    hs = [slice(h * HEAD_DIM, (h + 1) * HEAD_DIM) for h in heads]
    cb = [_dot_nt(cmat[:, ns[grp]], bmat[:, ns[grp]]) for grp in groups]
    y_off = [_dot(cmat[:, ns[grp]], st[grp]) for grp in groups]
    st_new = [st[grp] * decay[:, gs[grp]] + _dot_tn(bmat[:, ns[grp]], x_end[:, gs[grp]]) for grp in groups]
    yield
    cum_h = [cum[:, sl] for sl in hs]
    seg = [jnp.where(causal, jnp.exp2(ch - ch.T), 0.0) for ch in cum_h]
    yield
    y_diag = [_dot(cb[h // per_group] * seg[h], xdt[:, hs[h]]) for h in heads]
    yield
    y = (jnp.concatenate(y_diag, axis=-1) + jnp.concatenate(y_off, axis=-1) * e_cum + skip * xs)
    y = y * _silu(z)
    parts = []
    for grp in range(SSD_GROUPS):
        yg = y[:, grp * 256:(grp + 1) * 256]
        ms = jnp.mean(yg * yg, axis=-1, keepdims=True)
        parts.append(yg * lax.rsqrt(ms + NORM_EPS))
    return jnp.concatenate(parts, axis=-1) * norm_gain, st_new, new_tail


def _mix_odd(u3, p, layer, consts):
    bsz, seq, n = u3.shape
    pair_masks = consts
    small = [p["c_lb"], p["c_norm_gain"], p["conv_w"], p["conv_b"], p["dt_bias"], p["a_log"], p["skip"],
             p["d_norm_gain"], pair_masks]

    def full(arr):
        nd = arr.ndim
        return pl.BlockSpec(arr.shape, lambda b, s, _nd=nd: (0,) * _nd)

    rows = ODD_ROWS if bsz % ODD_ROWS == 0 else 1
    return pl.pallas_call(
        functools.partial(_mix_odd_kernel, layer=layer),
        grid=(bsz // rows, seq // ODD_TILE),
        in_specs=[pl.BlockSpec((rows, ODD_TILE, n), lambda b, s: (b, s, 0))] + [full(a) for a in small],
        out_specs=pl.BlockSpec((rows, ODD_TILE, D_MODEL), lambda b, s: (b, s, 0)),
        out_shape=jax.ShapeDtypeStruct((bsz, seq, D_MODEL), BF16),
        scratch_shapes=[pltpu.VMEM((rows * 4, 128, 64), F32),
                        pltpu.VMEM((rows * SSD_GROUPS, SSD_STATE, 256), F32),
                        pltpu.VMEM((rows, 8, SSD_XBC), F32)],
        compiler_params=pltpu.CompilerParams(dimension_semantics=("parallel", "arbitrary"),
                                             vmem_limit_bytes=VMEM_LIMIT),
        name="mix_odd",
    )(u3, *small)


def _route_kernel(y_ref, wo_ref, x_ref, gm_ref, gain_ref, sc_ref, sh_ref, wr_ref, bias_ref, tri_ref,
                  xo_ref, h_ref, e_ref, gt_ref, r_ref, cnt_ref):
    x = x_ref[...] + gm_ref[0] * jnp.dot(y_ref[...], wo_ref[...], preferred_element_type=F32)
    xo_ref[...] = x
    h = _modulated_norm(x, gain_ref[...], sc_ref[0], sh_ref[0])
    wr = wr_ref[...]
    wr_hi = wr.astype(BF16)
    wr_lo = (wr - wr_hi.astype(F32)).astype(BF16)
    h_hi = h.astype(BF16)
    h_ref[...] = h_hi
    h_lo = (h - h_hi.astype(F32)).astype(BF16)
    both = _dot_nt(jnp.concatenate([wr_hi, wr_lo], axis=0), h_hi)
    logits = both[0:N_EXPERTS] + both[N_EXPERTS:] + _dot_nt(wr_hi, h_lo)
    score = _sigmoid(logits)
    sel = score + bias_ref[...]
    gscore = []
    for grp in range(N_EXPERT_GROUPS):
        a, b, c, d = [sel[grp * 4 + j:grp * 4 + j + 1] for j in range(4)]
        hi1, lo1, hi2, lo2 = jnp.maximum(a, b), jnp.minimum(a, b), jnp.maximum(c, d), jnp.minimum(c, d)
        gscore.append(jnp.maximum(hi1, hi2) + jnp.maximum(jnp.minimum(hi1, hi2), jnp.maximum(lo1, lo2)))
    best, gidx = gscore[0], jnp.zeros_like(gscore[0], dtype=jnp.int32)
    for grp in range(1, N_EXPERT_GROUPS):
        better = gscore[grp] > best
        gidx = jnp.where(better, grp, gidx)
        best = jnp.where(better, gscore[grp], best)
    vals, raw = [], []
    for j in range(EXPERTS_PER_GROUP):
        vj, rj = sel[j:j + 1], score[j:j + 1]
        for grp in range(1, N_EXPERT_GROUPS):
            vj = jnp.where(gidx == grp, sel[grp * 4 + j:grp * 4 + j + 1], vj)
            rj = jnp.where(gidx == grp, score[grp * 4 + j:grp * 4 + j + 1], rj)
        vals.append(vj)
        raw.append(rj)
    i1, m1, g1 = jnp.zeros_like(gidx), vals[0], raw[0]
    for j in range(1, EXPERTS_PER_GROUP):
        better = vals[j] > m1
        i1 = jnp.where(better, j, i1)
        m1 = jnp.where(better, vals[j], m1)
        g1 = jnp.where(better, raw[j], g1)
    i2, m2, g2 = jnp.zeros_like(gidx), jnp.full_like(m1, -jnp.inf), jnp.zeros_like(m1)
    for j in range(EXPERTS_PER_GROUP):
        better = jnp.logical_and(i1 != j, vals[j] > m2)
        i2 = jnp.where(better, j, i2)
        m2 = jnp.where(better, vals[j], m2)
        g2 = jnp.where(better, raw[j], g2)
    total = g1 + g2
    e1 = gidx * EXPERTS_PER_GROUP + i1
    e2 = gidx * EXPERTS_PER_GROUP + i2
    e_ref[0:1, :] = e1
    e_ref[1:2, :] = e2
    gt_ref[0:1, :] = g1 / total
    gt_ref[1:2, :] = g2 / total
    @pl.when(pl.program_id(0) == 0)
    def _():
        cnt_ref[...] = jnp.zeros_like(cnt_ref)

    eid = lax.broadcasted_iota(jnp.int32, logits.shape, 0)
    is1, is2 = eid == e1, eid == e2
    member = jnp.where(jnp.logical_or(is1, is2), 1.0, 0.0)
    before = jnp.dot(member.astype(BF16), tri_ref[...], preferred_element_type=F32) - member
    base = cnt_ref[:, 0:1] + before
    r_ref[0:1, :] = jnp.sum(jnp.where(is1, base, 0.0), axis=0, keepdims=True).astype(jnp.int32)
    r_ref[1:2, :] = jnp.sum(jnp.where(is2, base, 0.0), axis=0, keepdims=True).astype(jnp.int32)
    cnt_ref[...] = cnt_ref[...] + jnp.sum(member, axis=1, keepdims=True)


def _out_route(y2, w_out, x2, gate_m, gain, scale, shift, wr_t, bias_col, seq, tm):
    t, d = x2.shape
    per_b = seq // tm
    tri = jnp.asarray(np.triu(np.ones((tm, tm), np.float32)), dtype=BF16)
    per_batch = pl.BlockSpec((1, 1, d), lambda i: (i // per_b, 0, 0))
    rows = pl.BlockSpec((tm, d), lambda i: (i, 0))
    pair = pl.BlockSpec((2, tm), lambda i: (0, i))
    return pl.pallas_call(
        _route_kernel,
        grid=(t // tm,),
        in_specs=[rows,
                  pl.BlockSpec((d, d), lambda i: (0, 0)),
                  rows,
                  per_batch,
                  pl.BlockSpec((1, d), lambda i: (0, 0)),
                  per_batch,
                  per_batch,
                  pl.BlockSpec((N_EXPERTS, d), lambda i: (0, 0)),
                  pl.BlockSpec((N_EXPERTS, 1), lambda i: (0, 0)),
                  pl.BlockSpec((tm, tm), lambda i: (0, 0))],
        out_specs=[rows, rows, pair, pair, pair, pl.BlockSpec((N_EXPERTS, LANE), lambda i: (0, 0))],
        out_shape=[jax.ShapeDtypeStruct((t, d), F32),
                   jax.ShapeDtypeStruct((t, d), BF16),
                   jax.ShapeDtypeStruct((2, t), jnp.int32),
                   jax.ShapeDtypeStruct((2, t), F32),
                   jax.ShapeDtypeStruct((2, t), jnp.int32),
                   jax.ShapeDtypeStruct((N_EXPERTS, LANE), F32)],
        compiler_params=pltpu.CompilerParams(dimension_semantics=("arbitrary",),
                                             vmem_limit_bytes=VMEM_LIMIT),
        name="out_route",
    )(y2, w_out, x2, gate_m, gain, scale, shift, wr_t, bias_col, tri)


def _expert_kernel(ib_ref, ie_ref, lo_ref, hi_ref, x_ref, wg_ref, wu_ref, wd_ref, o_ref, wg_s, wu_s, wd_s):
    i = pl.program_id(0)
    blk, lo, hi = ib_ref[i], lo_ref[i], hi_ref[i]

    @pl.when(jnp.logical_or(i == 0, ie_ref[i] != ie_ref[jnp.maximum(i - 1, 0)]))
    def _():
        wg_s[...] = wg_ref[0, 0].astype(BF16)
        wu_s[...] = wu_ref[0, 0].astype(BF16)
        wd_s[...] = wd_ref[0, 0].astype(BF16)

    @pl.when(hi > lo)
    def _():
        x = x_ref[...]
        gate = jnp.dot(x, wg_s[...], preferred_element_type=F32)
        up = jnp.dot(x, wu_s[...], preferred_element_type=F32)
        act = (_silu(gate) * up).astype(BF16)
        y = jnp.dot(act, wd_s[...], preferred_element_type=F32).astype(o_ref.dtype)
        first = lo == blk * MOE_ROWS
        whole = jnp.logical_and(first, hi == (blk + 1) * MOE_ROWS)

        @pl.when(whole)
        def _():
            o_ref[...] = y

        @pl.when(jnp.logical_not(whole))
        def _():
            rows = blk * MOE_ROWS + lax.broadcasted_iota(jnp.int32, y.shape, 0)
            mine = jnp.logical_and(rows >= lo, rows < hi)

            @pl.when(first)
            def _():
                o_ref[...] = jnp.where(mine, y, jnp.zeros_like(y))

            @pl.when(jnp.logical_not(first))
            def _():
                o_ref[...] = jnp.where(mine, y, o_ref[...])


def _experts(xb, w_gate, w_up, w_down, layer, items):
    n_rows, d = xb.shape
    de = w_gate.shape[-1]
    n_items = items[0].shape[0]
    grid_spec = pltpu.PrefetchScalarGridSpec(
        num_scalar_prefetch=4,
        grid=(n_items,),
        in_specs=[pl.BlockSpec((MOE_ROWS, d), lambda i, ib, ie, lo, hi: (ib[i], 0)),
                  pl.BlockSpec((1, 1, d, de), lambda i, ib, ie, lo, hi: (layer, ie[i], 0, 0)),
                  pl.BlockSpec((1, 1, d, de), lambda i, ib, ie, lo, hi: (layer, ie[i], 0, 0)),
                  pl.BlockSpec((1, 1, de, d), lambda i, ib, ie, lo, hi: (layer, ie[i], 0, 0))],
        out_specs=pl.BlockSpec((MOE_ROWS, d), lambda i, ib, ie, lo, hi: (ib[i], 0)),
        scratch_shapes=[pltpu.VMEM((d, de), BF16), pltpu.VMEM((d, de), BF16), pltpu.VMEM((de, d), BF16)],
    )
    return pl.pallas_call(
        _expert_kernel,
        grid_spec=grid_spec,
        out_shape=jax.ShapeDtypeStruct((n_rows, d), BF16),
        compiler_params=pltpu.CompilerParams(dimension_semantics=("arbitrary",),
                                             vmem_limit_bytes=VMEM_LIMIT),
        name="experts",
    )(*items, xb, w_gate, w_up, w_down)


def _combine_kernel(x_ref, y0_ref, y1_ref, w_ref, g_ref, fg_ref, o_ref, *, final):
    w = w_ref[...]
    moe = y0_ref[...].astype(F32) * w[:, 0:1] + y1_ref[...].astype(F32) * w[:, 1:2]
    x = x_ref[...] + g_ref[0] * moe
    if final:
        x = x * lax.rsqrt(jnp.mean(x * x, axis=-1, keepdims=True) + NORM_EPS) * fg_ref[...]
    o_ref[...] = x


def _combine(x2, y0, y1, weights, gate, final_gain, final, seq, tm):
    t, d = x2.shape
    per_b = seq // tm
    return pl.pallas_call(
        functools.partial(_combine_kernel, final=final),
        grid=(t // tm,),
        in_specs=[pl.BlockSpec((tm, d), lambda i: (i, 0)),
                  pl.BlockSpec((tm, d), lambda i: (i, 0)),
                  pl.BlockSpec((tm, d), lambda i: (i, 0)),
                  pl.BlockSpec((tm, 2), lambda i: (i, 0)),
                  pl.BlockSpec((1, 1, d), lambda i: (i // per_b, 0, 0)),
                  pl.BlockSpec((1, d), lambda i: (0, 0))],
        out_specs=pl.BlockSpec((tm, d), lambda i: (i, 0)),
        out_shape=jax.ShapeDtypeStruct((t, d), F32),
        compiler_params=pltpu.CompilerParams(dimension_semantics=("parallel",),
                                             vmem_limit_bytes=VMEM_LIMIT),
        name="combine",
    )(x2, y0, y1, weights, gate, final_gain)


def _out_moe(y2, w_out, x2, gate_m, gain, scale, shift, gate, wr_t, bias_col, w_gate, w_up, w_down, layer,
             final_gain, final, seq, tm):
    t, d = x2.shape
    x2, h, experts, weights, ranks, cnt = _out_route(y2, w_out, x2, gate_m, gain, scale, shift, wr_t, bias_col,
                                                     seq, tm)
    counts = cnt[:, 0].astype(jnp.int32)
    end = jnp.cumsum(counts)
    start = end - counts
    dest = ranks
    for e in range(N_EXPERTS):
        dest = dest + jnp.where(experts == e, start[e], 0)
    tok = jnp.arange(t, dtype=jnp.uint32)
    keys = dest.astype(jnp.uint32) * jnp.uint32(t) + tok[None, :]
    slot_tok = (jnp.sort(keys.reshape(-1)) % jnp.uint32(t)).astype(jnp.int32)
    xb = jnp.take(h, slot_tok, axis=0, mode="clip")
    n_blocks = 2 * t // MOE_ROWS
    n_items = n_blocks + N_EXPERTS - 1
    first_blk = start // MOE_ROWS
    per_expert = jnp.where(counts > 0, (end - 1) // MOE_ROWS - first_blk + 1, 0)
    item_end = jnp.cumsum(per_expert)
    item_start = item_end - per_expert
    idx = jnp.arange(n_items, dtype=jnp.int32)
    item_e = jnp.zeros((n_items,), jnp.int32)
    for e in range(N_EXPERTS - 1):
        item_e = item_e + (idx >= item_end[e]).astype(jnp.int32)
    pick = lambda table: sum(jnp.where(item_e == e, table[e], 0) for e in range(N_EXPERTS))
    valid = idx < item_end[-1]
    item_blk = jnp.where(valid, pick(first_blk) + idx - pick(item_start), n_blocks - 1)
    item_lo = jnp.where(valid, jnp.maximum(pick(start), item_blk * MOE_ROWS), 0)
    item_hi = jnp.where(valid, jnp.minimum(pick(end), (item_blk + 1) * MOE_ROWS), 0)
    yb = _experts(xb, w_gate, w_up, w_down, layer, (item_blk, item_e, item_lo, item_hi))
    y0 = jnp.take(yb, dest[0], axis=0, mode="clip", unique_indices=True)
    y1 = jnp.take(yb, dest[1], axis=0, mode="clip", unique_indices=True)
    return _combine(x2, y0, y1, weights.T, gate, final_gain, final, seq, tm)


def kernel(x, c, norm_gain, w_ada, b_ada, w_in_even, w_in_odd, w_out, a_mu, a_w0, a_w_up, a_a0, a_a_up,
           a_g_up, a_k_k, a_k_a, a_r_k, a_ln_gain, a_ln_bias, b_alpha_up, b_alpha_bias, b_norm_gain, c_lb,
           c_norm_gain, d_conv_w, d_conv_b, d_dt_bias, d_a_log, d_skip, d_norm_gain, w_router, router_bias,
           w_gate, w_up, w_down, final_gain):
    bsz, seq, d = x.shape
    depth = w_ada.shape[0]
    t = bsz * seq
    tm = min(512, seq)
    pair_masks = _chunk_constants()
    ind = _head_indicator(LANE, HEAD_DIM)

    mods = _ada(c, w_ada, b_ada).reshape(depth, bsz, 6, 1, d)
    wr_t = w_router.T
    bias_col = router_bias.reshape(N_EXPERTS, 1)

    x2 = x.reshape(t, d)
    for l in range(depth):
        j = l // 2
        sh_m, sc_m, g_m, sh_f, sc_f, g_f = [mods[l, :, i] for i in range(6)]
        gain_m, gain_f = norm_gain[l, 0].reshape(1, d), norm_gain[l, 1].reshape(1, d)
        if l % 2 == 0:
            wl = w_in_even[j].astype(BF16)
            cut = RWKV_IN + 1024 + GLA_GATE_RANK
            w = jnp.zeros((d, EVEN_COLS), BF16).at[:, :cut].set(wl[:, :cut])
            w = w.at[:, cut + LANE - GLA_GATE_RANK:].set(wl[:, cut:])
            u = _in_proj(x2, gain_m, sc_m, sh_m, w, seq, tm)
            alpha_up = jnp.concatenate([b_alpha_up[j], jnp.zeros((LANE - GLA_GATE_RANK, 256), F32)], axis=0)
            p = dict(mu=_row(a_mu[j]), w0=_row(a_w0[j]), w_up=a_w_up[j], a0=_row(a_a0[j]), a_up=a_a_up[j],
                     g_up=a_g_up[j], k_k=_row(a_k_k[j]), k_a=_row(a_k_a[j]), r_k=_row(a_r_k[j]),
                     ln_gain=_row(a_ln_gain[j]), ln_bias=_row(a_ln_bias[j]), alpha_up=alpha_up,
                     alpha_bias=_row(b_alpha_bias[j]), b_norm_gain=_row(b_norm_gain[j]))
            y = _mix_even(u.reshape(bsz, seq, EVEN_COLS), p, (pair_masks, ind))
        else:
            wl = w_in_odd[j].astype(BF16)
            n_main = wl.shape[1] - SSD_HEADS
            w = jnp.zeros((d, ODD_COLS), BF16).at[:, :n_main].set(wl[:, :n_main])
            w = w.at[:, n_main:].set(jnp.repeat(wl[:, n_main:], HEAD_DIM, axis=1))
            u = _in_proj(x2, gain_m, sc_m, sh_m, w, seq, tm)
            p = dict(c_lb=c_lb.astype(F32), c_norm_gain=_row(c_norm_gain[j]), conv_w=d_conv_w[j],
                     conv_b=_row(d_conv_b[j]), dt_bias=_row(jnp.repeat(d_dt_bias[j], HEAD_DIM)),
                     a_log=_row(jnp.repeat(d_a_log[j], HEAD_DIM)), skip=_row(jnp.repeat(d_skip[j], HEAD_DIM)),
                     d_norm_gain=_row(d_norm_gain[j]))
            y = _mix_odd(u.reshape(bsz, seq, ODD_COLS), p, l, pair_masks)
        x2 = _out_moe(y.reshape(t, d), w_out[l].astype(BF16), x2, g_m, gain_f, sc_f, sh_f, g_f, wr_t, bias_col,
                      w_gate, w_up, w_down, l, final_gain.reshape(1, d), l == depth - 1, seq, tm)
    return x2.reshape(bsz, seq, d)
```
